```python
import math, functools
import jax, jax.numpy as jnp
from jax import lax
import numpy as np

D_MODEL = 2048
BATCH = 8
SEQ = 2048
DEPTH = 1
DEC_BATCH = 128
DEC_SEQ = 8
PAST_LEN = 8192
PAGE_SIZE = 128

HEAD_DIM = 64
N_Q_HEADS = D_MODEL // HEAD_DIM
N_KV_HEADS = N_Q_HEADS // 8
GQA_GROUP = N_Q_HEADS // N_KV_HEADS
WINDOW = 128
D_Q = N_Q_HEADS * HEAD_DIM
D_KV = N_KV_HEADS * HEAD_DIM
POOL_WINDOWS = (2, 4, 8, 16)
N_POOL_GROUPS = len(POOL_WINDOWS)
D_POOL = D_MODEL // 2
POOL_GROUP = D_POOL // N_POOL_GROUPS
POOL_OUT_GROUP = D_MODEL // N_POOL_GROUPS
POOL_HIST = max(POOL_WINDOWS) - 1
D_IN = D_Q + 2 * D_KV + D_POOL + 2 * D_MODEL
D_FF = 4 * D_MODEL
EPS = 1e-6
NEG_INF = -1e30

kernel_name = "hybrid_swa_sink_pool_gated_decoder_step"


def rmsnorm(x, g):
    xf = x.astype(jnp.float32)
    r = lax.rsqrt(jnp.mean(xf * xf, axis=-1, keepdims=True) + EPS)
    return (xf * r * g.astype(jnp.float32)).astype(x.dtype)


def split_proj(h, w_in):
    B, T, _ = h.shape
    z = jnp.einsum('btd,de->bte', h, w_in)
    c = np.cumsum([D_Q, D_KV, D_KV, D_POOL, D_MODEL])
    q, k, v, u, ga, gp = jnp.split(z, list(c), axis=-1)
    q = q.reshape(B, T, N_KV_HEADS, GQA_GROUP, HEAD_DIM)
    k = k.reshape(B, T, N_KV_HEADS, HEAD_DIM)
    v = v.reshape(B, T, N_KV_HEADS, HEAD_DIM)
    return q, k, v, u, ga, gp


def sink_softmax(s, mask, sinks):
    s = jnp.where(mask, s, NEG_INF)
    sk = sinks.astype(jnp.float32)[:, :, None]
    m = jnp.maximum(jnp.max(s, axis=-1), sk)
    p = jnp.exp(s - m[..., None])
    denom = jnp.sum(p, axis=-1) + jnp.exp(sk - m)
    return p / denom[..., None]


def window_attn_prompt(q, k, v, sinks):
    B, S = q.shape[:2]
    nb = S // WINDOW
    scale = HEAD_DIM ** -0.5
    qb = q.reshape(B, nb, WINDOW, N_KV_HEADS, GQA_GROUP, HEAD_DIM)
    kc = k.reshape(B, nb, WINDOW, N_KV_HEADS, HEAD_DIM)
    vc = v.reshape(B, nb, WINDOW, N_KV_HEADS, HEAD_DIM)
    kb = jnp.concatenate([jnp.concatenate([jnp.zeros_like(kc[:, :1]), kc[:, :-1]], 1), kc], axis=2)
    vb = jnp.concatenate([jnp.concatenate([jnp.zeros_like(vc[:, :1]), vc[:, :-1]], 1), vc], axis=2)
    s = jnp.einsum('bnqhgd,bnkhd->bnhgqk', qb.astype(jnp.float32), kb.astype(jnp.float32)) * scale
    i = jnp.arange(WINDOW)[:, None] + WINDOW
    j = jnp.arange(2 * WINDOW)[None, :]
    rel = i - j
    band = (rel >= 0) & (rel < WINDOW)
    has_prev = (jnp.arange(nb)[:, None, None] > 0) | (j >= WINDOW)[None]
    mask = (band[None] & has_prev)[None, :, None, None]
    probs = sink_softmax(s, mask, sinks)
    out = jnp.einsum('bnhgqk,bnkhd->bnqhgd', probs, vb.astype(jnp.float32))
    wb = min(WINDOW, S)
    return out.reshape(B, S, D_Q).astype(q.dtype), k[:, S - wb:], v[:, S - wb:]


def window_attn_sample(q, k, v, k_hist, v_hist, sinks):
    B, T = q.shape[:2]
    wb = k_hist.shape[1]
    scale = HEAD_DIM ** -0.5
    k_ext = jnp.concatenate([k_hist, k.astype(k_hist.dtype)], axis=1)
    v_ext = jnp.concatenate([v_hist, v.astype(v_hist.dtype)], axis=1)
    s = jnp.einsum('bqhgd,bkhd->bhgqk', q.astype(jnp.float32), k_ext.astype(jnp.float32)) * scale
    rel = (wb + jnp.arange(T))[:, None] - jnp.arange(wb + T)[None, :]
    mask = ((rel >= 0) & (rel < WINDOW))[None, None, None]
    probs = sink_softmax(s, mask, sinks)
    out = jnp.einsum('bhgqk,bkhd->bqhgd', probs, v_ext.astype(jnp.float32))
    return out.reshape(B, T, D_Q).astype(q.dtype), k_ext[:, T:], v_ext[:, T:]


def pool_branch(u_hist, u, pos0, w_pool, pool_scale):
    B, T, _ = u.shape
    P = u_hist.shape[1]
    u_ext = jnp.concatenate([u_hist.astype(u.dtype), u], axis=1)
    uf = u_ext.astype(jnp.float32)
    cs = jnp.concatenate([jnp.zeros((B, 1, D_POOL), jnp.float32), jnp.cumsum(uf, axis=1)], axis=1)
    end = P + jnp.arange(T) + 1
    pos = pos0 + jnp.arange(T)
    u_new = uf[:, P:]
    outs = []
    for g, w in enumerate(POOL_WINDOWS):
        sl = slice(g * POOL_GROUP, (g + 1) * POOL_GROUP)
        wsum = cs[:, end, sl] - cs[:, end - w, sl]
        cnt = jnp.minimum(pos + 1, w).astype(jnp.float32)[None, :, None]
        pooled = wsum / cnt - u_new[:, :, sl]
        outs.append(jnp.einsum('btc,cd->btd', pooled, w_pool[g].astype(jnp.float32)))
    p = jnp.concatenate(outs, axis=-1) * pool_scale.astype(jnp.float32)
    return p.astype(u.dtype), u_ext[:, P + T - POOL_HIST:]


def decoder_layer(x, attn_fn, u_hist, pos0, norm_attn_pre, norm_attn_post, w_in, w_pool,
                  pool_scale, w_out, norm_mlp_pre, norm_mlp_post, w_up, w_down):
    h = rmsnorm(x, norm_attn_pre)
    q, k, v, u, ga, gp = split_proj(h, w_in)
    a, k_state, v_state = attn_fn(q, k, v)
    p, u_state = pool_branch(u_hist, u, pos0, w_pool, pool_scale)
    mixed = jax.nn.sigmoid(ga) * a + jax.nn.sigmoid(gp) * p
    x = x + rmsnorm(jnp.einsum('btd,de->bte', mixed, w_out), norm_attn_post)
    h2 = rmsnorm(x, norm_mlp_pre)
    f = jnp.einsum('btf,fd->btd', jnp.square(jax.nn.relu(jnp.einsum('btd,df->btf', h2, w_up))), w_down)
    x = x + rmsnorm(f, norm_mlp_post)
    return x, k_state, v_state, u_state


def setup_inputs(seed: int = 0) -> dict:
    key = jax.random.key(seed)
    ks = jax.random.split(key, 16)
    wb = min(WINDOW, PAST_LEN)
    f32 = jnp.float32
    nrm = lambda k, shape, s: jax.random.normal(k, shape, f32) * s
    return {
        "x_prompt": nrm(ks[0], (BATCH, SEQ, D_MODEL), 1.0),
        "x_sample": nrm(ks[1], (DEC_BATCH, DEC_SEQ, D_MODEL), 1.0),
        "cache_k_win": nrm(ks[2], (DEPTH, DEC_BATCH, wb, N_KV_HEADS, HEAD_DIM), 1.0),
        "cache_v_win": nrm(ks[3], (DEPTH, DEC_BATCH, wb, N_KV_HEADS, HEAD_DIM), 1.0),
        "state_pool": nrm(ks[4], (DEPTH, DEC_BATCH, POOL_HIST, D_POOL), 1.0),
        "norm_attn_pre": 1.0 + nrm(ks[5], (DEPTH, D_MODEL), 0.02),
        "norm_attn_post": 1.0 + nrm(ks[6], (DEPTH, D_MODEL), 0.02),
        "w_in": nrm(ks[7], (DEPTH, D_MODEL, D_IN), D_MODEL ** -0.5),
        "attn_sinks": nrm(ks[8], (DEPTH, N_KV_HEADS, GQA_GROUP), 0.5),
        "w_pool": nrm(ks[9], (DEPTH, N_POOL_GROUPS, POOL_GROUP, POOL_OUT_GROUP), POOL_GROUP ** -0.5),
        "pool_scale": 1.0 + nrm(ks[10], (DEPTH, D_MODEL), 0.1),
        "w_out": nrm(ks[11], (DEPTH, D_MODEL, D_MODEL), D_MODEL ** -0.5),
        "norm_mlp_pre": 1.0 + nrm(ks[12], (DEPTH, D_MODEL), 0.02),
        "norm_mlp_post": 1.0 + nrm(ks[13], (DEPTH, D_MODEL), 0.02),
        "w_up": nrm(ks[14], (DEPTH, D_MODEL, D_FF), D_MODEL ** -0.5),
        "w_down": nrm(ks[15], (DEPTH, D_FF, D_MODEL), D_FF ** -0.5),
    }


def reference(x_prompt, x_sample, cache_k_win, cache_v_win, state_pool, norm_attn_pre,
              norm_attn_post, w_in, attn_sinks, w_pool, pool_scale, w_out, norm_mlp_pre,
              norm_mlp_post, w_up, w_down):
    xp, xs = x_prompt, x_sample
    kp_l, vp_l, up_l, ks_l, vs_l, us_l = [], [], [], [], [], []
    for l in range(DEPTH):
        params = (norm_attn_pre[l], norm_attn_post[l], w_in[l], w_pool[l], pool_scale[l],
                  w_out[l], norm_mlp_pre[l], norm_mlp_post[l], w_up[l], w_down[l])
        u0 = jnp.zeros((xp.shape[0], POOL_HIST, D_POOL), xp.dtype)
        attn_p = functools.partial(window_attn_prompt, sinks=attn_sinks[l])
        xp, kp, vp, up = decoder_layer(xp, attn_p, u0, 0, *params)
        attn_s = functools.partial(window_attn_sample, k_hist=cache_k_win[l],
                                   v_hist=cache_v_win[l], sinks=attn_sinks[l])
        xs, kss, vss, uss = decoder_layer(xs, attn_s, state_pool[l], PAST_LEN, *params)
        kp_l.append(kp); vp_l.append(vp); up_l.append(up)
        ks_l.append(kss); vs_l.append(vss); us_l.append(uss)
    y_prompt, y_sample = xp, xs
    return (y_prompt, y_sample, jnp.stack(kp_l), jnp.stack(vp_l), jnp.stack(up_l),
            jnp.stack(ks_l), jnp.stack(vs_l), jnp.stack(us_l))
```

```python
import functools

import jax
import jax.numpy as jnp
from jax import lax
from jax.experimental import pallas as pl
from jax.experimental.pallas import tpu as pltpu

F32 = jnp.float32
BF16 = jnp.bfloat16

D_MODEL = 2048
HEAD_DIM = 64
N_KV_HEADS = 4
GQA_GROUP = 8
WINDOW = 128
D_Q = 2048
D_KV = 256
POOL_WINDOWS = (2, 4, 8, 16)
D_POOL = 1024
POOL_GROUP = 256
POOL_OUT_GROUP = 512
POOL_HIST = 15
D_FF = 8192
D_IN = D_Q + 2 * D_KV + D_POOL + 2 * D_MODEL
PAST_LEN = 8192
EPS = 1e-6
NEG_INF = -1e30
SCALE = HEAD_DIM ** -0.5

OFF_Q, OFF_GA, OFF_GP, OFF_U, OFF_K, OFF_V = 0, 2048, 4096, 6144, 7168, 7424
HALO = 16

VMEM_LIMIT = 56 * 1024 * 1024


def _rmsnorm(x, g):
    r = lax.rsqrt(jnp.mean(x * x, axis=-1, keepdims=True) + EPS)
    return x * r * g


def _const_spec(shape):
    return pl.BlockSpec(shape, lambda *_: (0,) * len(shape), pipeline_mode=pl.Buffered(1))


def _in_proj_kernel(x_ref, g_ref, w_ref, z_ref, h_ref):
    @pl.when(pl.program_id(1) == 0)
    def _():
        h_ref[...] = _rmsnorm(x_ref[...], g_ref[...]).astype(BF16)

    z_ref[...] = jnp.dot(h_ref[...], w_ref[...], preferred_element_type=F32)


def _in_proj(x2d, g, w_bf, *, bm, bn):
    m = x2d.shape[0]
    return pl.pallas_call(
        _in_proj_kernel,
        grid=(m // bm, D_IN // bn),
        in_specs=[
            pl.BlockSpec((bm, D_MODEL), lambda i, j: (i, 0)),
            _const_spec((1, D_MODEL)),
            pl.BlockSpec((D_MODEL, bn), lambda i, j: (0, j)),
        ],
        out_specs=pl.BlockSpec((bm, bn), lambda i, j: (i, j)),
        out_shape=jax.ShapeDtypeStruct((m, D_IN), F32),
        scratch_shapes=[pltpu.VMEM((bm, D_MODEL), BF16)],
        compiler_params=pltpu.CompilerParams(
            dimension_semantics=("parallel", "arbitrary"), vmem_limit_bytes=VMEM_LIMIT),
        name="in_proj",
    )(x2d, g, w_bf)


def _attn_core(q, k_all, v_all, sinks_ref, mask, keep_prev, t):
    qs = (q * SCALE).astype(BF16)
    outs = []
    for h in range(N_KV_HEADS):
        qh = jnp.concatenate(
            [qs[:, (h * GQA_GROUP + g) * HEAD_DIM:(h * GQA_GROUP + g + 1) * HEAD_DIM]
             for g in range(GQA_GROUP)], axis=0)
        kh = k_all[:, h * HEAD_DIM:(h + 1) * HEAD_DIM]
        s_all = lax.dot_general(qh, kh, (((1,), (1,)), ((), ())),
                                preferred_element_type=F32)
        s = jnp.where(mask, s_all[:, WINDOW:], s_all[:, :WINDOW])
        if keep_prev is not None:
            s = jnp.where(jnp.logical_or(mask, keep_prev), s, NEG_INF)
        sk = jnp.concatenate(
            [jnp.full((t, 1), sinks_ref[h, g], F32) for g in range(GQA_GROUP)], axis=0)
        m = jnp.maximum(jnp.max(s, axis=-1, keepdims=True), sk)
        p = jnp.exp(s - m)
        denom = jnp.sum(p, axis=-1, keepdims=True) + jnp.exp(sk - m)
        probs = p * (1.0 / denom)
        p_all = jnp.concatenate(
            [jnp.where(mask, 0.0, probs), jnp.where(mask, probs, 0.0)], axis=1).astype(BF16)
        oh = jnp.dot(p_all, v_all[:, h * HEAD_DIM:(h + 1) * HEAD_DIM],
                     preferred_element_type=F32)
        outs += [oh[g * t:(g + 1) * t] for g in range(GQA_GROUP)]
    return jnp.concatenate(outs, axis=1)


def _causal_mask(t):
    rows = lax.broadcasted_iota(jnp.int32, (GQA_GROUP * t, WINDOW), 0) % t
    lanes = lax.broadcasted_iota(jnp.int32, (GQA_GROUP * t, WINDOW), 1)
    return lanes <= rows


def _attn_prompt_kernel(sinks_ref, q_ref, kp_ref, kc_ref, vp_ref, vc_ref,
                        a_ref, kwin_ref, vwin_ref):
    n = pl.program_id(1)
    kc = kc_ref[0]
    vc = vc_ref[0]
    k_all = jnp.concatenate([kp_ref[0], kc], axis=0).astype(BF16)
    v_all = jnp.concatenate([vp_ref[0], vc], axis=0).astype(BF16)
    a_ref[0] = _attn_core(q_ref[0], k_all, v_all, sinks_ref, _causal_mask(WINDOW), n > 0, WINDOW)

    @pl.when(n == pl.num_programs(1) - 1)
    def _():
        kwin_ref[0] = kc
        vwin_ref[0] = vc


def _attn_prompt(z3, sinks):
    b, s, _ = z3.shape
    nb = s // WINDOW
    kcol, vcol = OFF_K // D_KV, OFF_V // D_KV
    prev = lambda col: (lambda bi, n: (bi, jnp.maximum(n - 1, 0), col))
    cur = lambda col: (lambda bi, n: (bi, n, col))
    win_spec = pl.BlockSpec((1, WINDOW, D_KV), lambda bi, n: (bi, 0, 0))
    return pl.pallas_call(
        _attn_prompt_kernel,
        grid=(b, nb),
        in_specs=[
            pl.BlockSpec(memory_space=pltpu.SMEM),
            pl.BlockSpec((1, WINDOW, D_Q), lambda bi, n: (bi, n, 0)),
            pl.BlockSpec((1, WINDOW, D_KV), prev(kcol)),
            pl.BlockSpec((1, WINDOW, D_KV), cur(kcol)),
            pl.BlockSpec((1, WINDOW, D_KV), prev(vcol)),
            pl.BlockSpec((1, WINDOW, D_KV), cur(vcol)),
        ],
        out_specs=[pl.BlockSpec((1, WINDOW, D_Q), lambda bi, n: (bi, n, 0)), win_spec, win_spec],
        out_shape=[jax.ShapeDtypeStruct((b, s, D_Q), F32),
                   jax.ShapeDtypeStruct((b, WINDOW, D_KV), F32),
                   jax.ShapeDtypeStruct((b, WINDOW, D_KV), F32)],
        compiler_params=pltpu.CompilerParams(
            dimension_semantics=("parallel", "arbitrary"), vmem_limit_bytes=VMEM_LIMIT),
        name="attn_prompt",
    )(sinks, z3, z3, z3, z3, z3)


def _attn_sample_kernel(sinks_ref, q_ref, kn_ref, vn_ref, ck_ref, cv_ref,
                        a_ref, kwin_ref, vwin_ref, *, bt, t):
    mask = _causal_mask(t)
    pad = jnp.zeros((WINDOW - t, D_KV), F32)

    def body(s, carry):
        kn, vn, ck, cv = kn_ref[s], vn_ref[s], ck_ref[s], cv_ref[s]
        k_all = jnp.concatenate([ck, kn, pad], axis=0).astype(BF16)
        v_all = jnp.concatenate([cv, vn, pad], axis=0).astype(BF16)
        a_ref[s] = _attn_core(q_ref[s], k_all, v_all, sinks_ref, mask, None, t)
        kwin_ref[s, 0:WINDOW - t, :] = ck[t:, :]
        kwin_ref[s, WINDOW - t:WINDOW, :] = kn
        vwin_ref[s, 0:WINDOW - t, :] = cv[t:, :]
        vwin_ref[s, WINDOW - t:WINDOW, :] = vn
        return carry

    lax.fori_loop(0, bt, body, 0)


def _attn_sample(z3, cache_k, cache_v, sinks, *, bt):
    b, t, _ = z3.shape
    kcol, vcol = OFF_K // D_KV, OFF_V // D_KV
    cache_spec = pl.BlockSpec((bt, WINDOW, D_KV), lambda i: (i, 0, 0))
    return pl.pallas_call(
        functools.partial(_attn_sample_kernel, bt=bt, t=t),
        grid=(b // bt,),
        in_specs=[
            pl.BlockSpec(memory_space=pltpu.SMEM),
            pl.BlockSpec((bt, t, D_Q), lambda i: (i, 0, 0)),
            pl.BlockSpec((bt, t, D_KV), lambda i: (i, 0, kcol)),
            pl.BlockSpec((bt, t, D_KV), lambda i: (i, 0, vcol)),
            cache_spec, cache_spec,
        ],
        out_specs=[pl.BlockSpec((bt, t, D_Q), lambda i: (i, 0, 0)), cache_spec, cache_spec],
        out_shape=[jax.ShapeDtypeStruct((b, t, D_Q), F32),
                   jax.ShapeDtypeStruct((b, WINDOW, D_KV), F32),
                   jax.ShapeDtypeStruct((b, WINDOW, D_KV), F32)],
        compiler_params=pltpu.CompilerParams(
            dimension_semantics=("parallel",), vmem_limit_bytes=VMEM_LIMIT),
        name="attn_sample",
    )(sinks, z3, z3, z3, cache_k, cache_v)


def _mix_tail(pooled_parts, a, ga, gp, x, wpool_ref, pscale_ref, wout_ref, gpost_ref, gpre_ref,
              x1_ref, h2_ref):
    parts = [jnp.dot(pooled_parts[g].astype(BF16), wpool_ref[g], preferred_element_type=F32)
             for g in range(len(POOL_WINDOWS))]
    p = jnp.concatenate(parts, axis=-1) * pscale_ref[...]
    mixed = jax.nn.sigmoid(ga) * a + jax.nn.sigmoid(gp) * p
    y = jnp.dot(mixed.astype(BF16), wout_ref[...], preferred_element_type=F32)
    x1 = x + _rmsnorm(y, gpost_ref[...])
    x1_ref[...] = x1
    h2_ref[...] = _rmsnorm(x1, gpre_ref[...]).astype(BF16)


def _mix_prompt_kernel(a_ref, ga_ref, gp_ref, u_ref, halo_ref, x_ref, wpool_ref, pscale_ref,
                       wout_ref, gpost_ref, gpre_ref, x1_ref, h2_ref, pool_ref, ext_ref,
                       *, bm, blocks_per_seq):
    blk = pl.program_id(0) % blocks_per_seq
    ext_ref[0:HALO, :] = jnp.where(blk == 0, 0.0, halo_ref[...])
    ext_ref[HALO:, :] = u_ref[...]
    pos = blk * bm + lax.broadcasted_iota(jnp.int32, (bm, 1), 0)
    pooled = []
    for g, w in enumerate(POOL_WINDOWS):
        lanes = pl.ds(g * POOL_GROUP, POOL_GROUP)
        wsum = ext_ref[pl.ds(HALO, bm), lanes]
        for s in range(1, w):
            wsum = wsum + ext_ref[pl.ds(HALO - s, bm), lanes]
        inv_cnt = 1.0 / jnp.minimum(pos + 1, w).astype(F32)
        pooled.append(wsum * inv_cnt - u_ref[:, lanes])
    _mix_tail(pooled, a_ref[...], ga_ref[...], gp_ref[...], x_ref[...], wpool_ref, pscale_ref,
              wout_ref, gpost_ref, gpre_ref, x1_ref, h2_ref)

    @pl.when(blk == blocks_per_seq - 1)
    def _():
        pool_ref[0] = ext_ref[pl.ds(HALO + bm - POOL_HIST, POOL_HIST), :]


def _mix_prompt(a2d, z2d, x2d, wpool_bf, pscale, wout_bf, gpost, gpre, *, bm, seq):
    m = x2d.shape[0]
    blocks_per_seq = seq // bm
    row_spec = lambda col: pl.BlockSpec((bm, D_MODEL), lambda i: (i, col))
    halo_blocks = bm // HALO
    return pl.pallas_call(
        functools.partial(_mix_prompt_kernel, bm=bm, blocks_per_seq=blocks_per_seq),
        grid=(m // bm,),
        in_specs=[
            row_spec(0),
            row_spec(OFF_GA // D_MODEL),
            row_spec(OFF_GP // D_MODEL),
            pl.BlockSpec((bm, D_POOL), lambda i: (i, OFF_U // D_POOL)),
            pl.BlockSpec((HALO, D_POOL),
                         lambda i: (jnp.maximum(i * halo_blocks - 1, 0), OFF_U // D_POOL)),
            row_spec(0),
            _const_spec((len(POOL_WINDOWS), POOL_GROUP, POOL_OUT_GROUP)),
            _const_spec((1, D_MODEL)),
            _const_spec((D_MODEL, D_MODEL)),
            _const_spec((1, D_MODEL)),
            _const_spec((1, D_MODEL)),
        ],
        out_specs=[row_spec(0), row_spec(0),
                   pl.BlockSpec((1, POOL_HIST, D_POOL), lambda i: (i // blocks_per_seq, 0, 0))],
        out_shape=[jax.ShapeDtypeStruct((m, D_MODEL), F32),
                   jax.ShapeDtypeStruct((m, D_MODEL), BF16),
                   jax.ShapeDtypeStruct((m // seq, POOL_HIST, D_POOL), F32)],
        scratch_shapes=[pltpu.VMEM((HALO + bm, D_POOL), F32)],
        compiler_params=pltpu.CompilerParams(
            dimension_semantics=("arbitrary",), vmem_limit_bytes=VMEM_LIMIT),
        name="mix_prompt",
    )(a2d, z2d, z2d, z2d, z2d, x2d, wpool_bf, pscale, wout_bf, gpost, gpre)


def _mix_sample_kernel(a_ref, ga_ref, gp_ref, u_ref, hist_ref, x_ref, wpool_ref, pscale_ref,
                       wout_ref, gpost_ref, gpre_ref, x1_ref, h2_ref, pool_ref, ext_ref,
                       *, bt, t, pos0):
    ext_ref[:, 0:HALO, :] = hist_ref[...]
    ext_ref[:, HALO:, :] = u_ref[...].reshape(bt, t, D_POOL)
    pooled = []
    for g, w in enumerate(POOL_WINDOWS):
        lanes = pl.ds(g * POOL_GROUP, POOL_GROUP)
        wsum = ext_ref[:, pl.ds(HALO, t), lanes]
        for s in range(1, w):
            wsum = wsum + ext_ref[:, pl.ds(HALO - s, t), lanes]
        assert pos0 + 1 >= w
        pooled.append(wsum.reshape(bt * t, POOL_GROUP) * (1.0 / w) - u_ref[:, lanes])
    _mix_tail(pooled, a_ref[...], ga_ref[...], gp_ref[...], x_ref[...], wpool_ref, pscale_ref,
              wout_ref, gpost_ref, gpre_ref, x1_ref, h2_ref)
    pool_ref[...] = ext_ref[:, pl.ds(HALO + t - POOL_HIST, POOL_HIST), :]


def _mix_sample(a2d, z2d, x2d, hist16, wpool_bf, pscale, wout_bf, gpost, gpre, *, bt, t):
    m = x2d.shape[0]
    bm = bt * t
    row_spec = lambda col: pl.BlockSpec((bm, D_MODEL), lambda i: (i, col))
    return pl.pallas_call(
        functools.partial(_mix_sample_kernel, bt=bt, t=t, pos0=PAST_LEN),
        grid=(m // bm,),
        in_specs=[
            row_spec(0),
            row_spec(OFF_GA // D_MODEL),
            row_spec(OFF_GP // D_MODEL),
            pl.BlockSpec((bm, D_POOL), lambda i: (i, OFF_U // D_POOL)),
            pl.BlockSpec((bt, HALO, D_POOL), lambda i: (i, 0, 0)),
            row_spec(0),
            _const_spec((len(POOL_WINDOWS), POOL_GROUP, POOL_OUT_GROUP)),
            _const_spec((1, D_MODEL)),
            _const_spec((D_MODEL, D_MODEL)),
            _const_spec((1, D_MODEL)),
            _const_spec((1, D_MODEL)),
        ],
        out_specs=[row_spec(0), row_spec(0),
                   pl.BlockSpec((bt, POOL_HIST, D_POOL), lambda i: (i, 0, 0))],
        out_shape=[jax.ShapeDtypeStruct((m, D_MODEL), F32),
                   jax.ShapeDtypeStruct((m, D_MODEL), BF16),
                   jax.ShapeDtypeStruct((m // t, POOL_HIST, D_POOL), F32)],
        scratch_shapes=[pltpu.VMEM((bt, HALO + t, D_POOL), F32)],
        compiler_params=pltpu.CompilerParams(
            dimension_semantics=("parallel",), vmem_limit_bytes=VMEM_LIMIT),
        name="mix_sample",
    )(a2d, z2d, z2d, z2d, hist16, x2d, wpool_bf, pscale, wout_bf, gpost, gpre)


def _mlp_kernel(h2_ref, x1_ref, wup_ref, wdn_ref, g_ref, y_ref, acc_ref):
    j = pl.program_id(1)
    hid = jnp.dot(h2_ref[...], wup_ref[...], preferred_element_type=F32)
    hid = jnp.square(jnp.maximum(hid, 0.0)).astype(BF16)
    part = jnp.dot(hid, wdn_ref[...], preferred_element_type=F32)

    @pl.when(j == 0)
    def _():
        acc_ref[...] = part

    @pl.when(j > 0)
    def _():
        acc_ref[...] += part

    @pl.when(j == pl.num_programs(1) - 1)
    def _():
        y_ref[...] = x1_ref[...] + _rmsnorm(acc_ref[...], g_ref[...])


def _mlp(h2, x1, wup_bf, wdn_bf, g, *, bm, fc):
    m = x1.shape[0]
    return pl.pallas_call(
        _mlp_kernel,
        grid=(m // bm, D_FF // fc),
        in_specs=[
            pl.BlockSpec((bm, D_MODEL), lambda i, j: (i, 0)),
            pl.BlockSpec((bm, D_MODEL), lambda i, j: (i, 0)),
            pl.BlockSpec((D_MODEL, fc), lambda i, j: (0, j)),
            pl.BlockSpec((fc, D_MODEL), lambda i, j: (j, 0)),
            _const_spec((1, D_MODEL)),
        ],
        out_specs=pl.BlockSpec((bm, D_MODEL), lambda i, j: (i, 0)),
        out_shape=jax.ShapeDtypeStruct((m, D_MODEL), F32),
        scratch_shapes=[pltpu.VMEM((bm, D_MODEL), F32)],
        compiler_params=pltpu.CompilerParams(
            dimension_semantics=("parallel", "arbitrary"), vmem_limit_bytes=VMEM_LIMIT),
        name="mlp",
    )(h2, x1, wup_bf, wdn_bf, g)


def kernel(x_prompt, x_sample, cache_k_win, cache_v_win, state_pool, norm_attn_pre, norm_attn_post,
           w_in, attn_sinks, w_pool, pool_scale, w_out, norm_mlp_pre, norm_mlp_post, w_up, w_down):
    depth = w_in.shape[0]
    assert depth == 1
    b, s, _ = x_prompt.shape
    bs, t, _ = x_sample.shape

    l = 0
    wi = w_in[l]
    c = [0, D_Q, D_Q + D_KV, D_Q + 2 * D_KV, D_Q + 2 * D_KV + D_POOL,
         D_Q + 2 * D_KV + D_POOL + D_MODEL, D_IN]
    sl = lambda i: wi[:, c[i]:c[i + 1]]
    w_in_bf = jnp.concatenate([sl(0), sl(4), sl(5), sl(3), sl(1), sl(2)], axis=1).astype(BF16)
    w_pool_bf = w_pool[l].astype(BF16)
    w_out_bf = w_out[l].astype(BF16)
    w_up_bf = w_up[l].astype(BF16)
    w_down_bf = w_down[l].astype(BF16)
    row = lambda v: v[l].reshape(1, D_MODEL)
    g_attn_pre, g_attn_post = row(norm_attn_pre), row(norm_attn_post)
    g_mlp_pre, g_mlp_post = row(norm_mlp_pre), row(norm_mlp_post)
    pscale = row(pool_scale)
    sinks = attn_sinks[l]

    xp = x_prompt.reshape(b * s, D_MODEL)
    zp = _in_proj(xp, g_attn_pre, w_in_bf, bm=1024, bn=1536)
    ap, kp, vp = _attn_prompt(zp.reshape(b, s, D_IN), sinks)
    x1p, h2p, poolp = _mix_prompt(ap.reshape(b * s, D_Q), zp, xp, w_pool_bf, pscale, w_out_bf,
                                  g_attn_post, g_mlp_pre, bm=256, seq=s)
    yp = _mlp(h2p, x1p, w_up_bf, w_down_bf, g_mlp_post, bm=512, fc=1024)

    xs = x_sample.reshape(bs * t, D_MODEL)
    zs = _in_proj(xs, g_attn_pre, w_in_bf, bm=1024, bn=1536)
    ck = cache_k_win[l].reshape(bs, WINDOW, D_KV)
    cv = cache_v_win[l].reshape(bs, WINDOW, D_KV)
    as_, ks, vs = _attn_sample(zs.reshape(bs, t, D_IN), ck, cv, sinks, bt=8)
    hist16 = jnp.pad(state_pool[l], ((0, 0), (HALO - POOL_HIST, 0), (0, 0)))
    x1s, h2s, pools = _mix_sample(as_.reshape(bs * t, D_Q), zs, xs, hist16, w_pool_bf, pscale,
                                  w_out_bf, g_attn_post, g_mlp_pre, bt=32, t=t)
    ys = _mlp(h2s, x1s, w_up_bf, w_down_bf, g_mlp_post, bm=512, fc=1024)

    kv_shape = lambda nb: (1, nb, WINDOW, N_KV_HEADS, HEAD_DIM)
    return (yp.reshape(b, s, D_MODEL), ys.reshape(bs, t, D_MODEL),
            kp.reshape(kv_shape(b)), vp.reshape(kv_shape(b)), poolp[None],
            ks.reshape(kv_shape(bs)), vs.reshape(kv_shape(bs)), pools[None])
```

```python
import functools

import jax
import jax.numpy as jnp
from jax import lax
from jax.experimental import pallas as pl
from jax.experimental.pallas import tpu as pltpu

F32 = jnp.float32
BF16 = jnp.bfloat16

D_MODEL = 2048
HEAD_DIM = 64
N_KV_HEADS = 4
GQA_GROUP = 8
N_Q_HEADS = N_KV_HEADS * GQA_GROUP
WINDOW = 128
D_Q = 2048
D_KV = 256
POOL_WINDOWS = (2, 4, 8, 16)
D_POOL = 1024
POOL_GROUP = 256
POOL_OUT_GROUP = 512
POOL_HIST = 15
D_FF = 8192
D_IN = D_Q + 2 * D_KV + D_POOL + 2 * D_MODEL
PAST_LEN = 8192
EPS = 1e-6
NEG_INF = -1e30
SCALE = HEAD_DIM ** -0.5

LANES = 128
PAIR = 2
N_PAIRS = GQA_GROUP // PAIR
ROW_CHUNK = 16

OFF_Q, OFF_GA, OFF_GP, OFF_U, OFF_K, OFF_V = 0, 2048, 4096, 6144, 7168, 7424
HALO = 16

VMEM_LIMIT = 56 * 1024 * 1024

NT_DIMS = (((1,), (1,)), ((), ()))
TN_DIMS = (((0,), (0,)), ((), ()))


def _rmsnorm(x, g):
    r = lax.rsqrt(jnp.mean(x * x, axis=-1, keepdims=True) + EPS)
    return x * r * g


def _const_spec(shape):
    return pl.BlockSpec(shape, lambda *_: (0,) * len(shape), pipeline_mode=pl.Buffered(1))


def _in_proj_kernel(x_ref, g_ref, w_ref, z_ref, h_ref):
    @pl.when(pl.program_id(1) == 0)
    def _():
        h_ref[...] = _rmsnorm(x_ref[...], g_ref[...]).astype(BF16)

    z_ref[...] = jnp.dot(h_ref[...], w_ref[...], preferred_element_type=F32)


def _in_proj(x2d, g, w_bf, *, bm, bn):
    m = x2d.shape[0]
    return pl.pallas_call(
        _in_proj_kernel,
        grid=(m // bm, D_IN // bn),
        in_specs=[
            pl.BlockSpec((bm, D_MODEL), lambda i, j: (i, 0)),
            _const_spec((1, D_MODEL)),
            pl.BlockSpec((D_MODEL, bn), lambda i, j: (0, j)),
        ],
        out_specs=pl.BlockSpec((bm, bn), lambda i, j: (i, j)),
        out_shape=jax.ShapeDtypeStruct((m, D_IN), F32),
        scratch_shapes=[pltpu.VMEM((bm, D_MODEL), BF16)],
        compiler_params=pltpu.CompilerParams(
            dimension_semantics=("parallel", "arbitrary"), vmem_limit_bytes=VMEM_LIMIT),
        name="in_proj",
    )(x2d, g, w_bf)


def _own_and_swapped(slab, odd, lo):
    own = jnp.where(lo != odd, slab, 0.0)
    swapped = pltpu.roll(own, HEAD_DIM, axis=1)
    return (swapped, own) if odd else (own, swapped)


def _attn_prompt_kernel(sinks_ref, q_ref, kp_ref, kc_ref, vp_ref, vc_ref,
                        a_ref, kwin_ref, vwin_ref, s_ref, p_ref, e_ref):
    n = pl.program_id(1)
    kc = kc_ref[0]
    vc = vc_ref[0]
    k_all = jnp.concatenate([kp_ref[0], kc], axis=0)
    v_all = jnp.concatenate([vp_ref[0], vc], axis=0)
    prev_bias = jnp.where(n > 0, 0.0, NEG_INF).astype(F32)
    lo_kv = lax.broadcasted_iota(jnp.int32, (2 * WINDOW, LANES), 1) < HEAD_DIM
    ones_lo = jnp.where(lo_kv, 1.0, 0.0)
    ones_hi = 1.0 - ones_lo
    lo_c = lax.broadcasted_iota(jnp.int32, (ROW_CHUNK, LANES), 1) < HEAD_DIM
    lane_c = lax.broadcasted_iota(jnp.int32, (ROW_CHUNK, LANES), 1)
    row_c = lax.broadcasted_iota(jnp.int32, (ROW_CHUNK, LANES), 0)

    for h in range(N_KV_HEADS):
        col = pl.ds((h // PAIR) * LANES, LANES)
        odd = (h % PAIR) == 1
        k_l, k_r = _own_and_swapped(k_all[:, (h // PAIR) * LANES:(h // PAIR + 1) * LANES], odd, lo_kv)
        v_l, v_r = _own_and_swapped(v_all[:, (h // PAIR) * LANES:(h // PAIR + 1) * LANES], odd, lo_kv)
        wk = jnp.concatenate([k_l, k_r], axis=0).astype(BF16)
        vext = jnp.concatenate(
            [jnp.concatenate([v_l, ones_lo], axis=1),
             jnp.concatenate([v_r, ones_hi], axis=1)], axis=0).astype(BF16)
        qh = jnp.concatenate(
            [q_ref[0, :, pl.ds((h * N_PAIRS + j) * LANES, LANES)] for j in range(N_PAIRS)], axis=0)
        qh = (qh * SCALE).astype(BF16)
        s_ref[h] = lax.dot_general(qh, wk, NT_DIMS, preferred_element_type=F32)

        for j in range(N_PAIRS):
            sinks = [sinks_ref[h, PAIR * j], sinks_ref[h, PAIR * j + 1]]
            for c in range(WINDOW // ROW_CHUNK):
                rows = pl.ds(j * WINDOW + c * ROW_CHUNK, ROW_CHUNK)
                mask = lane_c <= (row_c + c * ROW_CHUNK)
                e_parts = []
                for gi in range(PAIR):
                    base = gi * 2 * WINDOW
                    prev = s_ref[h, rows, pl.ds(base, WINDOW)] + prev_bias
                    cur = s_ref[h, rows, pl.ds(base + WINDOW, WINDOW)]
                    s = jnp.where(mask, cur, prev)
                    m = jnp.maximum(jnp.max(s, axis=-1, keepdims=True), sinks[gi])
                    p = jnp.exp(s - m)
                    p_ref[h, rows, pl.ds(base, WINDOW)] = jnp.where(mask, 0.0, p).astype(BF16)
                    p_ref[h, rows, pl.ds(base + WINDOW, WINDOW)] = jnp.where(mask, p, 0.0).astype(BF16)
                    e_parts.append(jnp.broadcast_to(jnp.exp(sinks[gi] - m), (ROW_CHUNK, LANES)))
                e_ref[h, rows, :] = jnp.where(lo_c, e_parts[0], e_parts[1])

        o_ext = jnp.dot(p_ref[h], vext, preferred_element_type=F32)
        o = o_ext[:, :LANES] / (o_ext[:, LANES:] + e_ref[h])
        for j in range(N_PAIRS):
            a_ref[0, :, pl.ds((h * N_PAIRS + j) * LANES, LANES)] = o[j * WINDOW:(j + 1) * WINDOW]

    @pl.when(n == pl.num_programs(1) - 1)
    def _():
        kwin_ref[0] = kc
        vwin_ref[0] = vc


def _attn_prompt(z3, sinks):
    b, s, _ = z3.shape
    nb = s // WINDOW
    kcol, vcol = OFF_K // D_KV, OFF_V // D_KV
    prev = lambda col: (lambda bi, n: (bi, jnp.maximum(n - 1, 0), col))
    cur = lambda col: (lambda bi, n: (bi, n, col))
    win_spec = pl.BlockSpec((1, WINDOW, D_KV), lambda bi, n: (bi, 0, 0))
    rows = N_PAIRS * WINDOW
    return pl.pallas_call(
        _attn_prompt_kernel,
        grid=(b, nb),
        in_specs=[
            pl.BlockSpec(memory_space=pltpu.SMEM),
            pl.BlockSpec((1, WINDOW, D_Q), lambda bi, n: (bi, n, 0)),
            pl.BlockSpec((1, WINDOW, D_KV), prev(kcol)),
            pl.BlockSpec((1, WINDOW, D_KV), cur(kcol)),
            pl.BlockSpec((1, WINDOW, D_KV), prev(vcol)),
            pl.BlockSpec((1, WINDOW, D_KV), cur(vcol)),
        ],
        out_specs=[pl.BlockSpec((1, WINDOW, D_Q), lambda bi, n: (bi, n, 0)), win_spec, win_spec],
        out_shape=[jax.ShapeDtypeStruct((b, s, D_Q), F32),
                   jax.ShapeDtypeStruct((b, WINDOW, D_KV), F32),
                   jax.ShapeDtypeStruct((b, WINDOW, D_KV), F32)],
        scratch_shapes=[pltpu.VMEM((N_KV_HEADS, rows, PAIR * 2 * WINDOW), F32),
                        pltpu.VMEM((N_KV_HEADS, rows, PAIR * 2 * WINDOW), BF16),
                        pltpu.VMEM((N_KV_HEADS, rows, LANES), F32)],
        compiler_params=pltpu.CompilerParams(
            dimension_semantics=("parallel", "arbitrary"), vmem_limit_bytes=VMEM_LIMIT),
        name="attn_prompt",
    )(sinks, z3, z3, z3, z3, z3)


def _attn_sample_kernel(sinkrow_ref, q_ref, kn_ref, vn_ref, ck_ref, cv_ref,
                        a_ref, kwin_ref, vwin_ref, *, bt, t):
    pad_rows = 2 * WINDOW - WINDOW - t
    pad = jnp.zeros((pad_rows, D_KV), F32)
    p_pad = jnp.zeros((pad_rows, N_Q_HEADS * t), F32)
    zeros_col = jnp.zeros((t, LANES), F32)
    lo = lax.broadcasted_iota(jnp.int32, (t, LANES), 1) < HEAD_DIM
    tok = lax.broadcasted_iota(jnp.int32, (t, N_Q_HEADS * t), 1) % t
    key = lax.broadcasted_iota(jnp.int32, (t, N_Q_HEADS * t), 0)
    new_mask = key <= tok
    sinkrow = sinkrow_ref[...]

    def place(piece_col, src_odd, dst_odd):
        own = jnp.where(lo != src_odd, piece_col, 0.0)
        return own if src_odd == dst_odd else pltpu.roll(own, HEAD_DIM, axis=1)

    def body(s, carry):
        kn, vn, ck, cv = kn_ref[s], vn_ref[s], ck_ref[s], cv_ref[s]
        k_all = jnp.concatenate([ck, kn, pad], axis=0).astype(BF16)
        v_all = jnp.concatenate([cv, vn, pad], axis=0).astype(BF16)
        q = q_ref[s] * SCALE
        blocks = []
        for h in range(N_KV_HEADS):
            for g in range(GQA_GROUP):
                head = h * GQA_GROUP + g
                piece = place(q[:, (head // PAIR) * LANES:(head // PAIR + 1) * LANES],
                              head % PAIR == 1, h % PAIR == 1)
                cols = [zeros_col] * (D_KV // LANES)
                cols[h // PAIR] = piece
                blocks.append(jnp.concatenate(cols, axis=1))
        wq_t = jnp.concatenate(blocks, axis=0).astype(BF16)
        s_t = lax.dot_general(k_all, wq_t, NT_DIMS, preferred_element_type=F32)
        top = jnp.where(new_mask, s_t[WINDOW:WINDOW + t], s_t[0:t])
        s_m = jnp.concatenate([top, s_t[t:WINDOW]], axis=0)
        m = jnp.maximum(jnp.max(s_m, axis=0, keepdims=True), sinkrow)
        p = jnp.exp(s_m - m)
        denom = jnp.sum(p, axis=0, keepdims=True) + jnp.exp(sinkrow - m)
        probs = p * (1.0 / denom)
        p_all = jnp.concatenate(
            [jnp.where(new_mask, 0.0, probs[0:t]), probs[t:WINDOW],
             jnp.where(new_mask, probs[0:t], 0.0), p_pad], axis=0).astype(BF16)
        o_full = lax.dot_general(p_all, v_all, TN_DIMS, preferred_element_type=F32)
        out_cols = []
        for c in range(N_Q_HEADS // PAIR):
            h = (PAIR * c) // GQA_GROUP
            kv_col = slice((h // PAIR) * LANES, (h // PAIR + 1) * LANES)
            even = place(o_full[(PAIR * c) * t:(PAIR * c + 1) * t, kv_col], h % PAIR == 1, False)
            odd = place(o_full[(PAIR * c + 1) * t:(PAIR * c + 2) * t, kv_col], h % PAIR == 1, True)
            out_cols.append(even + odd)
        a_ref[s] = jnp.concatenate(out_cols, axis=1)
        kwin_ref[s, 0:WINDOW - t, :] = ck[t:, :]
        kwin_ref[s, WINDOW - t:WINDOW, :] = kn
        vwin_ref[s, 0:WINDOW - t, :] = cv[t:, :]
        vwin_ref[s, WINDOW - t:WINDOW, :] = vn
        return carry

    lax.fori_loop(0, bt, body, 0, unroll=8)


def _attn_sample(z3, cache_k, cache_v, sinks, *, bt):
    b, t, _ = z3.shape
    kcol, vcol = OFF_K // D_KV, OFF_V // D_KV
    cache_spec = pl.BlockSpec((bt, WINDOW, D_KV), lambda i: (i, 0, 0))
    sinkrow = jnp.repeat(sinks.reshape(1, N_Q_HEADS), t, axis=1)
    return pl.pallas_call(
        functools.partial(_attn_sample_kernel, bt=bt, t=t),
        grid=(b // bt,),
        in_specs=[
            _const_spec((1, N_Q_HEADS * t)),
            pl.BlockSpec((bt, t, D_Q), lambda i: (i, 0, 0)),
            pl.BlockSpec((bt, t, D_KV), lambda i: (i, 0, kcol)),
            pl.BlockSpec((bt, t, D_KV), lambda i: (i, 0, vcol)),
            cache_spec, cache_spec,
        ],
        out_specs=[pl.BlockSpec((bt, t, D_Q), lambda i: (i, 0, 0)), cache_spec, cache_spec],
        out_shape=[jax.ShapeDtypeStruct((b, t, D_Q), F32),
                   jax.ShapeDtypeStruct((b, WINDOW, D_KV), F32),
                   jax.ShapeDtypeStruct((b, WINDOW, D_KV), F32)],
        compiler_params=pltpu.CompilerParams(
            dimension_semantics=("parallel",), vmem_limit_bytes=VMEM_LIMIT),
        name="attn_sample",
    )(sinkrow, z3, z3, z3, cache_k, cache_v)


def _mix_tail(pooled_parts, a, ga, gp, x, wpool_ref, pscale_ref, wout_ref, gpost_ref, gpre_ref,
              x1_ref, h2_ref):
    parts = [jnp.dot(pooled_parts[g].astype(BF16), wpool_ref[g], preferred_element_type=F32)
             for g in range(len(POOL_WINDOWS))]
    p = jnp.concatenate(parts, axis=-1) * pscale_ref[...]
    mixed = jax.nn.sigmoid(ga) * a + jax.nn.sigmoid(gp) * p
    y = jnp.dot(mixed.astype(BF16), wout_ref[...], preferred_element_type=F32)
    x1 = x + _rmsnorm(y, gpost_ref[...])
    x1_ref[...] = x1
    h2_ref[...] = _rmsnorm(x1, gpre_ref[...]).astype(BF16)


def _mix_prompt_kernel(a_ref, ga_ref, gp_ref, u_ref, halo_ref, x_ref, wpool_ref, pscale_ref,
                       wout_ref, gpost_ref, gpre_ref, x1_ref, h2_ref, pool_ref, ext_ref,
                       *, bm, blocks_per_seq):
    blk = pl.program_id(0) % blocks_per_seq
    ext_ref[0:HALO, :] = jnp.where(blk == 0, 0.0, halo_ref[...])
    ext_ref[HALO:, :] = u_ref[...]
    pos = blk * bm + lax.broadcasted_iota(jnp.int32, (bm, 1), 0)
    pooled = []
    for g, w in enumerate(POOL_WINDOWS):
        lanes = pl.ds(g * POOL_GROUP, POOL_GROUP)
        wsum = ext_ref[pl.ds(HALO, bm), lanes]
        for s in range(1, w):
            wsum = wsum + ext_ref[pl.ds(HALO - s, bm), lanes]
        inv_cnt = 1.0 / jnp.minimum(pos + 1, w).astype(F32)
        pooled.append(wsum * inv_cnt - u_ref[:, lanes])
    _mix_tail(pooled, a_ref[...], ga_ref[...], gp_ref[...], x_ref[...], wpool_ref, pscale_ref,
              wout_ref, gpost_ref, gpre_ref, x1_ref, h2_ref)

    @pl.when(blk == blocks_per_seq - 1)
    def _():
        pool_ref[0] = ext_ref[pl.ds(HALO + bm - POOL_HIST, POOL_HIST), :]


def _mix_prompt(a2d, z2d, x2d, wpool_bf, pscale, wout_bf, gpost, gpre, *, bm, seq):
    m = x2d.shape[0]
    blocks_per_seq = seq // bm
    row_spec = lambda col: pl.BlockSpec((bm, D_MODEL), lambda i: (i, col))
    halo_blocks = bm // HALO
    return pl.pallas_call(
        functools.partial(_mix_prompt_kernel, bm=bm, blocks_per_seq=blocks_per_seq),
        grid=(m // bm,),
        in_specs=[
            row_spec(0),
            row_spec(OFF_GA // D_MODEL),
            row_spec(OFF_GP // D_MODEL),
            pl.BlockSpec((bm, D_POOL), lambda i: (i, OFF_U // D_POOL)),
            pl.BlockSpec((HALO, D_POOL),
                         lambda i: (jnp.maximum(i * halo_blocks - 1, 0), OFF_U // D_POOL)),
            row_spec(0),
            _const_spec((len(POOL_WINDOWS), POOL_GROUP, POOL_OUT_GROUP)),
            _const_spec((1, D_MODEL)),
            _const_spec((D_MODEL, D_MODEL)),
            _const_spec((1, D_MODEL)),
            _const_spec((1, D_MODEL)),
        ],
        out_specs=[row_spec(0), row_spec(0),
                   pl.BlockSpec((1, POOL_HIST, D_POOL), lambda i: (i // blocks_per_seq, 0, 0))],
        out_shape=[jax.ShapeDtypeStruct((m, D_MODEL), F32),
                   jax.ShapeDtypeStruct((m, D_MODEL), BF16),
                   jax.ShapeDtypeStruct((m // seq, POOL_HIST, D_POOL), F32)],
        scratch_shapes=[pltpu.VMEM((HALO + bm, D_POOL), F32)],
        compiler_params=pltpu.CompilerParams(
            dimension_semantics=("arbitrary",), vmem_limit_bytes=VMEM_LIMIT),
        name="mix_prompt",
    )(a2d, z2d, z2d, z2d, z2d, x2d, wpool_bf, pscale, wout_bf, gpost, gpre)


def _mix_sample_kernel(a_ref, ga_ref, gp_ref, u_ref, hist_ref, x_ref, wpool_ref, pscale_ref,
                       wout_ref, gpost_ref, gpre_ref, x1_ref, h2_ref, pool_ref, ext_ref,
                       *, bt, t, pos0):
    ext_ref[:, 0:HALO, :] = hist_ref[...]
    ext_ref[:, HALO:, :] = u_ref[...].reshape(bt, t, D_POOL)
    pooled = []
    for g, w in enumerate(POOL_WINDOWS):
        lanes = pl.ds(g * POOL_GROUP, POOL_GROUP)
        wsum = ext_ref[:, pl.ds(HALO, t), lanes]
        for s in range(1, w):
            wsum = wsum + ext_ref[:, pl.ds(HALO - s, t), lanes]
        assert pos0 + 1 >= w
        pooled.append(wsum.reshape(bt * t, POOL_GROUP) * (1.0 / w) - u_ref[:, lanes])
    _mix_tail(pooled, a_ref[...], ga_ref[...], gp_ref[...], x_ref[...], wpool_ref, pscale_ref,
              wout_ref, gpost_ref, gpre_ref, x1_ref, h2_ref)
    pool_ref[...] = ext_ref[:, pl.ds(HALO + t - POOL_HIST, POOL_HIST), :]


def _mix_sample(a2d, z2d, x2d, hist16, wpool_bf, pscale, wout_bf, gpost, gpre, *, bt, t):
    m = x2d.shape[0]
    bm = bt * t
    row_spec = lambda col: pl.BlockSpec((bm, D_MODEL), lambda i: (i, col))
    return pl.pallas_call(
        functools.partial(_mix_sample_kernel, bt=bt, t=t, pos0=PAST_LEN),
        grid=(m // bm,),
        in_specs=[
            row_spec(0),
            row_spec(OFF_GA // D_MODEL),
            row_spec(OFF_GP // D_MODEL),
            pl.BlockSpec((bm, D_POOL), lambda i: (i, OFF_U // D_POOL)),
            pl.BlockSpec((bt, HALO, D_POOL), lambda i: (i, 0, 0)),
            row_spec(0),
            _const_spec((len(POOL_WINDOWS), POOL_GROUP, POOL_OUT_GROUP)),
            _const_spec((1, D_MODEL)),
            _const_spec((D_MODEL, D_MODEL)),
            _const_spec((1, D_MODEL)),
            _const_spec((1, D_MODEL)),
        ],
        out_specs=[row_spec(0), row_spec(0),
                   pl.BlockSpec((bt, POOL_HIST, D_POOL), lambda i: (i, 0, 0))],
        out_shape=[jax.ShapeDtypeStruct((m, D_MODEL), F32),
                   jax.ShapeDtypeStruct((m, D_MODEL), BF16),
                   jax.ShapeDtypeStruct((m // t, POOL_HIST, D_POOL), F32)],
        scratch_shapes=[pltpu.VMEM((bt, HALO + t, D_POOL), F32)],
        compiler_params=pltpu.CompilerParams(
            dimension_semantics=("parallel",), vmem_limit_bytes=VMEM_LIMIT),
        name="mix_sample",
    )(a2d, z2d, z2d, z2d, hist16, x2d, wpool_bf, pscale, wout_bf, gpost, gpre)


def _mlp_kernel(h2_ref, x1_ref, wup_ref, wdn_ref, g_ref, y_ref, acc_ref):
    j = pl.program_id(1)
    hid = jnp.dot(h2_ref[...], wup_ref[...], preferred_element_type=F32)
    hid = jnp.square(jnp.maximum(hid, 0.0)).astype(BF16)
    part = jnp.dot(hid, wdn_ref[...], preferred_element_type=F32)

    @pl.when(j == 0)
    def _():
        acc_ref[...] = part

    @pl.when(j > 0)
    def _():
        acc_ref[...] += part

    @pl.when(j == pl.num_programs(1) - 1)
    def _():
        y_ref[...] = x1_ref[...] + _rmsnorm(acc_ref[...], g_ref[...])


def _mlp(h2, x1, wup_bf, wdn_bf, g, *, bm, fc):
    m = x1.shape[0]
    return pl.pallas_call(
        _mlp_kernel,
        grid=(m // bm, D_FF // fc),
        in_specs=[
            pl.BlockSpec((bm, D_MODEL), lambda i, j: (i, 0)),
            pl.BlockSpec((bm, D_MODEL), lambda i, j: (i, 0)),
            pl.BlockSpec((D_MODEL, fc), lambda i, j: (0, j)),
            pl.BlockSpec((fc, D_MODEL), lambda i, j: (j, 0)),
            _const_spec((1, D_MODEL)),
        ],
        out_specs=pl.BlockSpec((bm, D_MODEL), lambda i, j: (i, 0)),
        out_shape=jax.ShapeDtypeStruct((m, D_MODEL), F32),
        scratch_shapes=[pltpu.VMEM((bm, D_MODEL), F32)],
        compiler_params=pltpu.CompilerParams(
            dimension_semantics=("parallel", "arbitrary"), vmem_limit_bytes=VMEM_LIMIT),
        name="mlp",
    )(h2, x1, wup_bf, wdn_bf, g)


def kernel(x_prompt, x_sample, cache_k_win, cache_v_win, state_pool, norm_attn_pre, norm_attn_post,
           w_in, attn_sinks, w_pool, pool_scale, w_out, norm_mlp_pre, norm_mlp_post, w_up, w_down):
    depth = w_in.shape[0]
    assert depth == 1
    b, s, _ = x_prompt.shape
    bs, t, _ = x_sample.shape

    l = 0
    wi = w_in[l]
    c = [0, D_Q, D_Q + D_KV, D_Q + 2 * D_KV, D_Q + 2 * D_KV + D_POOL,
         D_Q + 2 * D_KV + D_POOL + D_MODEL, D_IN]
    sl = lambda i: wi[:, c[i]:c[i + 1]]
    w_in_bf = jnp.concatenate([sl(0), sl(4), sl(5), sl(3), sl(1), sl(2)], axis=1).astype(BF16)
    w_pool_bf = w_pool[l].astype(BF16)
    w_out_bf = w_out[l].astype(BF16)
    w_up_bf = w_up[l].astype(BF16)
    w_down_bf = w_down[l].astype(BF16)
    row = lambda v: v[l].reshape(1, D_MODEL)
    g_attn_pre, g_attn_post = row(norm_attn_pre), row(norm_attn_post)
    g_mlp_pre, g_mlp_post = row(norm_mlp_pre), row(norm_mlp_post)
    pscale = row(pool_scale)
    sinks = attn_sinks[l]

    xp = x_prompt.reshape(b * s, D_MODEL)
    zp = _in_proj(xp, g_attn_pre, w_in_bf, bm=1024, bn=1536)
    ap, kp, vp = _attn_prompt(zp.reshape(b, s, D_IN), sinks)
    x1p, h2p, poolp = _mix_prompt(ap.reshape(b * s, D_Q), zp, xp, w_pool_bf, pscale, w_out_bf,
                                  g_attn_post, g_mlp_pre, bm=256, seq=s)
    yp = _mlp(h2p, x1p, w_up_bf, w_down_bf, g_mlp_post, bm=512, fc=1024)

    xs = x_sample.reshape(bs * t, D_MODEL)
    zs = _in_proj(xs, g_attn_pre, w_in_bf, bm=1024, bn=1536)
    ck = cache_k_win[l].reshape(bs, WINDOW, D_KV)
    cv = cache_v_win[l].reshape(bs, WINDOW, D_KV)
    as_, ks, vs = _attn_sample(zs.reshape(bs, t, D_IN), ck, cv, sinks, bt=8)
    hist16 = jnp.pad(state_pool[l], ((0, 0), (HALO - POOL_HIST, 0), (0, 0)))
    x1s, h2s, pools = _mix_sample(as_.reshape(bs * t, D_Q), zs, xs, hist16, w_pool_bf, pscale,
                                  w_out_bf, g_attn_post, g_mlp_pre, bt=32, t=t)
    ys = _mlp(h2s, x1s, w_up_bf, w_down_bf, g_mlp_post, bm=512, fc=1024)

    kv_shape = lambda nb: (1, nb, WINDOW, N_KV_HEADS, HEAD_DIM)
    return (yp.reshape(b, s, D_MODEL), ys.reshape(bs, t, D_MODEL),
            kp.reshape(kv_shape(b)), vp.reshape(kv_shape(b)), poolp[None],
            ks.reshape(kv_shape(bs)), vs.reshape(kv_shape(bs)), pools[None])
```

```python
import functools

import jax
import jax.numpy as jnp
from jax import lax
from jax.experimental import pallas as pl
from jax.experimental.pallas import tpu as pltpu

F32 = jnp.float32
BF16 = jnp.bfloat16

D_MODEL = 2048
HEAD_DIM = 64
N_KV_HEADS = 4
GQA_GROUP = 8
N_Q_HEADS = N_KV_HEADS * GQA_GROUP
WINDOW = 128
D_Q = 2048
D_KV = 256
POOL_WINDOWS = (2, 4, 8, 16)
D_POOL = 1024
POOL_GROUP = 256
POOL_OUT_GROUP = 512
POOL_HIST = 15
D_FF = 8192
D_IN = D_Q + 2 * D_KV + D_POOL + 2 * D_MODEL
PAST_LEN = 8192
EPS = 1e-6
NEG_INF = -1e30
SCALE = HEAD_DIM ** -0.5

LANES = 128
PAIR = 2
N_PAIRS = GQA_GROUP // PAIR
ROW_CHUNK = 16

OFF_Q, OFF_GA, OFF_GP, OFF_U, OFF_K, OFF_V = 0, 2048, 4096, 6144, 7168, 7424
HALO = 16

VMEM_LIMIT = 60 * 1024 * 1024

NT_DIMS = (((1,), (1,)), ((), ()))
TN_DIMS = (((0,), (0,)), ((), ()))


def _rmsnorm(x, g):
    r = lax.rsqrt(jnp.mean(x * x, axis=-1, keepdims=True) + EPS)
    return x * r * g


def _const_spec(shape):
    return pl.BlockSpec(shape, lambda *_: (0,) * len(shape), pipeline_mode=pl.Buffered(1))


def _in_proj_kernel(x_ref, g_ref, w_ref, z_ref, h_ref):
    @pl.when(pl.program_id(1) == 0)
    def _():
        h_ref[...] = _rmsnorm(x_ref[...], g_ref[...]).astype(BF16)

    z_ref[...] = jnp.dot(h_ref[...], w_ref[...], preferred_element_type=F32)


def _in_proj_cast_kernel(x_ref, g_ref, w_ref, z_ref, wbf_ref, h_ref):
    @pl.when(pl.program_id(0) == 0)
    def _():
        h_ref[...] = _rmsnorm(x_ref[...], g_ref[...]).astype(BF16)

    w = w_ref[...].astype(BF16)
    wbf_ref[...] = w
    z_ref[...] = jnp.dot(h_ref[...], w, preferred_element_type=F32)


def _in_proj_cast(x2d, g, w_f32):
    m = x2d.shape[0]
    bn = 512
    src_bounds = (D_Q, D_Q + 2 * D_KV, D_Q + 2 * D_KV + D_POOL, D_Q + 2 * D_KV + D_POOL + D_MODEL)
    assert all(off % bn == 0 for off in src_bounds + (OFF_GA, OFF_GP, OFF_U, OFF_K))

    def reordered(j):
        return jnp.where(j < 4, j, jnp.where(j == 4, OFF_K // bn, jnp.where(j < 7, j + 7, j - 3)))

    return pl.pallas_call(
        _in_proj_cast_kernel,
        grid=(D_IN // bn,),
        in_specs=[
            _const_spec((m, D_MODEL)),
            _const_spec((1, D_MODEL)),
            pl.BlockSpec((D_MODEL, bn), lambda j: (0, j)),
        ],
        out_specs=[pl.BlockSpec((m, bn), lambda j: (0, reordered(j))),
                   pl.BlockSpec((D_MODEL, bn), lambda j: (0, reordered(j)))],
        out_shape=[jax.ShapeDtypeStruct((m, D_IN), F32),
                   jax.ShapeDtypeStruct((D_MODEL, D_IN), BF16)],
        scratch_shapes=[pltpu.VMEM((m, D_MODEL), BF16)],
        compiler_params=pltpu.CompilerParams(
            dimension_semantics=("arbitrary",), vmem_limit_bytes=VMEM_LIMIT),
        name="in_proj_cast",
    )(x2d, g, w_f32)


def _in_proj(x2d, g, w_bf, *, bm, bn):
    m = x2d.shape[0]
    return pl.pallas_call(
        _in_proj_kernel,
        grid=(m // bm, D_IN // bn),
        in_specs=[
            pl.BlockSpec((bm, D_MODEL), lambda i, j: (i, 0)),
            _const_spec((1, D_MODEL)),
            pl.BlockSpec((D_MODEL, bn), lambda i, j: (0, j)),
        ],
        out_specs=pl.BlockSpec((bm, bn), lambda i, j: (i, j)),
        out_shape=jax.ShapeDtypeStruct((m, D_IN), F32),
        scratch_shapes=[pltpu.VMEM((bm, D_MODEL), BF16)],
        compiler_params=pltpu.CompilerParams(
            dimension_semantics=("parallel", "arbitrary"), vmem_limit_bytes=VMEM_LIMIT),
        name="in_proj",
    )(x2d, g, w_bf)


def _own_and_swapped(slab, odd, lo):
    own = jnp.where(lo != odd, slab, 0.0)
    swapped = pltpu.roll(own, HEAD_DIM, axis=1)
    return (swapped, own) if odd else (own, swapped)


def _attn_prompt_kernel(sinks_ref, q_ref, kp_ref, kc_ref, vp_ref, vc_ref,
                        a_ref, kwin_ref, vwin_ref, s_ref, p_ref, e_ref):
    n = pl.program_id(1)
    kc = kc_ref[0]
    vc = vc_ref[0]
    k_all = jnp.concatenate([kp_ref[0], kc], axis=0)
    v_all = jnp.concatenate([vp_ref[0], vc], axis=0)
    prev_bias = jnp.where(n > 0, 0.0, NEG_INF).astype(F32)
    lo_kv = lax.broadcasted_iota(jnp.int32, (2 * WINDOW, LANES), 1) < HEAD_DIM
    ones_lo = jnp.where(lo_kv, 1.0, 0.0)
    ones_hi = 1.0 - ones_lo
    lo_c = lax.broadcasted_iota(jnp.int32, (ROW_CHUNK, LANES), 1) < HEAD_DIM
    lane_c = lax.broadcasted_iota(jnp.int32, (ROW_CHUNK, LANES), 1)
    row_c = lax.broadcasted_iota(jnp.int32, (ROW_CHUNK, LANES), 0)

    for h in range(N_KV_HEADS):
        col = pl.ds((h // PAIR) * LANES, LANES)
        odd = (h % PAIR) == 1
        k_l, k_r = _own_and_swapped(k_all[:, (h // PAIR) * LANES:(h // PAIR + 1) * LANES], odd, lo_kv)
        v_l, v_r = _own_and_swapped(v_all[:, (h // PAIR) * LANES:(h // PAIR + 1) * LANES], odd, lo_kv)
        wk = jnp.concatenate([k_l, k_r], axis=0).astype(BF16)
        vext = jnp.concatenate(
            [jnp.concatenate([v_l, ones_lo], axis=1),
             jnp.concatenate([v_r, ones_hi], axis=1)], axis=0).astype(BF16)
        qh = jnp.concatenate(
            [q_ref[0, :, pl.ds((h * N_PAIRS + j) * LANES, LANES)] for j in range(N_PAIRS)], axis=0)
        qh = (qh * SCALE).astype(BF16)
        s_ref[h] = lax.dot_general(qh, wk, NT_DIMS, preferred_element_type=F32)

        for j in range(N_PAIRS):
            sinks = [sinks_ref[h, PAIR * j], sinks_ref[h, PAIR * j + 1]]
            for c in range(WINDOW // ROW_CHUNK):
                rows = pl.ds(j * WINDOW + c * ROW_CHUNK, ROW_CHUNK)
                mask = lane_c <= (row_c + c * ROW_CHUNK)
                e_parts = []
                for gi in range(PAIR):
                    base = gi * 2 * WINDOW
                    prev = s_ref[h, rows, pl.ds(base, WINDOW)] + prev_bias
                    cur = s_ref[h, rows, pl.ds(base + WINDOW, WINDOW)]
                    s = jnp.where(mask, cur, prev)
                    m = jnp.maximum(jnp.max(s, axis=-1, keepdims=True), sinks[gi])
                    p = jnp.exp(s - m)
                    p_ref[h, rows, pl.ds(base, WINDOW)] = jnp.where(mask, 0.0, p).astype(BF16)
                    p_ref[h, rows, pl.ds(base + WINDOW, WINDOW)] = jnp.where(mask, p, 0.0).astype(BF16)
                    e_parts.append(jnp.broadcast_to(jnp.exp(sinks[gi] - m), (ROW_CHUNK, LANES)))
                e_ref[h, rows, :] = jnp.where(lo_c, e_parts[0], e_parts[1])

        o_ext = jnp.dot(p_ref[h], vext, preferred_element_type=F32)
        o = o_ext[:, :LANES] / (o_ext[:, LANES:] + e_ref[h])
        for j in range(N_PAIRS):
            a_ref[0, :, pl.ds((h * N_PAIRS + j) * LANES, LANES)] = o[j * WINDOW:(j + 1) * WINDOW]

    @pl.when(n == pl.num_programs(1) - 1)
    def _():
        kwin_ref[0] = kc
        vwin_ref[0] = vc


def _attn_prompt(z3, sinks):
    b, s, _ = z3.shape
    nb = s // WINDOW
    kcol, vcol = OFF_K // D_KV, OFF_V // D_KV
    prev = lambda col: (lambda bi, n: (bi, jnp.maximum(n - 1, 0), col))
    cur = lambda col: (lambda bi, n: (bi, n, col))
    win_spec = pl.BlockSpec((1, WINDOW, D_KV), lambda bi, n: (bi, 0, 0))
    rows = N_PAIRS * WINDOW
    return pl.pallas_call(
        _attn_prompt_kernel,
        grid=(b, nb),
        in_specs=[
            pl.BlockSpec(memory_space=pltpu.SMEM),
            pl.BlockSpec((1, WINDOW, D_Q), lambda bi, n: (bi, n, 0)),
            pl.BlockSpec((1, WINDOW, D_KV), prev(kcol)),
            pl.BlockSpec((1, WINDOW, D_KV), cur(kcol)),
            pl.BlockSpec((1, WINDOW, D_KV), prev(vcol)),
            pl.BlockSpec((1, WINDOW, D_KV), cur(vcol)),
        ],
        out_specs=[pl.BlockSpec((1, WINDOW, D_Q), lambda bi, n: (bi, n, 0)), win_spec, win_spec],
        out_shape=[jax.ShapeDtypeStruct((b, s, D_Q), F32),
                   jax.ShapeDtypeStruct((b, WINDOW, D_KV), F32),
                   jax.ShapeDtypeStruct((b, WINDOW, D_KV), F32)],
        scratch_shapes=[pltpu.VMEM((N_KV_HEADS, rows, PAIR * 2 * WINDOW), F32),
                        pltpu.VMEM((N_KV_HEADS, rows, PAIR * 2 * WINDOW), BF16),
                        pltpu.VMEM((N_KV_HEADS, rows, LANES), F32)],
        compiler_params=pltpu.CompilerParams(
            dimension_semantics=("parallel", "arbitrary"), vmem_limit_bytes=VMEM_LIMIT),
        name="attn_prompt",
    )(sinks, z3, z3, z3, z3, z3)


def _attn_sample_kernel(sinkrow_ref, q_ref, kn_ref, vn_ref, ck_ref, cv_ref,
                        a_ref, kwin_ref, vwin_ref, *, bt, t):
    pad_rows = 2 * WINDOW - WINDOW - t
    pad = jnp.zeros((pad_rows, D_KV), F32)
    p_pad = jnp.zeros((pad_rows, N_Q_HEADS * t), F32)
    zeros_col = jnp.zeros((t, LANES), F32)
    lo = lax.broadcasted_iota(jnp.int32, (t, LANES), 1) < HEAD_DIM
    tok = lax.broadcasted_iota(jnp.int32, (t, N_Q_HEADS * t), 1) % t
    key = lax.broadcasted_iota(jnp.int32, (t, N_Q_HEADS * t), 0)
    new_mask = key <= tok
    sinkrow = sinkrow_ref[...]

    def place(piece_col, src_odd, dst_odd):
        own = jnp.where(lo != src_odd, piece_col, 0.0)
        return own if src_odd == dst_odd else pltpu.roll(own, HEAD_DIM, axis=1)

    def body(s, carry):
        kn, vn, ck, cv = kn_ref[s], vn_ref[s], ck_ref[s], cv_ref[s]
        k_all = jnp.concatenate([ck, kn, pad], axis=0).astype(BF16)
        v_all = jnp.concatenate([cv, vn, pad], axis=0).astype(BF16)
        q = q_ref[s] * SCALE
        blocks = []
        for h in range(N_KV_HEADS):
            for g in range(GQA_GROUP):
                head = h * GQA_GROUP + g
                piece = place(q[:, (head // PAIR) * LANES:(head // PAIR + 1) * LANES],
                              head % PAIR == 1, h % PAIR == 1)
                cols = [zeros_col] * (D_KV // LANES)
                cols[h // PAIR] = piece
                blocks.append(jnp.concatenate(cols, axis=1))
        wq_t = jnp.concatenate(blocks, axis=0).astype(BF16)
        s_t = lax.dot_general(k_all, wq_t, NT_DIMS, preferred_element_type=F32)
        top = jnp.where(new_mask, s_t[WINDOW:WINDOW + t], s_t[0:t])
        s_m = jnp.concatenate([top, s_t[t:WINDOW]], axis=0)
        m = jnp.maximum(jnp.max(s_m, axis=0, keepdims=True), sinkrow)
        p = jnp.exp(s_m - m)
        denom = jnp.sum(p, axis=0, keepdims=True) + jnp.exp(sinkrow - m)
        probs = p * (1.0 / denom)
        p_all = jnp.concatenate(
            [jnp.where(new_mask, 0.0, probs[0:t]), probs[t:WINDOW],
             jnp.where(new_mask, probs[0:t], 0.0), p_pad], axis=0).astype(BF16)
        o_full = lax.dot_general(p_all, v_all, TN_DIMS, preferred_element_type=F32)
        out_cols = []
        for c in range(N_Q_HEADS // PAIR):
            h = (PAIR * c) // GQA_GROUP
            kv_col = slice((h // PAIR) * LANES, (h // PAIR + 1) * LANES)
            even = place(o_full[(PAIR * c) * t:(PAIR * c + 1) * t, kv_col], h % PAIR == 1, False)
            odd = place(o_full[(PAIR * c + 1) * t:(PAIR * c + 2) * t, kv_col], h % PAIR == 1, True)
            out_cols.append(even + odd)
        a_ref[s] = jnp.concatenate(out_cols, axis=1)
        kwin_ref[s, 0:WINDOW - t, :] = ck[t:, :]
        kwin_ref[s, WINDOW - t:WINDOW, :] = kn
        vwin_ref[s, 0:WINDOW - t, :] = cv[t:, :]
        vwin_ref[s, WINDOW - t:WINDOW, :] = vn
        return carry

    lax.fori_loop(0, bt, body, 0, unroll=8)


def _attn_sample(z3, cache_k, cache_v, sinks, *, bt):
    b, t, _ = z3.shape
    kcol, vcol = OFF_K // D_KV, OFF_V // D_KV
    cache_spec = pl.BlockSpec((bt, WINDOW, D_KV), lambda i: (i, 0, 0))
    sinkrow = jnp.repeat(sinks.reshape(1, N_Q_HEADS), t, axis=1)
    return pl.pallas_call(
        functools.partial(_attn_sample_kernel, bt=bt, t=t),
        grid=(b // bt,),
        in_specs=[
            _const_spec((1, N_Q_HEADS * t)),
            pl.BlockSpec((bt, t, D_Q), lambda i: (i, 0, 0)),
            pl.BlockSpec((bt, t, D_KV), lambda i: (i, 0, kcol)),
            pl.BlockSpec((bt, t, D_KV), lambda i: (i, 0, vcol)),
            cache_spec, cache_spec,
        ],
        out_specs=[pl.BlockSpec((bt, t, D_Q), lambda i: (i, 0, 0)), cache_spec, cache_spec],
        out_shape=[jax.ShapeDtypeStruct((b, t, D_Q), F32),
                   jax.ShapeDtypeStruct((b, WINDOW, D_KV), F32),
                   jax.ShapeDtypeStruct((b, WINDOW, D_KV), F32)],
        compiler_params=pltpu.CompilerParams(
            dimension_semantics=("parallel",), vmem_limit_bytes=VMEM_LIMIT),
        name="attn_sample",
    )(sinkrow, z3, z3, z3, cache_k, cache_v)


def _mix_tail(pooled_parts, a, ga, gp, x, wpool_ref, pscale_ref, wout_ref, gpost_ref, gpre_ref,
              x1_ref, h2_ref):
    parts = [jnp.dot(pooled_parts[g].astype(BF16), wpool_ref[g], preferred_element_type=F32)
             for g in range(len(POOL_WINDOWS))]
    p = jnp.concatenate(parts, axis=-1) * pscale_ref[...]
    mixed = jax.nn.sigmoid(ga) * a + jax.nn.sigmoid(gp) * p
    y = jnp.dot(mixed.astype(BF16), wout_ref[...], preferred_element_type=F32)
    x1 = x + _rmsnorm(y, gpost_ref[...])
    x1_ref[...] = x1
    h2_ref[...] = _rmsnorm(x1, gpre_ref[...]).astype(BF16)


def _mix_prompt_kernel(a_ref, ga_ref, gp_ref, u_ref, halo_ref, x_ref, wpool_ref, pscale_ref,
                       wout_ref, gpost_ref, gpre_ref, x1_ref, h2_ref, pool_ref, ext_ref,
                       *, bm, blocks_per_seq):
    blk = pl.program_id(0) % blocks_per_seq
    ext_ref[0:HALO, :] = jnp.where(blk == 0, 0.0, halo_ref[...])
    ext_ref[HALO:, :] = u_ref[...]
    pos = blk * bm + lax.broadcasted_iota(jnp.int32, (bm, 1), 0)
    pooled = []
    for g, w in enumerate(POOL_WINDOWS):
        lanes = pl.ds(g * POOL_GROUP, POOL_GROUP)
        wsum = ext_ref[pl.ds(HALO, bm), lanes]
        for s in range(1, w):
            wsum = wsum + ext_ref[pl.ds(HALO - s, bm), lanes]
        inv_cnt = 1.0 / jnp.minimum(pos + 1, w).astype(F32)
        pooled.append(wsum * inv_cnt - u_ref[:, lanes])
    _mix_tail(pooled, a_ref[...], ga_ref[...], gp_ref[...], x_ref[...], wpool_ref, pscale_ref,
              wout_ref, gpost_ref, gpre_ref, x1_ref, h2_ref)

    @pl.when(blk == blocks_per_seq - 1)
    def _():
        pool_ref[0] = ext_ref[pl.ds(HALO + bm - POOL_HIST, POOL_HIST), :]


def _mix_prompt(a2d, z2d, x2d, wpool_bf, pscale, wout_bf, gpost, gpre, *, bm, seq):
    m = x2d.shape[0]
    blocks_per_seq = seq // bm
    row_spec = lambda col: pl.BlockSpec((bm, D_MODEL), lambda i: (i, col))
    halo_blocks = bm // HALO
    return pl.pallas_call(
        functools.partial(_mix_prompt_kernel, bm=bm, blocks_per_seq=blocks_per_seq),
        grid=(m // bm,),
        in_specs=[
            row_spec(0),
            row_spec(OFF_GA // D_MODEL),
            row_spec(OFF_GP // D_MODEL),
            pl.BlockSpec((bm, D_POOL), lambda i: (i, OFF_U // D_POOL)),
            pl.BlockSpec((HALO, D_POOL),
                         lambda i: (jnp.maximum(i * halo_blocks - 1, 0), OFF_U // D_POOL)),
            row_spec(0),
            _const_spec((len(POOL_WINDOWS), POOL_GROUP, POOL_OUT_GROUP)),
            _const_spec((1, D_MODEL)),
            _const_spec((D_MODEL, D_MODEL)),
            _const_spec((1, D_MODEL)),
            _const_spec((1, D_MODEL)),
        ],
        out_specs=[row_spec(0), row_spec(0),
                   pl.BlockSpec((1, POOL_HIST, D_POOL), lambda i: (i // blocks_per_seq, 0, 0))],
        out_shape=[jax.ShapeDtypeStruct((m, D_MODEL), F32),
                   jax.ShapeDtypeStruct((m, D_MODEL), BF16),
                   jax.ShapeDtypeStruct((m // seq, POOL_HIST, D_POOL), F32)],
        scratch_shapes=[pltpu.VMEM((HALO + bm, D_POOL), F32)],
        compiler_params=pltpu.CompilerParams(
            dimension_semantics=("arbitrary",), vmem_limit_bytes=VMEM_LIMIT),
        name="mix_prompt",
    )(a2d, z2d, z2d, z2d, z2d, x2d, wpool_bf, pscale, wout_bf, gpost, gpre)


def _mix_sample_kernel(a_ref, ga_ref, gp_ref, u_ref, hist_ref, x_ref, wpool_ref, pscale_ref,
                       wout_ref, gpost_ref, gpre_ref, x1_ref, h2_ref, pool_ref, ext_ref,
                       *, bt, t, pos0):
    ext_ref[:, HALO - POOL_HIST:HALO, :] = hist_ref[...]
    ext_ref[:, HALO:, :] = u_ref[...].reshape(bt, t, D_POOL)
    pooled = []
    for g, w in enumerate(POOL_WINDOWS):
        lanes = pl.ds(g * POOL_GROUP, POOL_GROUP)
        wsum = ext_ref[:, pl.ds(HALO, t), lanes]
        for s in range(1, w):
            wsum = wsum + ext_ref[:, pl.ds(HALO - s, t), lanes]
        assert pos0 + 1 >= w
        pooled.append(wsum.reshape(bt * t, POOL_GROUP) * (1.0 / w) - u_ref[:, lanes])
    _mix_tail(pooled, a_ref[...], ga_ref[...], gp_ref[...], x_ref[...], wpool_ref, pscale_ref,
              wout_ref, gpost_ref, gpre_ref, x1_ref, h2_ref)
    pool_ref[...] = ext_ref[:, pl.ds(HALO + t - POOL_HIST, POOL_HIST), :]


def _mix_sample(a2d, z2d, x2d, hist, wpool_bf, pscale, wout_bf, gpost, gpre, *, bt, t):
    m = x2d.shape[0]
    bm = bt * t
    row_spec = lambda col: pl.BlockSpec((bm, D_MODEL), lambda i: (i, col))
    return pl.pallas_call(
        functools.partial(_mix_sample_kernel, bt=bt, t=t, pos0=PAST_LEN),
        grid=(m // bm,),
        in_specs=[
            row_spec(0),
            row_spec(OFF_GA // D_MODEL),
            row_spec(OFF_GP // D_MODEL),
            pl.BlockSpec((bm, D_POOL), lambda i: (i, OFF_U // D_POOL)),
            pl.BlockSpec((bt, POOL_HIST, D_POOL), lambda i: (i, 0, 0)),
            row_spec(0),
            _const_spec((len(POOL_WINDOWS), POOL_GROUP, POOL_OUT_GROUP)),
            _const_spec((1, D_MODEL)),
            _const_spec((D_MODEL, D_MODEL)),
            _const_spec((1, D_MODEL)),
            _const_spec((1, D_MODEL)),
        ],
        out_specs=[row_spec(0), row_spec(0),
                   pl.BlockSpec((bt, POOL_HIST, D_POOL), lambda i: (i, 0, 0))],
        out_shape=[jax.ShapeDtypeStruct((m, D_MODEL), F32),
                   jax.ShapeDtypeStruct((m, D_MODEL), BF16),
                   jax.ShapeDtypeStruct((m // t, POOL_HIST, D_POOL), F32)],
        scratch_shapes=[pltpu.VMEM((bt, HALO + t, D_POOL), F32)],
        compiler_params=pltpu.CompilerParams(
            dimension_semantics=("parallel",), vmem_limit_bytes=VMEM_LIMIT),
        name="mix_sample",
    )(a2d, z2d, z2d, z2d, hist, x2d, wpool_bf, pscale, wout_bf, gpost, gpre)


def _mlp_kernel(h2_ref, x1_ref, wup_ref, wdn_ref, g_ref, y_ref):
    j = pl.program_id(1)

    @pl.when(j == 0)
    def _():
        y_ref[...] = jnp.zeros_like(y_ref)

    hid = jnp.dot(h2_ref[...], wup_ref[...], preferred_element_type=F32)
    hid = jnp.square(jnp.maximum(hid, 0.0)).astype(BF16)
    y_ref[...] += jnp.dot(hid, wdn_ref[...], preferred_element_type=F32)

    @pl.when(j == pl.num_programs(1) - 1)
    def _():
        y_ref[...] = x1_ref[...] + _rmsnorm(y_ref[...], g_ref[...])


def _mlp_cast_kernel(h2_ref, x1_ref, wup_ref, wdn_ref, g_ref, y_ref, wup_bf_ref, wdn_bf_ref):
    j = pl.program_id(0)

    @pl.when(j == 0)
    def _():
        y_ref[...] = jnp.zeros_like(y_ref)

    wup = wup_ref[...].astype(BF16)
    wdn = wdn_ref[...].astype(BF16)
    wup_bf_ref[...] = wup
    wdn_bf_ref[...] = wdn
    hid = jnp.dot(h2_ref[...], wup, preferred_element_type=F32)
    hid = jnp.square(jnp.maximum(hid, 0.0)).astype(BF16)
    y_ref[...] += jnp.dot(hid, wdn, preferred_element_type=F32)

    @pl.when(j == pl.num_programs(0) - 1)
    def _():
        y_ref[...] = x1_ref[...] + _rmsnorm(y_ref[...], g_ref[...])


def _mlp_cast(h2, x1, wup_f32, wdn_f32, g, *, fc):
    m = x1.shape[0]
    return pl.pallas_call(
        _mlp_cast_kernel,
        grid=(D_FF // fc,),
        in_specs=[
            _const_spec((m, D_MODEL)),
            _const_spec((m, D_MODEL)),
            pl.BlockSpec((D_MODEL, fc), lambda j: (0, j)),
            pl.BlockSpec((fc, D_MODEL), lambda j: (j, 0)),
            _const_spec((1, D_MODEL)),
        ],
        out_specs=[pl.BlockSpec((m, D_MODEL), lambda j: (0, 0)),
                   pl.BlockSpec((D_MODEL, fc), lambda j: (0, j)),
                   pl.BlockSpec((fc, D_MODEL), lambda j: (j, 0))],
        out_shape=[jax.ShapeDtypeStruct((m, D_MODEL), F32),
                   jax.ShapeDtypeStruct((D_MODEL, D_FF), BF16),
                   jax.ShapeDtypeStruct((D_FF, D_MODEL), BF16)],
        compiler_params=pltpu.CompilerParams(
            dimension_semantics=("arbitrary",), vmem_limit_bytes=VMEM_LIMIT),
        name="mlp_cast",
    )(h2, x1, wup_f32, wdn_f32, g)


def _mlp(h2, x1, wup_bf, wdn_bf, g, *, bm, fc):
    m = x1.shape[0]
    return pl.pallas_call(
        _mlp_kernel,
        grid=(m // bm, D_FF // fc),
        in_specs=[
            pl.BlockSpec((bm, D_MODEL), lambda i, j: (i, 0)),
            pl.BlockSpec((bm, D_MODEL), lambda i, j: (i, 0)),
            pl.BlockSpec((D_MODEL, fc), lambda i, j: (0, j)),
            pl.BlockSpec((fc, D_MODEL), lambda i, j: (j, 0)),
            _const_spec((1, D_MODEL)),
        ],
        out_specs=pl.BlockSpec((bm, D_MODEL), lambda i, j: (i, 0)),
        out_shape=jax.ShapeDtypeStruct((m, D_MODEL), F32),
        compiler_params=pltpu.CompilerParams(
            dimension_semantics=("parallel", "arbitrary"), vmem_limit_bytes=VMEM_LIMIT),
        name="mlp",
    )(h2, x1, wup_bf, wdn_bf, g)


def kernel(x_prompt, x_sample, cache_k_win, cache_v_win, state_pool, norm_attn_pre, norm_attn_post,
           w_in, attn_sinks, w_pool, pool_scale, w_out, norm_mlp_pre, norm_mlp_post, w_up, w_down):
    depth = w_in.shape[0]
    assert depth == 1
    b, s, _ = x_prompt.shape
    bs, t, _ = x_sample.shape

    l = 0
    w_pool_bf = w_pool[l].astype(BF16)
    w_out_bf = w_out[l].astype(BF16)
    row = lambda v: v[l].reshape(1, D_MODEL)
    g_attn_pre, g_attn_post = row(norm_attn_pre), row(norm_attn_post)
    g_mlp_pre, g_mlp_post = row(norm_mlp_pre), row(norm_mlp_post)
    pscale = row(pool_scale)
    sinks = attn_sinks[l]

    xs = x_sample.reshape(bs * t, D_MODEL)
    zs, w_in_bf = _in_proj_cast(xs, g_attn_pre, w_in[l])
    ck = cache_k_win[l].reshape(bs, WINDOW, D_KV)
    cv = cache_v_win[l].reshape(bs, WINDOW, D_KV)
    as_, ks, vs = _attn_sample(zs.reshape(bs, t, D_IN), ck, cv, sinks, bt=8)
    x1s, h2s, pools = _mix_sample(as_.reshape(bs * t, D_Q), zs, xs, state_pool[l], w_pool_bf, pscale,
                                  w_out_bf, g_attn_post, g_mlp_pre, bt=32, t=t)
    ys, w_up_bf, w_down_bf = _mlp_cast(h2s, x1s, w_up[l], w_down[l], g_mlp_post, fc=512)

    xp = x_prompt.reshape(b * s, D_MODEL)
    zp = _in_proj(xp, g_attn_pre, w_in_bf, bm=1024, bn=1536)
    ap, kp, vp = _attn_prompt(zp.reshape(b, s, D_IN), sinks)
    x1p, h2p, poolp = _mix_prompt(ap.reshape(b * s, D_Q), zp, xp, w_pool_bf, pscale, w_out_bf,
                                  g_attn_post, g_mlp_pre, bm=256, seq=s)
    yp = _mlp(h2p, x1p, w_up_bf, w_down_bf, g_mlp_post, bm=1024, fc=512)

    kv_shape = lambda nb: (1, nb, WINDOW, N_KV_HEADS, HEAD_DIM)
    return (yp.reshape(b, s, D_MODEL), ys.reshape(bs, t, D_MODEL),
            kp.reshape(kv_shape(b)), vp.reshape(kv_shape(b)), poolp[None],
            ks.reshape(kv_shape(bs)), vs.reshape(kv_shape(bs)), pools[None])
```

```python
import functools

import jax
import jax.numpy as jnp
from jax import lax
from jax.experimental import pallas as pl
from jax.experimental.pallas import tpu as pltpu

F32 = jnp.float32
BF16 = jnp.bfloat16

D_MODEL = 2048
HEAD_DIM = 64
N_KV_HEADS = 4
GQA_GROUP = 8
N_Q_HEADS = N_KV_HEADS * GQA_GROUP
WINDOW = 128
D_Q = 2048
D_KV = 256
POOL_WINDOWS = (2, 4, 8, 16)
D_POOL = 1024
POOL_GROUP = 256
POOL_OUT_GROUP = 512
POOL_HIST = 15
D_FF = 8192
D_IN = D_Q + 2 * D_KV + D_POOL + 2 * D_MODEL
PAST_LEN = 8192
EPS = 1e-6
NEG_INF = -1e30
SCALE = HEAD_DIM ** -0.5
LOG2E = 1.4426950408889634

LANES = 128
PAIR = 2
N_PAIRS = GQA_GROUP // PAIR
ROW_CHUNK = 16

OFF_Q, OFF_GA, OFF_GP, OFF_U, OFF_K, OFF_V = 0, 2048, 4096, 6144, 7168, 7424
HALO = 16
POOL_PAD = 8

VMEM_LIMIT = 60 * 1024 * 1024

NT_DIMS = (((1,), (1,)), ((), ()))
TN_DIMS = (((0,), (0,)), ((), ()))


def _rmsnorm(x, g):
    r = lax.rsqrt(jnp.mean(x * x, axis=-1, keepdims=True) + EPS)
    return x * r * g


def _const_spec(shape):
    return pl.BlockSpec(shape, lambda *_: (0,) * len(shape), pipeline_mode=pl.Buffered(1))


def _in_proj_kernel(x_ref, g_ref, w_ref, z_ref, h_ref):
    @pl.when(pl.program_id(1) == 0)
    def _():
        h_ref[...] = _rmsnorm(x_ref[...], g_ref[...]).astype(BF16)

    z_ref[...] = jnp.dot(h_ref[...], w_ref[...], preferred_element_type=F32)


def _in_proj_cast_kernel(x_ref, g_ref, w_ref, z_ref, wbf_ref, h_ref):
    @pl.when(pl.program_id(0) == 0)
    def _():
        h_ref[...] = _rmsnorm(x_ref[...], g_ref[...]).astype(BF16)

    w = w_ref[...].astype(BF16)
    wbf_ref[...] = w
    z_ref[...] = jnp.dot(h_ref[...], w, preferred_element_type=F32)


def _in_proj_cast(x2d, g, w_f32):
    m = x2d.shape[0]
    bn = 512
    src_bounds = (D_Q, D_Q + 2 * D_KV, D_Q + 2 * D_KV + D_POOL, D_Q + 2 * D_KV + D_POOL + D_MODEL)
    assert all(off % bn == 0 for off in src_bounds + (OFF_GA, OFF_GP, OFF_U, OFF_K))

    def reordered(j):
        return jnp.where(j < 4, j, jnp.where(j == 4, OFF_K // bn, jnp.where(j < 7, j + 7, j - 3)))

    return pl.pallas_call(
        _in_proj_cast_kernel,
        grid=(D_IN // bn,),
        in_specs=[
            _const_spec((m, D_MODEL)),
            _const_spec((1, D_MODEL)),
            pl.BlockSpec((D_MODEL, bn), lambda j: (0, j)),
        ],
        out_specs=[pl.BlockSpec((m, bn), lambda j: (0, reordered(j))),
                   pl.BlockSpec((D_MODEL, bn), lambda j: (0, reordered(j)))],
        out_shape=[jax.ShapeDtypeStruct((m, D_IN), F32),
                   jax.ShapeDtypeStruct((D_MODEL, D_IN), BF16)],
        scratch_shapes=[pltpu.VMEM((m, D_MODEL), BF16)],
        compiler_params=pltpu.CompilerParams(
            dimension_semantics=("arbitrary",), vmem_limit_bytes=VMEM_LIMIT),
        name="in_proj_cast",
    )(x2d, g, w_f32)


def _in_proj(x2d, g, w_bf, *, bm, bn):
    m = x2d.shape[0]
    return pl.pallas_call(
        _in_proj_kernel,
        grid=(m // bm, D_IN // bn),
        in_specs=[
            pl.BlockSpec((bm, D_MODEL), lambda i, j: (i, 0)),
            _const_spec((1, D_MODEL)),
            pl.BlockSpec((D_MODEL, bn), lambda i, j: (0, j)),
        ],
        out_specs=pl.BlockSpec((bm, bn), lambda i, j: (i, j)),
        out_shape=jax.ShapeDtypeStruct((m, D_IN), F32),
        scratch_shapes=[pltpu.VMEM((bm, D_MODEL), BF16)],
        compiler_params=pltpu.CompilerParams(
            dimension_semantics=("parallel", "arbitrary"), vmem_limit_bytes=VMEM_LIMIT),
        name="in_proj",
    )(x2d, g, w_bf)


def _own_and_swapped(slab, odd, lo):
    own = jnp.where(lo != odd, slab, 0.0)
    swapped = pltpu.roll(own, HEAD_DIM, axis=1)
    return (swapped, own) if odd else (own, swapped)


def _attn_prompt_kernel(sinks_ref, q_ref, kp_ref, kc_ref, vp_ref, vc_ref,
                        a_ref, kwin_ref, vwin_ref, s_ref, p_ref, e_ref):
    n = pl.program_id(1)
    kc = kc_ref[0]
    vc = vc_ref[0]
    k_all = jnp.concatenate([kp_ref[0], kc], axis=0)
    v_all = jnp.concatenate([vp_ref[0], vc], axis=0)
    lo_kv = lax.broadcasted_iota(jnp.int32, (2 * WINDOW, LANES), 1) < HEAD_DIM
    ones_lo = jnp.where(lo_kv, 1.0, 0.0)
    ones_hi = 1.0 - ones_lo
    lo_c = lax.broadcasted_iota(jnp.int32, (ROW_CHUNK, LANES), 1) < HEAD_DIM
    lane_c = lax.broadcasted_iota(jnp.int32, (ROW_CHUNK, LANES), 1)
    row_c = lax.broadcasted_iota(jnp.int32, (ROW_CHUNK, LANES), 0)

    def heads(has_prev):
        for h in range(N_KV_HEADS):
            odd = (h % PAIR) == 1
            kv_col = slice((h // PAIR) * LANES, (h // PAIR + 1) * LANES)
            k_l, k_r = _own_and_swapped(k_all[:, kv_col], odd, lo_kv)
            v_l, v_r = _own_and_swapped(v_all[:, kv_col], odd, lo_kv)
            wk = jnp.concatenate([k_l, k_r], axis=0).astype(BF16)
            vext = jnp.concatenate(
                [jnp.concatenate([v_l, ones_lo], axis=1),
                 jnp.concatenate([v_r, ones_hi], axis=1)], axis=0).astype(BF16)
            qh = jnp.concatenate(
                [q_ref[0, :, pl.ds((h * N_PAIRS + j) * LANES, LANES)] for j in range(N_PAIRS)],
                axis=0)
            qh = (qh * (SCALE * LOG2E)).astype(BF16)
            s_ref[h] = lax.dot_general(qh, wk, NT_DIMS, preferred_element_type=F32)

            for j in range(N_PAIRS):
                sinks = [sinks_ref[h, PAIR * j] * LOG2E, sinks_ref[h, PAIR * j + 1] * LOG2E]
                for c in range(WINDOW // ROW_CHUNK):
                    rows = pl.ds(j * WINDOW + c * ROW_CHUNK, ROW_CHUNK)
                    mask = lane_c <= (row_c + c * ROW_CHUNK)
                    e_parts = []
                    for gi in range(PAIR):
                        base = gi * 2 * WINDOW
                        cur = s_ref[h, rows, pl.ds(base + WINDOW, WINDOW)]
                        prev = s_ref[h, rows, pl.ds(base, WINDOW)] if has_prev else NEG_INF
                        s = jnp.where(mask, cur, prev)
                        m = jnp.maximum(jnp.max(s, axis=-1, keepdims=True), sinks[gi])
                        p = jnp.exp2(s - m)
                        p_ref[h, rows, pl.ds(base, WINDOW)] = jnp.where(mask, 0.0, p).astype(BF16)
                        p_ref[h, rows, pl.ds(base + WINDOW, WINDOW)] = (
                            jnp.where(mask, p, 0.0).astype(BF16))
                        e_parts.append(jnp.broadcast_to(jnp.exp2(sinks[gi] - m), (ROW_CHUNK, LANES)))
                    e_ref[h, rows, :] = jnp.where(lo_c, e_parts[0], e_parts[1])

            o_ext = jnp.dot(p_ref[h], vext, preferred_element_type=F32)
            o = o_ext[:, :LANES] / (o_ext[:, LANES:] + e_ref[h])
            for j in range(N_PAIRS):
                a_ref[0, :, pl.ds((h * N_PAIRS + j) * LANES, LANES)] = o[j * WINDOW:(j + 1) * WINDOW]

    pl.when(n == 0)(functools.partial(heads, False))
    pl.when(n > 0)(functools.partial(heads, True))

    @pl.when(n == pl.num_programs(1) - 1)
    def _():
        kwin_ref[0] = kc
        vwin_ref[0] = vc


def _attn_prompt(z3, sinks):
    b, s, _ = z3.shape
    nb = s // WINDOW
    kcol, vcol = OFF_K // D_KV, OFF_V // D_KV
    prev = lambda col: (lambda bi, n: (bi, jnp.maximum(n - 1, 0), col))
    cur = lambda col: (lambda bi, n: (bi, n, col))
    win_spec = pl.BlockSpec((1, WINDOW, D_KV), lambda bi, n: (bi, 0, 0))
    rows = N_PAIRS * WINDOW
    return pl.pallas_call(
        _attn_prompt_kernel,
        grid=(b, nb),
        in_specs=[
            pl.BlockSpec(memory_space=pltpu.SMEM),
            pl.BlockSpec((1, WINDOW, D_Q), lambda bi, n: (bi, n, 0)),
            pl.BlockSpec((1, WINDOW, D_KV), prev(kcol)),
            pl.BlockSpec((1, WINDOW, D_KV), cur(kcol)),
            pl.BlockSpec((1, WINDOW, D_KV), prev(vcol)),
            pl.BlockSpec((1, WINDOW, D_KV), cur(vcol)),
        ],
        out_specs=[pl.BlockSpec((1, WINDOW, D_Q), lambda bi, n: (bi, n, 0)), win_spec, win_spec],
        out_shape=[jax.ShapeDtypeStruct((b, s, D_Q), F32),
                   jax.ShapeDtypeStruct((b, WINDOW, D_KV), F32),
                   jax.ShapeDtypeStruct((b, WINDOW, D_KV), F32)],
        scratch_shapes=[pltpu.VMEM((N_KV_HEADS, rows, PAIR * 2 * WINDOW), F32),
                        pltpu.VMEM((N_KV_HEADS, rows, PAIR * 2 * WINDOW), BF16),
                        pltpu.VMEM((N_KV_HEADS, rows, LANES), F32)],
        compiler_params=pltpu.CompilerParams(
            dimension_semantics=("parallel", "arbitrary"), vmem_limit_bytes=VMEM_LIMIT),
        name="attn_prompt",
    )(sinks, z3, z3, z3, z3, z3)


def _attn_sample_kernel(sinkrow_ref, q_ref, kn_ref, vn_ref, ck_ref, cv_ref,
                        a_ref, kwin_ref, vwin_ref, *, bt, t):
    pad_rows = 2 * WINDOW - WINDOW - t
    pad = jnp.zeros((pad_rows, D_KV), F32)
    p_pad = jnp.zeros((pad_rows, N_Q_HEADS * t), F32)
    zeros_col = jnp.zeros((t, LANES), F32)
    lo = lax.broadcasted_iota(jnp.int32, (t, LANES), 1) < HEAD_DIM
    tok = lax.broadcasted_iota(jnp.int32, (t, N_Q_HEADS * t), 1) % t
    key = lax.broadcasted_iota(jnp.int32, (t, N_Q_HEADS * t), 0)
    new_mask = key <= tok
    sinkrow = sinkrow_ref[...]

    def place(piece_col, src_odd, dst_odd):
        own = jnp.where(lo != src_odd, piece_col, 0.0)
        return own if src_odd == dst_odd else pltpu.roll(own, HEAD_DIM, axis=1)

    def body(s, carry):
        kn, vn, ck, cv = kn_ref[s], vn_ref[s], ck_ref[s], cv_ref[s]
        k_all = jnp.concatenate([ck, kn, pad], axis=0).astype(BF16)
        v_all = jnp.concatenate([cv, vn, pad], axis=0).astype(BF16)
        q = q_ref[s] * SCALE
        blocks = []
        for h in range(N_KV_HEADS):
            for g in range(GQA_GROUP):
                head = h * GQA_GROUP + g
                piece = place(q[:, (head // PAIR) * LANES:(head // PAIR + 1) * LANES],
                              head % PAIR == 1, h % PAIR == 1)
                cols = [zeros_col] * (D_KV // LANES)
                cols[h // PAIR] = piece
                blocks.append(jnp.concatenate(cols, axis=1))
        wq_t = jnp.concatenate(blocks, axis=0).astype(BF16)
        s_t = lax.dot_general(k_all, wq_t, NT_DIMS, preferred_element_type=F32)
        top = jnp.where(new_mask, s_t[WINDOW:WINDOW + t], s_t[0:t])
        s_m = jnp.concatenate([top, s_t[t:WINDOW]], axis=0)
        m = jnp.maximum(jnp.max(s_m, axis=0, keepdims=True), sinkrow)
        p = jnp.exp(s_m - m)
        denom = jnp.sum(p, axis=0, keepdims=True) + jnp.exp(sinkrow - m)
        probs = p * (1.0 / denom)
        p_all = jnp.concatenate(
            [jnp.where(new_mask, 0.0, probs[0:t]), probs[t:WINDOW],
             jnp.where(new_mask, probs[0:t], 0.0), p_pad], axis=0).astype(BF16)
        o_full = lax.dot_general(p_all, v_all, TN_DIMS, preferred_element_type=F32)
        out_cols = []
        for c in range(N_Q_HEADS // PAIR):
            h = (PAIR * c) // GQA_GROUP
            kv_col = slice((h // PAIR) * LANES, (h // PAIR + 1) * LANES)
            even = place(o_full[(PAIR * c) * t:(PAIR * c + 1) * t, kv_col], h % PAIR == 1, False)
            odd = place(o_full[(PAIR * c + 1) * t:(PAIR * c + 2) * t, kv_col], h % PAIR == 1, True)
            out_cols.append(even + odd)
        a_ref[s] = jnp.concatenate(out_cols, axis=1)
        kwin_ref[s, 0:WINDOW - t, :] = ck[t:, :]
        kwin_ref[s, WINDOW - t:WINDOW, :] = kn
        vwin_ref[s, 0:WINDOW - t, :] = cv[t:, :]
        vwin_ref[s, WINDOW - t:WINDOW, :] = vn
        return carry

    lax.fori_loop(0, bt, body, 0, unroll=8)


def _attn_sample(z3, cache_k, cache_v, sinks, *, bt):
    b, t, _ = z3.shape
    kcol, vcol = OFF_K // D_KV, OFF_V // D_KV
    cache_spec = pl.BlockSpec((bt, WINDOW, D_KV), lambda i: (i, 0, 0))
    sinkrow = jnp.repeat(sinks.reshape(1, N_Q_HEADS), t, axis=1)
    return pl.pallas_call(
        functools.partial(_attn_sample_kernel, bt=bt, t=t),
        grid=(b // bt,),
        in_specs=[
            _const_spec((1, N_Q_HEADS * t)),
            pl.BlockSpec((bt, t, D_Q), lambda i: (i, 0, 0)),
            pl.BlockSpec((bt, t, D_KV), lambda i: (i, 0, kcol)),
            pl.BlockSpec((bt, t, D_KV), lambda i: (i, 0, vcol)),
            cache_spec, cache_spec,
        ],
        out_specs=[pl.BlockSpec((bt, t, D_Q), lambda i: (i, 0, 0)), cache_spec, cache_spec],
        out_shape=[jax.ShapeDtypeStruct((b, t, D_Q), F32),
                   jax.ShapeDtypeStruct((b, WINDOW, D_KV), F32),
                   jax.ShapeDtypeStruct((b, WINDOW, D_KV), F32)],
        compiler_params=pltpu.CompilerParams(
            dimension_semantics=("parallel",), vmem_limit_bytes=VMEM_LIMIT),
        name="attn_sample",
    )(sinkrow, z3, z3, z3, cache_k, cache_v)


def _mix_tail(pooled_parts, a, ga, gp, x, wpool_ref, pscale_ref, wout_ref, gpost_ref, gpre_ref,
              x1_ref, h2_ref):
    parts = [jnp.dot(pooled_parts[g].astype(BF16), wpool_ref[g], preferred_element_type=F32)
             for g in range(len(POOL_WINDOWS))]
    p = jnp.concatenate(parts, axis=-1) * pscale_ref[...]
    mixed = jax.nn.sigmoid(ga) * a + jax.nn.sigmoid(gp) * p
    y = jnp.dot(mixed.astype(BF16), wout_ref[...], preferred_element_type=F32)
    x1 = x + _rmsnorm(y, gpost_ref[...])
    x1_ref[...] = x1
    h2_ref[...] = _rmsnorm(x1, gpre_ref[...]).astype(BF16)


def _mix_prompt_kernel(a_ref, ga_ref, gp_ref, u_ref, halo_ref, x_ref, wpool_ref, pscale_ref,
                       wout_ref, gpost_ref, gpre_ref, x1_ref, h2_ref, pool_ref,
                       ext_ref, sum2_ref, sum4_ref, sum8_ref, *, bm, blocks_per_seq):
    assert POOL_WINDOWS == (2, 4, 8, 16)
    blk = pl.program_id(0) % blocks_per_seq
    pad = jnp.zeros((POOL_PAD, D_POOL), F32)
    ext_ref[0:POOL_PAD, :] = pad
    sum2_ref[0:POOL_PAD, :] = pad
    sum4_ref[0:POOL_PAD, :] = pad[:, POOL_GROUP:]
    ext_ref[POOL_PAD:POOL_PAD + HALO, :] = jnp.where(blk == 0, 0.0, halo_ref[...])
    ext_ref[POOL_PAD + HALO:, :] = u_ref[...]
    n = HALO + bm
    body = pl.ds(POOL_PAD, n)
    shifted = lambda s: pl.ds(POOL_PAD - s, n)
    g1 = pl.ds(POOL_GROUP, D_POOL - POOL_GROUP)
    sum2_ref[body, :] = ext_ref[body, :] + ext_ref[shifted(1), :]
    sum4_ref[body, :] = sum2_ref[body, g1] + sum2_ref[shifted(2), g1]
    sum8_ref[body, :] = sum4_ref[body, POOL_GROUP:] + sum4_ref[shifted(4), POOL_GROUP:]
    first = POOL_PAD + HALO
    rows = pl.ds(first, bm)
    sum16 = sum8_ref[rows, POOL_GROUP:] + sum8_ref[pl.ds(first - 8, bm), POOL_GROUP:]
    wsums = [sum2_ref[rows, 0:POOL_GROUP], sum4_ref[rows, 0:POOL_GROUP],
             sum8_ref[rows, 0:POOL_GROUP], sum16]
    pos = blk * bm + lax.broadcasted_iota(jnp.int32, (bm, 1), 0)
    pooled = []
    for g, w in enumerate(POOL_WINDOWS):
        inv_cnt = 1.0 / jnp.minimum(pos + 1, w).astype(F32)
        pooled.append(wsums[g] * inv_cnt - u_ref[:, pl.ds(g * POOL_GROUP, POOL_GROUP)])
    _mix_tail(pooled, a_ref[...], ga_ref[...], gp_ref[...], x_ref[...], wpool_ref, pscale_ref,
              wout_ref, gpost_ref, gpre_ref, x1_ref, h2_ref)

    @pl.when(blk == blocks_per_seq - 1)
    def _():
        pool_ref[0] = ext_ref[pl.ds(first + bm - POOL_HIST, POOL_HIST), :]


def _mix_prompt(a2d, z2d, x2d, wpool_bf, pscale, wout_bf, gpost, gpre, *, bm, seq):
    m = x2d.shape[0]
    blocks_per_seq = seq // bm
    row_spec = lambda col: pl.BlockSpec((bm, D_MODEL), lambda i: (i, col))
    halo_blocks = bm // HALO
    return pl.pallas_call(
        functools.partial(_mix_prompt_kernel, bm=bm, blocks_per_seq=blocks_per_seq),
        grid=(m // bm,),
        in_specs=[
            row_spec(0),
            row_spec(OFF_GA // D_MODEL),
            row_spec(OFF_GP // D_MODEL),
            pl.BlockSpec((bm, D_POOL), lambda i: (i, OFF_U // D_POOL)),
            pl.BlockSpec((HALO, D_POOL),
                         lambda i: (jnp.maximum(i * halo_blocks - 1, 0), OFF_U // D_POOL)),
            row_spec(0),
            _const_spec((len(POOL_WINDOWS), POOL_GROUP, POOL_OUT_GROUP)),
            _const_spec((1, D_MODEL)),
            _const_spec((D_MODEL, D_MODEL)),
            _const_spec((1, D_MODEL)),
            _const_spec((1, D_MODEL)),
        ],
        out_specs=[row_spec(0), row_spec(0),
                   pl.BlockSpec((1, POOL_HIST, D_POOL), lambda i: (i // blocks_per_seq, 0, 0))],
        out_shape=[jax.ShapeDtypeStruct((m, D_MODEL), F32),
                   jax.ShapeDtypeStruct((m, D_MODEL), BF16),
                   jax.ShapeDtypeStruct((m // seq, POOL_HIST, D_POOL), F32)],
        scratch_shapes=[pltpu.VMEM((POOL_PAD + HALO + bm, D_POOL), F32),
                        pltpu.VMEM((POOL_PAD + HALO + bm, D_POOL), F32),
                        pltpu.VMEM((POOL_PAD + HALO + bm, D_POOL - POOL_GROUP), F32),
                        pltpu.VMEM((POOL_PAD + HALO + bm, D_POOL - 2 * POOL_GROUP), F32)],
        compiler_params=pltpu.CompilerParams(
            dimension_semantics=("arbitrary",), vmem_limit_bytes=VMEM_LIMIT),
        name="mix_prompt",
    )(a2d, z2d, z2d, z2d, z2d, x2d, wpool_bf, pscale, wout_bf, gpost, gpre)


def _mix_sample_kernel(a_ref, ga_ref, gp_ref, u_ref, hist_ref, x_ref, wpool_ref, pscale_ref,
                       wout_ref, gpost_ref, gpre_ref, x1_ref, h2_ref, pool_ref, ext_ref,
                       *, bt, t, pos0):
    ext_ref[:, HALO - POOL_HIST:HALO, :] = hist_ref[...]
    ext_ref[:, HALO:, :] = u_ref[...].reshape(bt, t, D_POOL)
    pooled = []
    for g, w in enumerate(POOL_WINDOWS):
        lanes = pl.ds(g * POOL_GROUP, POOL_GROUP)
        wsum = ext_ref[:, pl.ds(HALO, t), lanes]
        for s in range(1, w):
            wsum = wsum + ext_ref[:, pl.ds(HALO - s, t), lanes]
        assert pos0 + 1 >= w
        pooled.append(wsum.reshape(bt * t, POOL_GROUP) * (1.0 / w) - u_ref[:, lanes])
    _mix_tail(pooled, a_ref[...], ga_ref[...], gp_ref[...], x_ref[...], wpool_ref, pscale_ref,
              wout_ref, gpost_ref, gpre_ref, x1_ref, h2_ref)
    pool_ref[...] = ext_ref[:, pl.ds(HALO + t - POOL_HIST, POOL_HIST), :]


def _mix_sample(a2d, z2d, x2d, hist, wpool_bf, pscale, wout_bf, gpost, gpre, *, bt, t):
    m = x2d.shape[0]
    bm = bt * t
    row_spec = lambda col: pl.BlockSpec((bm, D_MODEL), lambda i: (i, col))
    return pl.pallas_call(
        functools.partial(_mix_sample_kernel, bt=bt, t=t, pos0=PAST_LEN),
        grid=(m // bm,),
        in_specs=[
            row_spec(0),
            row_spec(OFF_GA // D_MODEL),
            row_spec(OFF_GP // D_MODEL),
            pl.BlockSpec((bm, D_POOL), lambda i: (i, OFF_U // D_POOL)),
            pl.BlockSpec((bt, POOL_HIST, D_POOL), lambda i: (i, 0, 0)),
            row_spec(0),
            _const_spec((len(POOL_WINDOWS), POOL_GROUP, POOL_OUT_GROUP)),
            _const_spec((1, D_MODEL)),
            _const_spec((D_MODEL, D_MODEL)),
            _const_spec((1, D_MODEL)),
            _const_spec((1, D_MODEL)),
        ],
        out_specs=[row_spec(0), row_spec(0),
                   pl.BlockSpec((bt, POOL_HIST, D_POOL), lambda i: (i, 0, 0))],
        out_shape=[jax.ShapeDtypeStruct((m, D_MODEL), F32),
                   jax.ShapeDtypeStruct((m, D_MODEL), BF16),
                   jax.ShapeDtypeStruct((m // t, POOL_HIST, D_POOL), F32)],
        scratch_shapes=[pltpu.VMEM((bt, HALO + t, D_POOL), F32)],
        compiler_params=pltpu.CompilerParams(
            dimension_semantics=("parallel",), vmem_limit_bytes=VMEM_LIMIT),
        name="mix_sample",
    )(a2d, z2d, z2d, z2d, hist, x2d, wpool_bf, pscale, wout_bf, gpost, gpre)


MLP_TAIL_ROWS = 64


def _mlp_kernel(h2_ref, x1_slice_ref, wup_ref, wdn_ref, g_ref, y_ref, x1_ref, *, bm, slice_rows):
    j = pl.program_id(1)

    @pl.when(j == 0)
    def _():
        y_ref[...] = jnp.zeros_like(y_ref)

    x1_ref[pl.ds(pl.multiple_of(j * slice_rows, slice_rows), slice_rows), :] = x1_slice_ref[...]
    hid = jnp.dot(h2_ref[...], wup_ref[...], preferred_element_type=F32)
    hid = jnp.square(jnp.maximum(hid, 0.0)).astype(BF16)
    y_ref[...] += jnp.dot(hid, wdn_ref[...], preferred_element_type=F32)

    @pl.when(j == pl.num_programs(1) - 1)
    def _():
        def tail(c, carry):
            rows = pl.ds(pl.multiple_of(c * MLP_TAIL_ROWS, MLP_TAIL_ROWS), MLP_TAIL_ROWS)
            y_ref[rows, :] = x1_ref[rows, :] + _rmsnorm(y_ref[rows, :], g_ref[...])
            return carry

        lax.fori_loop(0, bm // MLP_TAIL_ROWS, tail, 0)


def _mlp_cast_kernel(h2_ref, x1_ref, wup_ref, wdn_ref, g_ref, y_ref, wup_bf_ref, wdn_bf_ref):
    j = pl.program_id(0)

    @pl.when(j == 0)
    def _():
        y_ref[...] = jnp.zeros_like(y_ref)

    wup = wup_ref[...].astype(BF16)
    wdn = wdn_ref[...].astype(BF16)
    wup_bf_ref[...] = wup
    wdn_bf_ref[...] = wdn
    hid = jnp.dot(h2_ref[...], wup, preferred_element_type=F32)
    hid = jnp.square(jnp.maximum(hid, 0.0)).astype(BF16)
    y_ref[...] += jnp.dot(hid, wdn, preferred_element_type=F32)

    @pl.when(j == pl.num_programs(0) - 1)
    def _():
        y_ref[...] = x1_ref[...] + _rmsnorm(y_ref[...], g_ref[...])


def _mlp_cast(h2, x1, wup_f32, wdn_f32, g, *, fc):
    m = x1.shape[0]
    return pl.pallas_call(
        _mlp_cast_kernel,
        grid=(D_FF // fc,),
        in_specs=[
            _const_spec((m, D_MODEL)),
            _const_spec((m, D_MODEL)),
            pl.BlockSpec((D_MODEL, fc), lambda j: (0, j)),
            pl.BlockSpec((fc, D_MODEL), lambda j: (j, 0)),
            _const_spec((1, D_MODEL)),
        ],
        out_specs=[pl.BlockSpec((m, D_MODEL), lambda j: (0, 0)),
                   pl.BlockSpec((D_MODEL, fc), lambda j: (0, j)),
                   pl.BlockSpec((fc, D_MODEL), lambda j: (j, 0))],
        out_shape=[jax.ShapeDtypeStruct((m, D_MODEL), F32),
                   jax.ShapeDtypeStruct((D_MODEL, D_FF), BF16),
                   jax.ShapeDtypeStruct((D_FF, D_MODEL), BF16)],
        compiler_params=pltpu.CompilerParams(
            dimension_semantics=("arbitrary",), vmem_limit_bytes=VMEM_LIMIT),
        name="mlp_cast",
    )(h2, x1, wup_f32, wdn_f32, g)


def _mlp(h2, x1, wup_bf, wdn_bf, g, *, bm, fc):
    m = x1.shape[0]
    n_ff = D_FF // fc
    slice_rows = bm // n_ff
    return pl.pallas_call(
        functools.partial(_mlp_kernel, bm=bm, slice_rows=slice_rows),
        grid=(m // bm, n_ff),
        in_specs=[
            pl.BlockSpec((bm, D_MODEL), lambda i, j: (i, 0)),
            pl.BlockSpec((slice_rows, D_MODEL), lambda i, j: (i * n_ff + j, 0)),
            pl.BlockSpec((D_MODEL, fc), lambda i, j: (0, j)),
            pl.BlockSpec((fc, D_MODEL), lambda i, j: (j, 0)),
            _const_spec((1, D_MODEL)),
        ],
        out_specs=pl.BlockSpec((bm, D_MODEL), lambda i, j: (i, 0)),
        out_shape=jax.ShapeDtypeStruct((m, D_MODEL), F32),
        scratch_shapes=[pltpu.VMEM((bm, D_MODEL), F32)],
        compiler_params=pltpu.CompilerParams(
            dimension_semantics=("parallel", "arbitrary"), vmem_limit_bytes=VMEM_LIMIT),
        name="mlp",
    )(h2, x1, wup_bf, wdn_bf, g)


def kernel(x_prompt, x_sample, cache_k_win, cache_v_win, state_pool, norm_attn_pre, norm_attn_post,
           w_in, attn_sinks, w_pool, pool_scale, w_out, norm_mlp_pre, norm_mlp_post, w_up, w_down):
    depth = w_in.shape[0]
    assert depth == 1
    b, s, _ = x_prompt.shape
    bs, t, _ = x_sample.shape

    l = 0
    w_pool_bf = w_pool[l].astype(BF16)
    w_out_bf = w_out[l].astype(BF16)
    row = lambda v: v[l].reshape(1, D_MODEL)
    g_attn_pre, g_attn_post = row(norm_attn_pre), row(norm_attn_post)
    g_mlp_pre, g_mlp_post = row(norm_mlp_pre), row(norm_mlp_post)
    pscale = row(pool_scale)
    sinks = attn_sinks[l]

    xs = x_sample.reshape(bs * t, D_MODEL)
    zs, w_in_bf = _in_proj_cast(xs, g_attn_pre, w_in[l])
    ck = cache_k_win[l].reshape(bs, WINDOW, D_KV)
    cv = cache_v_win[l].reshape(bs, WINDOW, D_KV)
    as_, ks, vs = _attn_sample(zs.reshape(bs, t, D_IN), ck, cv, sinks, bt=8)
    x1s, h2s, pools = _mix_sample(as_.reshape(bs * t, D_Q), zs, xs, state_pool[l], w_pool_bf, pscale,
                                  w_out_bf, g_attn_post, g_mlp_pre, bt=32, t=t)
    ys, w_up_bf, w_down_bf = _mlp_cast(h2s, x1s, w_up[l], w_down[l], g_mlp_post, fc=512)

    xp = x_prompt.reshape(b * s, D_MODEL)
    zp = _in_proj(xp, g_attn_pre, w_in_bf, bm=1024, bn=1536)
    ap, kp, vp = _attn_prompt(zp.reshape(b, s, D_IN), sinks)
    x1p, h2p, poolp = _mix_prompt(ap.reshape(b * s, D_Q), zp, xp, w_pool_bf, pscale, w_out_bf,
                                  g_attn_post, g_mlp_pre, bm=256, seq=s)
    yp = _mlp(h2p, x1p, w_up_bf, w_down_bf, g_mlp_post, bm=1024, fc=1024)

    kv_shape = lambda nb: (1, nb, WINDOW, N_KV_HEADS, HEAD_DIM)
    return (yp.reshape(b, s, D_MODEL), ys.reshape(bs, t, D_MODEL),
            kp.reshape(kv_shape(b)), vp.reshape(kv_shape(b)), poolp[None],
            ks.reshape(kv_shape(bs)), vs.reshape(kv_shape(bs)), pools[None])
```

```python
import functools

import jax
import jax.numpy as jnp
from jax import lax
from jax.experimental import pallas as pl
from jax.experimental.pallas import tpu as pltpu

F32 = jnp.float32
BF16 = jnp.bfloat16

D_MODEL = 2048
HEAD_DIM = 64
N_KV_HEADS = 4
GQA_GROUP = 8
N_Q_HEADS = N_KV_HEADS * GQA_GROUP
WINDOW = 128
D_Q = 2048
D_KV = 256
POOL_WINDOWS = (2, 4, 8, 16)
D_POOL = 1024
POOL_GROUP = 256
POOL_OUT_GROUP = 512
POOL_HIST = 15
D_FF = 8192
D_IN = D_Q + 2 * D_KV + D_POOL + 2 * D_MODEL
PAST_LEN = 8192
EPS = 1e-6
NEG_INF = -1e30
SCALE = HEAD_DIM ** -0.5
LOG2E = 1.4426950408889634

LANES = 128
PAIR = 2
N_PAIRS = GQA_GROUP // PAIR
ROW_CHUNK = 16

OFF_Q, OFF_GA, OFF_GP, OFF_U, OFF_K, OFF_V = 0, 2048, 4096, 6144, 7168, 7424
HALO = 16
POOL_PAD = 8

VMEM_LIMIT = 60 * 1024 * 1024

NT_DIMS = (((1,), (1,)), ((), ()))
TN_DIMS = (((0,), (0,)), ((), ()))


def _rmsnorm(x, g):
    r = lax.rsqrt(jnp.mean(x * x, axis=-1, keepdims=True) + EPS)
    return x * r * g


def _const_spec(shape):
    return pl.BlockSpec(shape, lambda *_: (0,) * len(shape), pipeline_mode=pl.Buffered(1))


def _in_proj_kernel(x_ref, g_ref, w_ref, z_ref, h_ref):
    @pl.when(pl.program_id(1) == 0)
    def _():
        h_ref[...] = _rmsnorm(x_ref[...], g_ref[...]).astype(BF16)

    z_ref[...] = jnp.dot(h_ref[...], w_ref[...], preferred_element_type=F32)


def _in_proj_cast_kernel(x_ref, g_ref, w_ref, z_ref, wbf_ref, h_ref):
    @pl.when(pl.program_id(0) == 0)
    def _():
        h_ref[...] = _rmsnorm(x_ref[...], g_ref[...]).astype(BF16)

    w = w_ref[...].astype(BF16)
    wbf_ref[...] = w
    z_ref[...] = jnp.dot(h_ref[...], w, preferred_element_type=F32)


def _in_proj_cast(x2d, g, w_f32):
    m = x2d.shape[0]
    bn = 512
    src_bounds = (D_Q, D_Q + 2 * D_KV, D_Q + 2 * D_KV + D_POOL, D_Q + 2 * D_KV + D_POOL + D_MODEL)
    assert all(off % bn == 0 for off in src_bounds + (OFF_GA, OFF_GP, OFF_U, OFF_K))

    def reordered(j):
        return jnp.where(j < 4, j, jnp.where(j == 4, OFF_K // bn, jnp.where(j < 7, j + 7, j - 3)))

    return pl.pallas_call(
        _in_proj_cast_kernel,
        grid=(D_IN // bn,),
        in_specs=[
            _const_spec((m, D_MODEL)),
            _const_spec((1, D_MODEL)),
            pl.BlockSpec((D_MODEL, bn), lambda j: (0, j)),
        ],
        out_specs=[pl.BlockSpec((m, bn), lambda j: (0, reordered(j))),
                   pl.BlockSpec((D_MODEL, bn), lambda j: (0, reordered(j)))],
        out_shape=[jax.ShapeDtypeStruct((m, D_IN), F32),
                   jax.ShapeDtypeStruct((D_MODEL, D_IN), BF16)],
        scratch_shapes=[pltpu.VMEM((m, D_MODEL), BF16)],
        compiler_params=pltpu.CompilerParams(
            dimension_semantics=("arbitrary",), vmem_limit_bytes=VMEM_LIMIT),
        name="in_proj_cast",
    )(x2d, g, w_f32)


def _in_proj(x2d, g, w_bf, *, bm, bn):
    m = x2d.shape[0]
    return pl.pallas_call(
        _in_proj_kernel,
        grid=(m // bm, D_IN // bn),
        in_specs=[
            pl.BlockSpec((bm, D_MODEL), lambda i, j: (i, 0)),
            _const_spec((1, D_MODEL)),
            pl.BlockSpec((D_MODEL, bn), lambda i, j: (0, j)),
        ],
        out_specs=pl.BlockSpec((bm, bn), lambda i, j: (i, j)),
        out_shape=jax.ShapeDtypeStruct((m, D_IN), F32),
        scratch_shapes=[pltpu.VMEM((bm, D_MODEL), BF16)],
        compiler_params=pltpu.CompilerParams(
            dimension_semantics=("parallel", "arbitrary"), vmem_limit_bytes=VMEM_LIMIT),
        name="in_proj",
    )(x2d, g, w_bf)


def _own_and_swapped(slab, odd, lo):
    own = jnp.where(lo != odd, slab, 0.0)
    swapped = pltpu.roll(own, HEAD_DIM, axis=1)
    return (swapped, own) if odd else (own, swapped)


ATTN_BLOCKS = 8
ATTN_BUFFERS = 8


def _attn_prompt_kernel(sinks_ref, q_ref, kp_ref, kc_ref, vp_ref, vc_ref,
                        a_ref, kwin_ref, vwin_ref, s_ref, p_ref, e_ref):
    n = pl.program_id(1)
    k_rows = jnp.concatenate([kp_ref[0], kc_ref[0]], axis=0)
    v_rows = jnp.concatenate([vp_ref[0], vc_ref[0]], axis=0)
    first_prev_bias = jnp.where(n > 0, 0.0, NEG_INF).astype(F32)
    lo_kv = lax.broadcasted_iota(jnp.int32, (2 * WINDOW, LANES), 1) < HEAD_DIM
    ones_lo = jnp.where(lo_kv, 1.0, 0.0)
    ones_hi = 1.0 - ones_lo
    lo_c = lax.broadcasted_iota(jnp.int32, (ROW_CHUNK, LANES), 1) < HEAD_DIM
    lane_c = lax.broadcasted_iota(jnp.int32, (ROW_CHUNK, LANES), 1)
    row_c = lax.broadcasted_iota(jnp.int32, (ROW_CHUNK, LANES), 0)

    for blk in range(ATTN_BLOCKS):
        k_all = k_rows[blk * WINDOW:(blk + 2) * WINDOW]
        v_all = v_rows[blk * WINDOW:(blk + 2) * WINDOW]
        q_rows = pl.ds(blk * WINDOW, WINDOW)
        for h in range(N_KV_HEADS):
            buf = (blk * N_KV_HEADS + h) % ATTN_BUFFERS
            odd = (h % PAIR) == 1
            kv_col = slice((h // PAIR) * LANES, (h // PAIR + 1) * LANES)
            k_l, k_r = _own_and_swapped(k_all[:, kv_col], odd, lo_kv)
            v_l, v_r = _own_and_swapped(v_all[:, kv_col], odd, lo_kv)
            wk = jnp.concatenate([k_l, k_r], axis=0).astype(BF16)
            vext = jnp.concatenate(
                [jnp.concatenate([v_l, ones_lo], axis=1),
                 jnp.concatenate([v_r, ones_hi], axis=1)], axis=0).astype(BF16)
            qh = jnp.concatenate(
                [q_ref[0, q_rows, pl.ds((h * N_PAIRS + j) * LANES, LANES)] for j in range(N_PAIRS)],
                axis=0)
            qh = (qh * (SCALE * LOG2E)).astype(BF16)
            s_ref[buf] = lax.dot_general(qh, wk, NT_DIMS, preferred_element_type=F32)

            for j in range(N_PAIRS):
                sinks = [sinks_ref[h, PAIR * j] * LOG2E, sinks_ref[h, PAIR * j + 1] * LOG2E]
                for c in range(WINDOW // ROW_CHUNK):
                    rows = pl.ds(j * WINDOW + c * ROW_CHUNK, ROW_CHUNK)
                    mask = lane_c <= (row_c + c * ROW_CHUNK)
                    e_parts = []
                    for gi in range(PAIR):
                        base = gi * 2 * WINDOW
                        cur = s_ref[buf, rows, pl.ds(base + WINDOW, WINDOW)]
                        prev = s_ref[buf, rows, pl.ds(base, WINDOW)]
                        if blk == 0:
                            prev = prev + first_prev_bias
                        s = jnp.where(mask, cur, prev)
                        m = jnp.maximum(jnp.max(s, axis=-1, keepdims=True), sinks[gi])
                        p = jnp.exp2(s - m)
                        p_ref[buf, rows, pl.ds(base, WINDOW)] = jnp.where(mask, 0.0, p).astype(BF16)
                        p_ref[buf, rows, pl.ds(base + WINDOW, WINDOW)] = (
                            jnp.where(mask, p, 0.0).astype(BF16))
                        e_parts.append(jnp.broadcast_to(jnp.exp2(sinks[gi] - m), (ROW_CHUNK, LANES)))
                    e_ref[buf, rows, :] = jnp.where(lo_c, e_parts[0], e_parts[1])

            o_ext = jnp.dot(p_ref[buf], vext, preferred_element_type=F32)
            o = o_ext[:, :LANES] / (o_ext[:, LANES:] + e_ref[buf])
            for j in range(N_PAIRS):
                a_ref[0, q_rows, pl.ds((h * N_PAIRS + j) * LANES, LANES)] = o[j * WINDOW:(j + 1) * WINDOW]

    @pl.when(n == pl.num_programs(1) - 1)
    def _():
        kwin_ref[0] = kc_ref[0, pl.ds((ATTN_BLOCKS - 1) * WINDOW, WINDOW), :]
        vwin_ref[0] = vc_ref[0, pl.ds((ATTN_BLOCKS - 1) * WINDOW, WINDOW), :]


def _attn_prompt(z3, sinks):
    b, s, _ = z3.shape
    step_rows = ATTN_BLOCKS * WINDOW
    kcol, vcol = OFF_K // D_KV, OFF_V // D_KV
    prev = lambda col: (lambda bi, n: (bi, jnp.maximum(n * ATTN_BLOCKS - 1, 0), col))
    cur = lambda col: (lambda bi, n: (bi, n, col))
    win_spec = pl.BlockSpec((1, WINDOW, D_KV), lambda bi, n: (bi, 0, 0))
    rows = N_PAIRS * WINDOW
    n_buf = ATTN_BUFFERS
    return pl.pallas_call(
        _attn_prompt_kernel,
        grid=(b, s // step_rows),
        in_specs=[
            pl.BlockSpec(memory_space=pltpu.SMEM),
            pl.BlockSpec((1, step_rows, D_Q), lambda bi, n: (bi, n, 0)),
            pl.BlockSpec((1, WINDOW, D_KV), prev(kcol)),
            pl.BlockSpec((1, step_rows, D_KV), cur(kcol)),
            pl.BlockSpec((1, WINDOW, D_KV), prev(vcol)),
            pl.BlockSpec((1, step_rows, D_KV), cur(vcol)),
        ],
        out_specs=[pl.BlockSpec((1, step_rows, D_Q), lambda bi, n: (bi, n, 0)), win_spec, win_spec],
        out_shape=[jax.ShapeDtypeStruct((b, s, D_Q), F32),
                   jax.ShapeDtypeStruct((b, WINDOW, D_KV), F32),
                   jax.ShapeDtypeStruct((b, WINDOW, D_KV), F32)],
        scratch_shapes=[pltpu.VMEM((n_buf, rows, PAIR * 2 * WINDOW), F32),
                        pltpu.VMEM((n_buf, rows, PAIR * 2 * WINDOW), BF16),
                        pltpu.VMEM((n_buf, rows, LANES), F32)],
        compiler_params=pltpu.CompilerParams(
            dimension_semantics=("parallel", "arbitrary"), vmem_limit_bytes=VMEM_LIMIT),
        name="attn_prompt",
    )(sinks, z3, z3, z3, z3, z3)


def _attn_sample_kernel(sinkrow_ref, q_ref, kn_ref, vn_ref, ck_ref, cv_ref,
                        a_ref, kwin_ref, vwin_ref, *, bt, t):
    pad_rows = 2 * WINDOW - WINDOW - t
    pad = jnp.zeros((pad_rows, D_KV), F32)
    p_pad = jnp.zeros((pad_rows, N_Q_HEADS * t), F32)
    zeros_col = jnp.zeros((t, LANES), F32)
    lo = lax.broadcasted_iota(jnp.int32, (t, LANES), 1) < HEAD_DIM
    tok = lax.broadcasted_iota(jnp.int32, (t, N_Q_HEADS * t), 1) % t
    key = lax.broadcasted_iota(jnp.int32, (t, N_Q_HEADS * t), 0)
    new_mask = key <= tok
    sinkrow = sinkrow_ref[...]

    def place(piece_col, src_odd, dst_odd):
        own = jnp.where(lo != src_odd, piece_col, 0.0)
        return own if src_odd == dst_odd else pltpu.roll(own, HEAD_DIM, axis=1)

    def body(s, carry):
        kn, vn, ck, cv = kn_ref[s], vn_ref[s], ck_ref[s], cv_ref[s]
        k_all = jnp.concatenate([ck, kn, pad], axis=0).astype(BF16)
        v_all = jnp.concatenate([cv, vn, pad], axis=0).astype(BF16)
        q = q_ref[s] * SCALE
        blocks = []
        for h in range(N_KV_HEADS):
            for g in range(GQA_GROUP):
                head = h * GQA_GROUP + g
                piece = place(q[:, (head // PAIR) * LANES:(head // PAIR + 1) * LANES],
                              head % PAIR == 1, h % PAIR == 1)
                cols = [zeros_col] * (D_KV // LANES)
                cols[h // PAIR] = piece
                blocks.append(jnp.concatenate(cols, axis=1))
        wq_t = jnp.concatenate(blocks, axis=0).astype(BF16)
        s_t = lax.dot_general(k_all, wq_t, NT_DIMS, preferred_element_type=F32)
        top = jnp.where(new_mask, s_t[WINDOW:WINDOW + t], s_t[0:t])
        s_m = jnp.concatenate([top, s_t[t:WINDOW]], axis=0)
        m = jnp.maximum(jnp.max(s_m, axis=0, keepdims=True), sinkrow)
        p = jnp.exp(s_m - m)
        denom = jnp.sum(p, axis=0, keepdims=True) + jnp.exp(sinkrow - m)
        probs = p * (1.0 / denom)
        p_all = jnp.concatenate(
            [jnp.where(new_mask, 0.0, probs[0:t]), probs[t:WINDOW],
             jnp.where(new_mask, probs[0:t], 0.0), p_pad], axis=0).astype(BF16)
        o_full = lax.dot_general(p_all, v_all, TN_DIMS, preferred_element_type=F32)
        out_cols = []
        for c in range(N_Q_HEADS // PAIR):
            h = (PAIR * c) // GQA_GROUP
            kv_col = slice((h // PAIR) * LANES, (h // PAIR + 1) * LANES)
            even = place(o_full[(PAIR * c) * t:(PAIR * c + 1) * t, kv_col], h % PAIR == 1, False)
            odd = place(o_full[(PAIR * c + 1) * t:(PAIR * c + 2) * t, kv_col], h % PAIR == 1, True)
            out_cols.append(even + odd)
        a_ref[s] = jnp.concatenate(out_cols, axis=1)
        kwin_ref[s, 0:WINDOW - t, :] = ck[t:, :]
        kwin_ref[s, WINDOW - t:WINDOW, :] = kn
        vwin_ref[s, 0:WINDOW - t, :] = cv[t:, :]
        vwin_ref[s, WINDOW - t:WINDOW, :] = vn
        return carry

    lax.fori_loop(0, bt, body, 0, unroll=8)


def _attn_sample(z3, cache_k, cache_v, sinks, *, bt):
    b, t, _ = z3.shape
    kcol, vcol = OFF_K // D_KV, OFF_V // D_KV
    cache_spec = pl.BlockSpec((bt, WINDOW, D_KV), lambda i: (i, 0, 0))
    sinkrow = jnp.repeat(sinks.reshape(1, N_Q_HEADS), t, axis=1)
    return pl.pallas_call(
        functools.partial(_attn_sample_kernel, bt=bt, t=t),
        grid=(b // bt,),
        in_specs=[
            _const_spec((1, N_Q_HEADS * t)),
            pl.BlockSpec((bt, t, D_Q), lambda i: (i, 0, 0)),
            pl.BlockSpec((bt, t, D_KV), lambda i: (i, 0, kcol)),
            pl.BlockSpec((bt, t, D_KV), lambda i: (i, 0, vcol)),
            cache_spec, cache_spec,
        ],
        out_specs=[pl.BlockSpec((bt, t, D_Q), lambda i: (i, 0, 0)), cache_spec, cache_spec],
        out_shape=[jax.ShapeDtypeStruct((b, t, D_Q), F32),
                   jax.ShapeDtypeStruct((b, WINDOW, D_KV), F32),
                   jax.ShapeDtypeStruct((b, WINDOW, D_KV), F32)],
        compiler_params=pltpu.CompilerParams(
            dimension_semantics=("parallel",), vmem_limit_bytes=VMEM_LIMIT),
        name="attn_sample",
    )(sinkrow, z3, z3, z3, cache_k, cache_v)


def _mix_tail(pooled_parts, a, ga, gp, x, wpool_ref, pscale_ref, wout_ref, gpost_ref, gpre_ref,
              x1_ref, h2_ref):
    parts = [jnp.dot(pooled_parts[g].astype(BF16), wpool_ref[g], preferred_element_type=F32)
             for g in range(len(POOL_WINDOWS))]
    p = jnp.concatenate(parts, axis=-1) * pscale_ref[...]
    mixed = jax.nn.sigmoid(ga) * a + jax.nn.sigmoid(gp) * p
    y = jnp.dot(mixed.astype(BF16), wout_ref[...], preferred_element_type=F32)
    x1 = x + _rmsnorm(y, gpost_ref[...])
    x1_ref[...] = x1
    h2_ref[...] = _rmsnorm(x1, gpre_ref[...]).astype(BF16)


def _mix_prompt_kernel(a_ref, ga_ref, gp_ref, u_ref, halo_ref, x_ref, wpool_ref, pscale_ref,
                       wout_ref, gpost_ref, gpre_ref, x1_ref, h2_ref, pool_ref,
                       ext_ref, sum2_ref, sum4_ref, sum8_ref, *, bm, blocks_per_seq):
    assert POOL_WINDOWS == (2, 4, 8, 16)
    blk = pl.program_id(0) % blocks_per_seq
    pad = jnp.zeros((POOL_PAD, D_POOL), F32)
    ext_ref[0:POOL_PAD, :] = pad
    sum2_ref[0:POOL_PAD, :] = pad
    sum4_ref[0:POOL_PAD, :] = pad[:, POOL_GROUP:]
    ext_ref[POOL_PAD:POOL_PAD + HALO, :] = jnp.where(blk == 0, 0.0, halo_ref[...])
    ext_ref[POOL_PAD + HALO:, :] = u_ref[...]
    n = HALO + bm
    body = pl.ds(POOL_PAD, n)
    shifted = lambda s: pl.ds(POOL_PAD - s, n)
    g1 = pl.ds(POOL_GROUP, D_POOL - POOL_GROUP)
    sum2_ref[body, :] = ext_ref[body, :] + ext_ref[shifted(1), :]
    sum4_ref[body, :] = sum2_ref[body, g1] + sum2_ref[shifted(2), g1]
    sum8_ref[body, :] = sum4_ref[body, POOL_GROUP:] + sum4_ref[shifted(4), POOL_GROUP:]
    first = POOL_PAD + HALO
    rows = pl.ds(first, bm)
    sum16 = sum8_ref[rows, POOL_GROUP:] + sum8_ref[pl.ds(first - 8, bm), POOL_GROUP:]
    wsums = [sum2_ref[rows, 0:POOL_GROUP], sum4_ref[rows, 0:POOL_GROUP],
             sum8_ref[rows, 0:POOL_GROUP], sum16]
    pos = blk * bm + lax.broadcasted_iota(jnp.int32, (bm, 1), 0)
    pooled = []
    for g, w in enumerate(POOL_WINDOWS):
        inv_cnt = 1.0 / jnp.minimum(pos + 1, w).astype(F32)
        pooled.append(wsums[g] * inv_cnt - u_ref[:, pl.ds(g * POOL_GROUP, POOL_GROUP)])
    _mix_tail(pooled, a_ref[...], ga_ref[...], gp_ref[...], x_ref[...], wpool_ref, pscale_ref,
              wout_ref, gpost_ref, gpre_ref, x1_ref, h2_ref)

    @pl.when(blk == blocks_per_seq - 1)
    def _():
        pool_ref[0] = ext_ref[pl.ds(first + bm - POOL_HIST, POOL_HIST), :]


def _mix_prompt(a2d, z2d, x2d, wpool_bf, pscale, wout_bf, gpost, gpre, *, bm, seq):
    m = x2d.shape[0]
    blocks_per_seq = seq // bm
    row_spec = lambda col: pl.BlockSpec((bm, D_MODEL), lambda i: (i, col))
    halo_blocks = bm // HALO
    return pl.pallas_call(
        functools.partial(_mix_prompt_kernel, bm=bm, blocks_per_seq=blocks_per_seq),
        grid=(m // bm,),
        in_specs=[
            row_spec(0),
            row_spec(OFF_GA // D_MODEL),
            row_spec(OFF_GP // D_MODEL),
            pl.BlockSpec((bm, D_POOL), lambda i: (i, OFF_U // D_POOL)),
            pl.BlockSpec((HALO, D_POOL),
                         lambda i: (jnp.maximum(i * halo_blocks - 1, 0), OFF_U // D_POOL)),
            row_spec(0),
            _const_spec((len(POOL_WINDOWS), POOL_GROUP, POOL_OUT_GROUP)),
            _const_spec((1, D_MODEL)),
            _const_spec((D_MODEL, D_MODEL)),
            _const_spec((1, D_MODEL)),
            _const_spec((1, D_MODEL)),
        ],
        out_specs=[row_spec(0), row_spec(0),
                   pl.BlockSpec((1, POOL_HIST, D_POOL), lambda i: (i // blocks_per_seq, 0, 0))],
        out_shape=[jax.ShapeDtypeStruct((m, D_MODEL), F32),
                   jax.ShapeDtypeStruct((m, D_MODEL), BF16),
                   jax.ShapeDtypeStruct((m // seq, POOL_HIST, D_POOL), F32)],
        scratch_shapes=[pltpu.VMEM((POOL_PAD + HALO + bm, D_POOL), F32),
                        pltpu.VMEM((POOL_PAD + HALO + bm, D_POOL), F32),
                        pltpu.VMEM((POOL_PAD + HALO + bm, D_POOL - POOL_GROUP), F32),
                        pltpu.VMEM((POOL_PAD + HALO + bm, D_POOL - 2 * POOL_GROUP), F32)],
        compiler_params=pltpu.CompilerParams(
            dimension_semantics=("arbitrary",), vmem_limit_bytes=VMEM_LIMIT),
        name="mix_prompt",
    )(a2d, z2d, z2d, z2d, z2d, x2d, wpool_bf, pscale, wout_bf, gpost, gpre)


def _mix_sample_kernel(a_ref, ga_ref, gp_ref, u_ref, hist_ref, x_ref, wpool_ref, pscale_ref,
                       wout_ref, gpost_ref, gpre_ref, x1_ref, h2_ref, pool_ref, ext_ref,
                       *, bt, t, pos0):
    ext_ref[:, HALO - POOL_HIST:HALO, :] = hist_ref[...]
    ext_ref[:, HALO:, :] = u_ref[...].reshape(bt, t, D_POOL)
    pooled = []
    for g, w in enumerate(POOL_WINDOWS):
        lanes = pl.ds(g * POOL_GROUP, POOL_GROUP)
        wsum = ext_ref[:, pl.ds(HALO, t), lanes]
        for s in range(1, w):
            wsum = wsum + ext_ref[:, pl.ds(HALO - s, t), lanes]
        assert pos0 + 1 >= w
        pooled.append(wsum.reshape(bt * t, POOL_GROUP) * (1.0 / w) - u_ref[:, lanes])
    _mix_tail(pooled, a_ref[...], ga_ref[...], gp_ref[...], x_ref[...], wpool_ref, pscale_ref,
              wout_ref, gpost_ref, gpre_ref, x1_ref, h2_ref)
    pool_ref[...] = ext_ref[:, pl.ds(HALO + t - POOL_HIST, POOL_HIST), :]


def _mix_sample(a2d, z2d, x2d, hist, wpool_bf, pscale, wout_bf, gpost, gpre, *, bt, t):
    m = x2d.shape[0]
    bm = bt * t
    row_spec = lambda col: pl.BlockSpec((bm, D_MODEL), lambda i: (i, col))
    return pl.pallas_call(
        functools.partial(_mix_sample_kernel, bt=bt, t=t, pos0=PAST_LEN),
        grid=(m // bm,),
        in_specs=[
            row_spec(0),
            row_spec(OFF_GA // D_MODEL),
            row_spec(OFF_GP // D_MODEL),
            pl.BlockSpec((bm, D_POOL), lambda i: (i, OFF_U // D_POOL)),
            pl.BlockSpec((bt, POOL_HIST, D_POOL), lambda i: (i, 0, 0)),
            row_spec(0),
            _const_spec((len(POOL_WINDOWS), POOL_GROUP, POOL_OUT_GROUP)),
            _const_spec((1, D_MODEL)),
            _const_spec((D_MODEL, D_MODEL)),
            _const_spec((1, D_MODEL)),
            _const_spec((1, D_MODEL)),
        ],
        out_specs=[row_spec(0), row_spec(0),
                   pl.BlockSpec((bt, POOL_HIST, D_POOL), lambda i: (i, 0, 0))],
        out_shape=[jax.ShapeDtypeStruct((m, D_MODEL), F32),
                   jax.ShapeDtypeStruct((m, D_MODEL), BF16),
                   jax.ShapeDtypeStruct((m // t, POOL_HIST, D_POOL), F32)],
        scratch_shapes=[pltpu.VMEM((bt, HALO + t, D_POOL), F32)],
        compiler_params=pltpu.CompilerParams(
            dimension_semantics=("parallel",), vmem_limit_bytes=VMEM_LIMIT),
        name="mix_sample",
    )(a2d, z2d, z2d, z2d, hist, x2d, wpool_bf, pscale, wout_bf, gpost, gpre)


MLP_TAIL_ROWS = 64


def _mlp_kernel(h2_ref, x1_slice_ref, wup_ref, wdn_ref, g_ref, y_ref, x1_ref, *, bm, slice_rows):
    j = pl.program_id(1)

    @pl.when(j == 0)
    def _():
        y_ref[...] = jnp.zeros_like(y_ref)

    x1_ref[pl.ds(pl.multiple_of(j * slice_rows, slice_rows), slice_rows), :] = x1_slice_ref[...]
    hid = jnp.dot(h2_ref[...], wup_ref[...], preferred_element_type=F32)
    hid = jnp.square(jnp.maximum(hid, 0.0)).astype(BF16)
    y_ref[...] += jnp.dot(hid, wdn_ref[...], preferred_element_type=F32)

    @pl.when(j == pl.num_programs(1) - 1)
    def _():
        def tail(c, carry):
            rows = pl.ds(pl.multiple_of(c * MLP_TAIL_ROWS, MLP_TAIL_ROWS), MLP_TAIL_ROWS)
            y_ref[rows, :] = x1_ref[rows, :] + _rmsnorm(y_ref[rows, :], g_ref[...])
            return carry

        lax.fori_loop(0, bm // MLP_TAIL_ROWS, tail, 0)


def _mlp_cast_kernel(h2_ref, x1_ref, wup_ref, wdn_ref, g_ref, y_ref, wup_bf_ref, wdn_bf_ref):
    j = pl.program_id(0)

    @pl.when(j == 0)
    def _():
        y_ref[...] = jnp.zeros_like(y_ref)

    wup = wup_ref[...].astype(BF16)
    wdn = wdn_ref[...].astype(BF16)
    wup_bf_ref[...] = wup
    wdn_bf_ref[...] = wdn
    hid = jnp.dot(h2_ref[...], wup, preferred_element_type=F32)
    hid = jnp.square(jnp.maximum(hid, 0.0)).astype(BF16)
    y_ref[...] += jnp.dot(hid, wdn, preferred_element_type=F32)

    @pl.when(j == pl.num_programs(0) - 1)
    def _():
        y_ref[...] = x1_ref[...] + _rmsnorm(y_ref[...], g_ref[...])


def _mlp_cast(h2, x1, wup_f32, wdn_f32, g, *, fc):
    m = x1.shape[0]
    return pl.pallas_call(
        _mlp_cast_kernel,
        grid=(D_FF // fc,),
        in_specs=[
            _const_spec((m, D_MODEL)),
            _const_spec((m, D_MODEL)),
            pl.BlockSpec((D_MODEL, fc), lambda j: (0, j)),
            pl.BlockSpec((fc, D_MODEL), lambda j: (j, 0)),
            _const_spec((1, D_MODEL)),
        ],
        out_specs=[pl.BlockSpec((m, D_MODEL), lambda j: (0, 0)),
                   pl.BlockSpec((D_MODEL, fc), lambda j: (0, j)),
                   pl.BlockSpec((fc, D_MODEL), lambda j: (j, 0))],
        out_shape=[jax.ShapeDtypeStruct((m, D_MODEL), F32),
                   jax.ShapeDtypeStruct((D_MODEL, D_FF), BF16),
                   jax.ShapeDtypeStruct((D_FF, D_MODEL), BF16)],
        compiler_params=pltpu.CompilerParams(
            dimension_semantics=("arbitrary",), vmem_limit_bytes=VMEM_LIMIT),
        name="mlp_cast",
    )(h2, x1, wup_f32, wdn_f32, g)


def _mlp(h2, x1, wup_bf, wdn_bf, g, *, bm, fc):
    m = x1.shape[0]
    n_ff = D_FF // fc
    slice_rows = bm // n_ff
    return pl.pallas_call(
        functools.partial(_mlp_kernel, bm=bm, slice_rows=slice_rows),
        grid=(m // bm, n_ff),
        in_specs=[
            pl.BlockSpec((bm, D_MODEL), lambda i, j: (i, 0)),
            pl.BlockSpec((slice_rows, D_MODEL), lambda i, j: (i * n_ff + j, 0)),
            pl.BlockSpec((D_MODEL, fc), lambda i, j: (0, j)),
            pl.BlockSpec((fc, D_MODEL), lambda i, j: (j, 0)),
            _const_spec((1, D_MODEL)),
        ],
        out_specs=pl.BlockSpec((bm, D_MODEL), lambda i, j: (i, 0)),
        out_shape=jax.ShapeDtypeStruct((m, D_MODEL), F32),
        scratch_shapes=[pltpu.VMEM((bm, D_MODEL), F32)],
        compiler_params=pltpu.CompilerParams(
            dimension_semantics=("parallel", "arbitrary"), vmem_limit_bytes=VMEM_LIMIT),
        name="mlp",
    )(h2, x1, wup_bf, wdn_bf, g)


def kernel(x_prompt, x_sample, cache_k_win, cache_v_win, state_pool, norm_attn_pre, norm_attn_post,
           w_in, attn_sinks, w_pool, pool_scale, w_out, norm_mlp_pre, norm_mlp_post, w_up, w_down):
    depth = w_in.shape[0]
    assert depth == 1
    b, s, _ = x_prompt.shape
    bs, t, _ = x_sample.shape

    l = 0
    w_pool_bf = w_pool[l].astype(BF16)
    w_out_bf = w_out[l].astype(BF16)
    row = lambda v: v[l].reshape(1, D_MODEL)
    g_attn_pre, g_attn_post = row(norm_attn_pre), row(norm_attn_post)
    g_mlp_pre, g_mlp_post = row(norm_mlp_pre), row(norm_mlp_post)
    pscale = row(pool_scale)
    sinks = attn_sinks[l]

    xs = x_sample.reshape(bs * t, D_MODEL)
    zs, w_in_bf = _in_proj_cast(xs, g_attn_pre, w_in[l])
    ck = cache_k_win[l].reshape(bs, WINDOW, D_KV)
    cv = cache_v_win[l].reshape(bs, WINDOW, D_KV)
    as_, ks, vs = _attn_sample(zs.reshape(bs, t, D_IN), ck, cv, sinks, bt=8)
    x1s, h2s, pools = _mix_sample(as_.reshape(bs * t, D_Q), zs, xs, state_pool[l], w_pool_bf, pscale,
                                  w_out_bf, g_attn_post, g_mlp_pre, bt=32, t=t)
    ys, w_up_bf, w_down_bf = _mlp_cast(h2s, x1s, w_up[l], w_down[l], g_mlp_post, fc=512)

    xp = x_prompt.reshape(b * s, D_MODEL)
    zp = _in_proj(xp, g_attn_pre, w_in_bf, bm=1024, bn=1536)
    ap, kp, vp = _attn_prompt(zp.reshape(b, s, D_IN), sinks)
    x1p, h2p, poolp = _mix_prompt(ap.reshape(b * s, D_Q), zp, xp, w_pool_bf, pscale, w_out_bf,
                                  g_attn_post, g_mlp_pre, bm=256, seq=s)
    yp = _mlp(h2p, x1p, w_up_bf, w_down_bf, g_mlp_post, bm=1024, fc=1024)

    kv_shape = lambda nb: (1, nb, WINDOW, N_KV_HEADS, HEAD_DIM)
    return (yp.reshape(b, s, D_MODEL), ys.reshape(bs, t, D_MODEL),
            kp.reshape(kv_shape(b)), vp.reshape(kv_shape(b)), poolp[None],
            ks.reshape(kv_shape(bs)), vs.reshape(kv_shape(bs)), pools[None])
```

```python
import functools

import jax
import jax.numpy as jnp
from jax import lax
from jax.experimental import pallas as pl
from jax.experimental.pallas import tpu as pltpu

F32 = jnp.float32
BF16 = jnp.bfloat16

D_MODEL = 2048
HEAD_DIM = 64
N_KV_HEADS = 4
GQA_GROUP = 8
N_Q_HEADS = N_KV_HEADS * GQA_GROUP
WINDOW = 128
D_Q = 2048
D_KV = 256
POOL_WINDOWS = (2, 4, 8, 16)
D_POOL = 1024
POOL_GROUP = 256
POOL_OUT_GROUP = 512
POOL_HIST = 15
D_FF = 8192
D_IN = D_Q + 2 * D_KV + D_POOL + 2 * D_MODEL
PAST_LEN = 8192
EPS = 1e-6
NEG_INF = -1e30
SCALE = HEAD_DIM ** -0.5
LOG2E = 1.4426950408889634

LANES = 128
PAIR = 2
N_PAIRS = GQA_GROUP // PAIR
ROW_CHUNK = 16

OFF_Q, OFF_GA, OFF_GP, OFF_U, OFF_K, OFF_V = 0, 2048, 4096, 6144, 7168, 7424
HALO = 16
POOL_PAD = 8

VMEM_LIMIT = 60 * 1024 * 1024

NT_DIMS = (((1,), (1,)), ((), ()))
TN_DIMS = (((0,), (0,)), ((), ()))


def _rmsnorm(x, g):
    r = lax.rsqrt(jnp.mean(x * x, axis=-1, keepdims=True) + EPS)
    return x * r * g


def _const_spec(shape):
    return pl.BlockSpec(shape, lambda *_: (0,) * len(shape), pipeline_mode=pl.Buffered(1))


def _in_proj_kernel(x_ref, g_ref, w_ref, z_ref, h_ref):
    @pl.when(pl.program_id(1) == 0)
    def _():
        h_ref[...] = _rmsnorm(x_ref[...], g_ref[...]).astype(BF16)

    z_ref[...] = jnp.dot(h_ref[...], w_ref[...], preferred_element_type=F32)


def _in_proj_cast_kernel(x_ref, g_ref, w_ref, wout_ref, z_ref, wbf_ref, wout_bf_ref, h_ref, *, n_out):
    j = pl.program_id(0)

    @pl.when(j == 0)
    def _():
        h_ref[...] = _rmsnorm(x_ref[...], g_ref[...]).astype(BF16)

    @pl.when(j < n_out)
    def _():
        wout_bf_ref[...] = wout_ref[...].astype(BF16)

    w = w_ref[...].astype(BF16)
    wbf_ref[...] = w
    z_ref[...] = jnp.dot(h_ref[...], w, preferred_element_type=F32)


def _in_proj_cast(x2d, g, w_f32, wout_f32):
    m = x2d.shape[0]
    bn = 512
    src_bounds = (D_Q, D_Q + 2 * D_KV, D_Q + 2 * D_KV + D_POOL, D_Q + 2 * D_KV + D_POOL + D_MODEL)
    assert all(off % bn == 0 for off in src_bounds + (OFF_GA, OFF_GP, OFF_U, OFF_K))
    n_out = D_MODEL // bn
    assert n_out <= D_IN // bn

    def reordered(j):
        return jnp.where(j < 4, j, jnp.where(j == 4, OFF_K // bn, jnp.where(j < 7, j + 7, j - 3)))

    wout_spec = pl.BlockSpec((D_MODEL, bn), lambda j: (0, jnp.minimum(j, n_out - 1)))
    return pl.pallas_call(
        functools.partial(_in_proj_cast_kernel, n_out=n_out),
        grid=(D_IN // bn,),
        in_specs=[
            _const_spec((m, D_MODEL)),
            _const_spec((1, D_MODEL)),
            pl.BlockSpec((D_MODEL, bn), lambda j: (0, j)),
            wout_spec,
        ],
        out_specs=[pl.BlockSpec((m, bn), lambda j: (0, reordered(j))),
                   pl.BlockSpec((D_MODEL, bn), lambda j: (0, reordered(j))),
                   wout_spec],
        out_shape=[jax.ShapeDtypeStruct((m, D_IN), F32),
                   jax.ShapeDtypeStruct((D_MODEL, D_IN), BF16),
                   jax.ShapeDtypeStruct((D_MODEL, D_MODEL), BF16)],
        scratch_shapes=[pltpu.VMEM((m, D_MODEL), BF16)],
        compiler_params=pltpu.CompilerParams(
            dimension_semantics=("arbitrary",), vmem_limit_bytes=VMEM_LIMIT),
        name="in_proj_cast",
    )(x2d, g, w_f32, wout_f32)


def _in_proj(x2d, g, w_bf, *, bm, bn):
    m = x2d.shape[0]
    return pl.pallas_call(
        _in_proj_kernel,
        grid=(m // bm, D_IN // bn),
        in_specs=[
            pl.BlockSpec((bm, D_MODEL), lambda i, j: (i, 0)),
            _const_spec((1, D_MODEL)),
            pl.BlockSpec((D_MODEL, bn), lambda i, j: (0, j)),
        ],
        out_specs=pl.BlockSpec((bm, bn), lambda i, j: (i, j)),
        out_shape=jax.ShapeDtypeStruct((m, D_IN), F32),
        scratch_shapes=[pltpu.VMEM((bm, D_MODEL), BF16)],
        compiler_params=pltpu.CompilerParams(
            dimension_semantics=("parallel", "arbitrary"), vmem_limit_bytes=VMEM_LIMIT),
        name="in_proj",
    )(x2d, g, w_bf)


def _own_and_swapped(slab, odd, lo):
    own = jnp.where(lo != odd, slab, 0.0)
    swapped = pltpu.roll(own, HEAD_DIM, axis=1)
    return (swapped, own) if odd else (own, swapped)


ATTN_BLOCKS = 8
ATTN_BUFFERS = 8


def _attn_prompt_kernel(sinks_ref, q_ref, kp_ref, kc_ref, vp_ref, vc_ref,
                        a_ref, kwin_ref, vwin_ref, s_ref, p_ref, e_ref):
    n = pl.program_id(1)
    k_rows = jnp.concatenate([kp_ref[0], kc_ref[0]], axis=0)
    v_rows = jnp.concatenate([vp_ref[0], vc_ref[0]], axis=0)
    first_prev_bias = jnp.where(n > 0, 0.0, NEG_INF).astype(F32)
    lo_kv = lax.broadcasted_iota(jnp.int32, (2 * WINDOW, LANES), 1) < HEAD_DIM
    ones_lo = jnp.where(lo_kv, 1.0, 0.0)
    ones_hi = 1.0 - ones_lo
    lo_c = lax.broadcasted_iota(jnp.int32, (ROW_CHUNK, LANES), 1) < HEAD_DIM
    lane_c = lax.broadcasted_iota(jnp.int32, (ROW_CHUNK, LANES), 1)
    row_c = lax.broadcasted_iota(jnp.int32, (ROW_CHUNK, LANES), 0)

    for blk in range(ATTN_BLOCKS):
        k_all = k_rows[blk * WINDOW:(blk + 2) * WINDOW]
        v_all = v_rows[blk * WINDOW:(blk + 2) * WINDOW]
        q_rows = pl.ds(blk * WINDOW, WINDOW)
        for h in range(N_KV_HEADS):
            buf = (blk * N_KV_HEADS + h) % ATTN_BUFFERS
            odd = (h % PAIR) == 1
            kv_col = slice((h // PAIR) * LANES, (h // PAIR + 1) * LANES)
            k_l, k_r = _own_and_swapped(k_all[:, kv_col], odd, lo_kv)
            v_l, v_r = _own_and_swapped(v_all[:, kv_col], odd, lo_kv)
            wk = jnp.concatenate([k_l, k_r], axis=0).astype(BF16)
            vext = jnp.concatenate(
                [jnp.concatenate([v_l, ones_lo], axis=1),
                 jnp.concatenate([v_r, ones_hi], axis=1)], axis=0).astype(BF16)
            qh = jnp.concatenate(
                [q_ref[0, q_rows, pl.ds((h * N_PAIRS + j) * LANES, LANES)] for j in range(N_PAIRS)],
                axis=0)
            qh = (qh * (SCALE * LOG2E)).astype(BF16)
            s_ref[buf] = lax.dot_general(qh, wk, NT_DIMS, preferred_element_type=F32)

            for j in range(N_PAIRS):
                sinks = [sinks_ref[h, PAIR * j] * LOG2E, sinks_ref[h, PAIR * j + 1] * LOG2E]
                for c in range(WINDOW // ROW_CHUNK):
                    rows = pl.ds(j * WINDOW + c * ROW_CHUNK, ROW_CHUNK)
                    mask = lane_c <= (row_c + c * ROW_CHUNK)
                    e_parts = []
                    for gi in range(PAIR):
                        base = gi * 2 * WINDOW
                        cur = s_ref[buf, rows, pl.ds(base + WINDOW, WINDOW)]
                        prev = s_ref[buf, rows, pl.ds(base, WINDOW)]
                        if blk == 0:
                            prev = prev + first_prev_bias
                        s = jnp.where(mask, cur, prev)
                        m = jnp.maximum(jnp.max(s, axis=-1, keepdims=True), sinks[gi])
                        p = jnp.exp2(s - m)
                        p_ref[buf, rows, pl.ds(base, WINDOW)] = jnp.where(mask, 0.0, p).astype(BF16)
                        p_ref[buf, rows, pl.ds(base + WINDOW, WINDOW)] = (
                            jnp.where(mask, p, 0.0).astype(BF16))
                        e_parts.append(jnp.broadcast_to(jnp.exp2(sinks[gi] - m), (ROW_CHUNK, LANES)))
                    e_ref[buf, rows, :] = jnp.where(lo_c, e_parts[0], e_parts[1])

            o_ext = jnp.dot(p_ref[buf], vext, preferred_element_type=F32)
            o = o_ext[:, :LANES] / (o_ext[:, LANES:] + e_ref[buf])
            for j in range(N_PAIRS):
                a_ref[0, q_rows, pl.ds((h * N_PAIRS + j) * LANES, LANES)] = o[j * WINDOW:(j + 1) * WINDOW]

    @pl.when(n == pl.num_programs(1) - 1)
    def _():
        kwin_ref[0] = kc_ref[0, pl.ds((ATTN_BLOCKS - 1) * WINDOW, WINDOW), :]
        vwin_ref[0] = vc_ref[0, pl.ds((ATTN_BLOCKS - 1) * WINDOW, WINDOW), :]


def _attn_prompt(z3, sinks):
    b, s, _ = z3.shape
    step_rows = ATTN_BLOCKS * WINDOW
    kcol, vcol = OFF_K // D_KV, OFF_V // D_KV
    prev = lambda col: (lambda bi, n: (bi, jnp.maximum(n * ATTN_BLOCKS - 1, 0), col))
    cur = lambda col: (lambda bi, n: (bi, n, col))
    win_spec = pl.BlockSpec((1, WINDOW, D_KV), lambda bi, n: (bi, 0, 0))
    rows = N_PAIRS * WINDOW
    n_buf = ATTN_BUFFERS
    return pl.pallas_call(
        _attn_prompt_kernel,
        grid=(b, s // step_rows),
        in_specs=[
            pl.BlockSpec(memory_space=pltpu.SMEM),
            pl.BlockSpec((1, step_rows, D_Q), lambda bi, n: (bi, n, 0)),
            pl.BlockSpec((1, WINDOW, D_KV), prev(kcol)),
            pl.BlockSpec((1, step_rows, D_KV), cur(kcol)),
            pl.BlockSpec((1, WINDOW, D_KV), prev(vcol)),
            pl.BlockSpec((1, step_rows, D_KV), cur(vcol)),
        ],
        out_specs=[pl.BlockSpec((1, step_rows, D_Q), lambda bi, n: (bi, n, 0)), win_spec, win_spec],
        out_shape=[jax.ShapeDtypeStruct((b, s, D_Q), F32),
                   jax.ShapeDtypeStruct((b, WINDOW, D_KV), F32),
                   jax.ShapeDtypeStruct((b, WINDOW, D_KV), F32)],
        scratch_shapes=[pltpu.VMEM((n_buf, rows, PAIR * 2 * WINDOW), F32),
                        pltpu.VMEM((n_buf, rows, PAIR * 2 * WINDOW), BF16),
                        pltpu.VMEM((n_buf, rows, LANES), F32)],
        compiler_params=pltpu.CompilerParams(
            dimension_semantics=("parallel", "arbitrary"), vmem_limit_bytes=VMEM_LIMIT),
        name="attn_prompt",
    )(sinks, z3, z3, z3, z3, z3)


def _attn_sample_kernel(sinkrow_ref, q_ref, kn_ref, vn_ref, ck_ref, cv_ref,
                        a_ref, kwin_ref, vwin_ref, *, bt, t):
    pad_rows = 2 * WINDOW - WINDOW - t
    pad = jnp.zeros((pad_rows, D_KV), F32)
    p_pad = jnp.zeros((pad_rows, N_Q_HEADS * t), F32)
    zeros_col = jnp.zeros((t, LANES), F32)
    lo = lax.broadcasted_iota(jnp.int32, (t, LANES), 1) < HEAD_DIM
    tok = lax.broadcasted_iota(jnp.int32, (t, N_Q_HEADS * t), 1) % t
    key = lax.broadcasted_iota(jnp.int32, (t, N_Q_HEADS * t), 0)
    new_mask = key <= tok
    sinkrow = sinkrow_ref[...]

    def place(piece_col, src_odd, dst_odd):
        own = jnp.where(lo != src_odd, piece_col, 0.0)
        return own if src_odd == dst_odd else pltpu.roll(own, HEAD_DIM, axis=1)

    def body(s, carry):
        kn, vn, ck, cv = kn_ref[s], vn_ref[s], ck_ref[s], cv_ref[s]
        k_all = jnp.concatenate([ck, kn, pad], axis=0).astype(BF16)
        v_all = jnp.concatenate([cv, vn, pad], axis=0).astype(BF16)
        q = q_ref[s] * SCALE
        blocks = []
        for h in range(N_KV_HEADS):
            for g in range(GQA_GROUP):
                head = h * GQA_GROUP + g
                piece = place(q[:, (head // PAIR) * LANES:(head // PAIR + 1) * LANES],
                              head % PAIR == 1, h % PAIR == 1)
                cols = [zeros_col] * (D_KV // LANES)
                cols[h // PAIR] = piece
                blocks.append(jnp.concatenate(cols, axis=1))
        wq_t = jnp.concatenate(blocks, axis=0).astype(BF16)
        s_t = lax.dot_general(k_all, wq_t, NT_DIMS, preferred_element_type=F32)
        top = jnp.where(new_mask, s_t[WINDOW:WINDOW + t], s_t[0:t])
        s_m = jnp.concatenate([top, s_t[t:WINDOW]], axis=0)
        m = jnp.maximum(jnp.max(s_m, axis=0, keepdims=True), sinkrow)
        p = jnp.exp(s_m - m)
        denom = jnp.sum(p, axis=0, keepdims=True) + jnp.exp(sinkrow - m)
        probs = p * (1.0 / denom)
        p_all = jnp.concatenate(
            [jnp.where(new_mask, 0.0, probs[0:t]), probs[t:WINDOW],
             jnp.where(new_mask, probs[0:t], 0.0), p_pad], axis=0).astype(BF16)
        o_full = lax.dot_general(p_all, v_all, TN_DIMS, preferred_element_type=F32)
        out_cols = []
        for c in range(N_Q_HEADS // PAIR):
            h = (PAIR * c) // GQA_GROUP
            kv_col = slice((h // PAIR) * LANES, (h // PAIR + 1) * LANES)
            even = place(o_full[(PAIR * c) * t:(PAIR * c + 1) * t, kv_col], h % PAIR == 1, False)
            odd = place(o_full[(PAIR * c + 1) * t:(PAIR * c + 2) * t, kv_col], h % PAIR == 1, True)
            out_cols.append(even + odd)
        a_ref[s] = jnp.concatenate(out_cols, axis=1)
        kwin_ref[s, 0:WINDOW - t, :] = ck[t:, :]
        kwin_ref[s, WINDOW - t:WINDOW, :] = kn
        vwin_ref[s, 0:WINDOW - t, :] = cv[t:, :]
        vwin_ref[s, WINDOW - t:WINDOW, :] = vn
        return carry

    lax.fori_loop(0, bt, body, 0, unroll=8)


def _attn_sample(z3, cache_k, cache_v, sinks, *, bt):
    b, t, _ = z3.shape
    kcol, vcol = OFF_K // D_KV, OFF_V // D_KV
    cache_spec = pl.BlockSpec((bt, WINDOW, D_KV), lambda i: (i, 0, 0))
    sinkrow = jnp.repeat(sinks.reshape(1, N_Q_HEADS), t, axis=1)
    return pl.pallas_call(
        functools.partial(_attn_sample_kernel, bt=bt, t=t),
        grid=(b // bt,),
        in_specs=[
            _const_spec((1, N_Q_HEADS * t)),
            pl.BlockSpec((bt, t, D_Q), lambda i: (i, 0, 0)),
            pl.BlockSpec((bt, t, D_KV), lambda i: (i, 0, kcol)),
            pl.BlockSpec((bt, t, D_KV), lambda i: (i, 0, vcol)),
            cache_spec, cache_spec,
        ],
        out_specs=[pl.BlockSpec((bt, t, D_Q), lambda i: (i, 0, 0)), cache_spec, cache_spec],
        out_shape=[jax.ShapeDtypeStruct((b, t, D_Q), F32),
                   jax.ShapeDtypeStruct((b, WINDOW, D_KV), F32),
                   jax.ShapeDtypeStruct((b, WINDOW, D_KV), F32)],
        compiler_params=pltpu.CompilerParams(
            dimension_semantics=("parallel",), vmem_limit_bytes=VMEM_LIMIT),
        name="attn_sample",
    )(sinkrow, z3, z3, z3, cache_k, cache_v)


def _mix_tail(pooled_parts, a, ga, gp, x, wpool_ref, pscale_ref, wout_ref, gpost_ref, gpre_ref,
              x1_ref, h2_ref):
    parts = [jnp.dot(pooled_parts[g].astype(BF16), wpool_ref[g], preferred_element_type=F32)
             for g in range(len(POOL_WINDOWS))]
    p = jnp.concatenate(parts, axis=-1) * pscale_ref[...]
    mixed = jax.nn.sigmoid(ga) * a + jax.nn.sigmoid(gp) * p
    y = jnp.dot(mixed.astype(BF16), wout_ref[...], preferred_element_type=F32)
    x1 = x + _rmsnorm(y, gpost_ref[...])
    x1_ref[...] = x1
    h2_ref[...] = _rmsnorm(x1, gpre_ref[...]).astype(BF16)


def _mix_prompt_kernel(a_ref, ga_ref, gp_ref, u_ref, halo_ref, x_ref, wpool_ref, pscale_ref,
                       wout_ref, gpost_ref, gpre_ref, x1_ref, h2_ref, pool_ref,
                       ext_ref, sum2_ref, sum4_ref, sum8_ref, *, bm, blocks_per_seq):
    assert POOL_WINDOWS == (2, 4, 8, 16)
    blk = pl.program_id(0) % blocks_per_seq
    pad = jnp.zeros((POOL_PAD, D_POOL), F32)
    ext_ref[0:POOL_PAD, :] = pad
    sum2_ref[0:POOL_PAD, :] = pad
    sum4_ref[0:POOL_PAD, :] = pad[:, POOL_GROUP:]
    ext_ref[POOL_PAD:POOL_PAD + HALO, :] = jnp.where(blk == 0, 0.0, halo_ref[...])
    ext_ref[POOL_PAD + HALO:, :] = u_ref[...]
    n = HALO + bm
    body = pl.ds(POOL_PAD, n)
    shifted = lambda s: pl.ds(POOL_PAD - s, n)
    g1 = pl.ds(POOL_GROUP, D_POOL - POOL_GROUP)
    sum2_ref[body, :] = ext_ref[body, :] + ext_ref[shifted(1), :]
    sum4_ref[body, :] = sum2_ref[body, g1] + sum2_ref[shifted(2), g1]
    sum8_ref[body, :] = sum4_ref[body, POOL_GROUP:] + sum4_ref[shifted(4), POOL_GROUP:]
    first = POOL_PAD + HALO
    rows = pl.ds(first, bm)
    sum16 = sum8_ref[rows, POOL_GROUP:] + sum8_ref[pl.ds(first - 8, bm), POOL_GROUP:]
    wsums = [sum2_ref[rows, 0:POOL_GROUP], sum4_ref[rows, 0:POOL_GROUP],
             sum8_ref[rows, 0:POOL_GROUP], sum16]
    pos = blk * bm + lax.broadcasted_iota(jnp.int32, (bm, 1), 0)
    pooled = []
    for g, w in enumerate(POOL_WINDOWS):
        inv_cnt = 1.0 / jnp.minimum(pos + 1, w).astype(F32)
        pooled.append(wsums[g] * inv_cnt - u_ref[:, pl.ds(g * POOL_GROUP, POOL_GROUP)])
    _mix_tail(pooled, a_ref[...], ga_ref[...], gp_ref[...], x_ref[...], wpool_ref, pscale_ref,
              wout_ref, gpost_ref, gpre_ref, x1_ref, h2_ref)

    @pl.when(blk == blocks_per_seq - 1)
    def _():
        pool_ref[0] = ext_ref[pl.ds(first + bm - POOL_HIST, POOL_HIST), :]


def _mix_prompt(a2d, z2d, x2d, wpool_bf, pscale, wout_bf, gpost, gpre, *, bm, seq):
    m = x2d.shape[0]
    blocks_per_seq = seq // bm
    row_spec = lambda col: pl.BlockSpec((bm, D_MODEL), lambda i: (i, col))
    halo_blocks = bm // HALO
    return pl.pallas_call(
        functools.partial(_mix_prompt_kernel, bm=bm, blocks_per_seq=blocks_per_seq),
        grid=(m // bm,),
        in_specs=[
            row_spec(0),
            row_spec(OFF_GA // D_MODEL),
            row_spec(OFF_GP // D_MODEL),
            pl.BlockSpec((bm, D_POOL), lambda i: (i, OFF_U // D_POOL)),
            pl.BlockSpec((HALO, D_POOL),
                         lambda i: (jnp.maximum(i * halo_blocks - 1, 0), OFF_U // D_POOL)),
            row_spec(0),
            _const_spec((len(POOL_WINDOWS), POOL_GROUP, POOL_OUT_GROUP)),
            _const_spec((1, D_MODEL)),
            _const_spec((D_MODEL, D_MODEL)),
            _const_spec((1, D_MODEL)),
            _const_spec((1, D_MODEL)),
        ],
        out_specs=[row_spec(0), row_spec(0),
                   pl.BlockSpec((1, POOL_HIST, D_POOL), lambda i: (i // blocks_per_seq, 0, 0))],
        out_shape=[jax.ShapeDtypeStruct((m, D_MODEL), F32),
                   jax.ShapeDtypeStruct((m, D_MODEL), BF16),
                   jax.ShapeDtypeStruct((m // seq, POOL_HIST, D_POOL), F32)],
        scratch_shapes=[pltpu.VMEM((POOL_PAD + HALO + bm, D_POOL), F32),
                        pltpu.VMEM((POOL_PAD + HALO + bm, D_POOL), F32),
                        pltpu.VMEM((POOL_PAD + HALO + bm, D_POOL - POOL_GROUP), F32),
                        pltpu.VMEM((POOL_PAD + HALO + bm, D_POOL - 2 * POOL_GROUP), F32)],
        compiler_params=pltpu.CompilerParams(
            dimension_semantics=("arbitrary",), vmem_limit_bytes=VMEM_LIMIT),
        name="mix_prompt",
    )(a2d, z2d, z2d, z2d, z2d, x2d, wpool_bf, pscale, wout_bf, gpost, gpre)


def _mix_sample_kernel(a_ref, ga_ref, gp_ref, u_ref, hist_ref, x_ref, wpool_ref, pscale_ref,
                       wout_ref, gpost_ref, gpre_ref, x1_ref, h2_ref, pool_ref, ext_ref,
                       *, bt, t, pos0):
    ext_ref[:, HALO - POOL_HIST:HALO, :] = hist_ref[...]
    ext_ref[:, HALO:, :] = u_ref[...].reshape(bt, t, D_POOL)
    pooled = []
    for g, w in enumerate(POOL_WINDOWS):
        lanes = pl.ds(g * POOL_GROUP, POOL_GROUP)
        wsum = ext_ref[:, pl.ds(HALO, t), lanes]
        for s in range(1, w):
            wsum = wsum + ext_ref[:, pl.ds(HALO - s, t), lanes]
        assert pos0 + 1 >= w
        pooled.append(wsum.reshape(bt * t, POOL_GROUP) * (1.0 / w) - u_ref[:, lanes])
    _mix_tail(pooled, a_ref[...], ga_ref[...], gp_ref[...], x_ref[...], wpool_ref, pscale_ref,
              wout_ref, gpost_ref, gpre_ref, x1_ref, h2_ref)
    pool_ref[...] = ext_ref[:, pl.ds(HALO + t - POOL_HIST, POOL_HIST), :]


def _mix_sample(a2d, z2d, x2d, hist, wpool_bf, pscale, wout_bf, gpost, gpre, *, bt, t):
    m = x2d.shape[0]
    bm = bt * t
    row_spec = lambda col: pl.BlockSpec((bm, D_MODEL), lambda i: (i, col))
    return pl.pallas_call(
        functools.partial(_mix_sample_kernel, bt=bt, t=t, pos0=PAST_LEN),
        grid=(m // bm,),
        in_specs=[
            row_spec(0),
            row_spec(OFF_GA // D_MODEL),
            row_spec(OFF_GP // D_MODEL),
            pl.BlockSpec((bm, D_POOL), lambda i: (i, OFF_U // D_POOL)),
            pl.BlockSpec((bt, POOL_HIST, D_POOL), lambda i: (i, 0, 0)),
            row_spec(0),
            _const_spec((len(POOL_WINDOWS), POOL_GROUP, POOL_OUT_GROUP)),
            _const_spec((1, D_MODEL)),
            _const_spec((D_MODEL, D_MODEL)),
            _const_spec((1, D_MODEL)),
            _const_spec((1, D_MODEL)),
        ],
        out_specs=[row_spec(0), row_spec(0),
                   pl.BlockSpec((bt, POOL_HIST, D_POOL), lambda i: (i, 0, 0))],
        out_shape=[jax.ShapeDtypeStruct((m, D_MODEL), F32),
                   jax.ShapeDtypeStruct((m, D_MODEL), BF16),
                   jax.ShapeDtypeStruct((m // t, POOL_HIST, D_POOL), F32)],
        scratch_shapes=[pltpu.VMEM((bt, HALO + t, D_POOL), F32)],
        compiler_params=pltpu.CompilerParams(
            dimension_semantics=("parallel",), vmem_limit_bytes=VMEM_LIMIT),
        name="mix_sample",
    )(a2d, z2d, z2d, z2d, hist, x2d, wpool_bf, pscale, wout_bf, gpost, gpre)


MLP_TAIL_ROWS = 64


def _mlp_kernel(h2_ref, x1_slice_ref, wup_ref, wdn_ref, g_ref, y_ref, x1_ref, *, bm, slice_rows):
    j = pl.program_id(1)

    @pl.when(j == 0)
    def _():
        y_ref[...] = jnp.zeros_like(y_ref)

    x1_ref[pl.ds(pl.multiple_of(j * slice_rows, slice_rows), slice_rows), :] = x1_slice_ref[...]
    hid = jnp.dot(h2_ref[...], wup_ref[...], preferred_element_type=F32)
    hid = jnp.square(jnp.maximum(hid, 0.0)).astype(BF16)
    y_ref[...] += jnp.dot(hid, wdn_ref[...], preferred_element_type=F32)

    @pl.when(j == pl.num_programs(1) - 1)
    def _():
        def tail(c, carry):
            rows = pl.ds(pl.multiple_of(c * MLP_TAIL_ROWS, MLP_TAIL_ROWS), MLP_TAIL_ROWS)
            y_ref[rows, :] = x1_ref[rows, :] + _rmsnorm(y_ref[rows, :], g_ref[...])
            return carry

        lax.fori_loop(0, bm // MLP_TAIL_ROWS, tail, 0)


def _mlp_cast_kernel(h2_ref, x1_ref, wup_ref, wdn_ref, g_ref, y_ref, wup_bf_ref, wdn_bf_ref):
    j = pl.program_id(0)

    @pl.when(j == 0)
    def _():
        y_ref[...] = jnp.zeros_like(y_ref)

    wup = wup_ref[...].astype(BF16)
    wdn = wdn_ref[...].astype(BF16)
    wup_bf_ref[...] = wup
    wdn_bf_ref[...] = wdn
    hid = jnp.dot(h2_ref[...], wup, preferred_element_type=F32)
    hid = jnp.square(jnp.maximum(hid, 0.0)).astype(BF16)
    y_ref[...] += jnp.dot(hid, wdn, preferred_element_type=F32)

    @pl.when(j == pl.num_programs(0) - 1)
    def _():
        y_ref[...] = x1_ref[...] + _rmsnorm(y_ref[...], g_ref[...])


def _mlp_cast(h2, x1, wup_f32, wdn_f32, g, *, fc):
    m = x1.shape[0]
    return pl.pallas_call(
        _mlp_cast_kernel,
        grid=(D_FF // fc,),
        in_specs=[
            _const_spec((m, D_MODEL)),
            _const_spec((m, D_MODEL)),
            pl.BlockSpec((D_MODEL, fc), lambda j: (0, j)),
            pl.BlockSpec((fc, D_MODEL), lambda j: (j, 0)),
            _const_spec((1, D_MODEL)),
        ],
        out_specs=[pl.BlockSpec((m, D_MODEL), lambda j: (0, 0)),
                   pl.BlockSpec((D_MODEL, fc), lambda j: (0, j)),
                   pl.BlockSpec((fc, D_MODEL), lambda j: (j, 0))],
        out_shape=[jax.ShapeDtypeStruct((m, D_MODEL), F32),
                   jax.ShapeDtypeStruct((D_MODEL, D_FF), BF16),
                   jax.ShapeDtypeStruct((D_FF, D_MODEL), BF16)],
        compiler_params=pltpu.CompilerParams(
            dimension_semantics=("arbitrary",), vmem_limit_bytes=VMEM_LIMIT),
        name="mlp_cast",
    )(h2, x1, wup_f32, wdn_f32, g)


def _mlp(h2, x1, wup_bf, wdn_bf, g, *, bm, fc):
    m = x1.shape[0]
    n_ff = D_FF // fc
    slice_rows = bm // n_ff
    return pl.pallas_call(
        functools.partial(_mlp_kernel, bm=bm, slice_rows=slice_rows),
        grid=(m // bm, n_ff),
        in_specs=[
            pl.BlockSpec((bm, D_MODEL), lambda i, j: (i, 0)),
            pl.BlockSpec((slice_rows, D_MODEL), lambda i, j: (i * n_ff + j, 0)),
            pl.BlockSpec((D_MODEL, fc), lambda i, j: (0, j)),
            pl.BlockSpec((fc, D_MODEL), lambda i, j: (j, 0)),
            _const_spec((1, D_MODEL)),
        ],
        out_specs=pl.BlockSpec((bm, D_MODEL), lambda i, j: (i, 0)),
        out_shape=jax.ShapeDtypeStruct((m, D_MODEL), F32),
        scratch_shapes=[pltpu.VMEM((bm, D_MODEL), F32)],
        compiler_params=pltpu.CompilerParams(
            dimension_semantics=("parallel", "arbitrary"), vmem_limit_bytes=VMEM_LIMIT),
        name="mlp",
    )(h2, x1, wup_bf, wdn_bf, g)


def kernel(x_prompt, x_sample, cache_k_win, cache_v_win, state_pool, norm_attn_pre, norm_attn_post,
           w_in, attn_sinks, w_pool, pool_scale, w_out, norm_mlp_pre, norm_mlp_post, w_up, w_down):
    depth = w_in.shape[0]
    assert depth == 1
    b, s, _ = x_prompt.shape
    bs, t, _ = x_sample.shape

    l = 0
    w_pool_bf = w_pool[l].astype(BF16)
    row = lambda v: v[l].reshape(1, D_MODEL)
    g_attn_pre, g_attn_post = row(norm_attn_pre), row(norm_attn_post)
    g_mlp_pre, g_mlp_post = row(norm_mlp_pre), row(norm_mlp_post)
    pscale = row(pool_scale)
    sinks = attn_sinks[l]

    xs = x_sample.reshape(bs * t, D_MODEL)
    zs, w_in_bf, w_out_bf = _in_proj_cast(xs, g_attn_pre, w_in[l], w_out[l])
    ck = cache_k_win[l].reshape(bs, WINDOW, D_KV)
    cv = cache_v_win[l].reshape(bs, WINDOW, D_KV)
    as_, ks, vs = _attn_sample(zs.reshape(bs, t, D_IN), ck, cv, sinks, bt=8)
    x1s, h2s, pools = _mix_sample(as_.reshape(bs * t, D_Q), zs, xs, state_pool[l], w_pool_bf, pscale,
                                  w_out_bf, g_attn_post, g_mlp_pre, bt=32, t=t)
    ys, w_up_bf, w_down_bf = _mlp_cast(h2s, x1s, w_up[l], w_down[l], g_mlp_post, fc=512)

    xp = x_prompt.reshape(b * s, D_MODEL)
    zp = _in_proj(xp, g_attn_pre, w_in_bf, bm=1024, bn=1536)
    ap, kp, vp = _attn_prompt(zp.reshape(b, s, D_IN), sinks)
    x1p, h2p, poolp = _mix_prompt(ap.reshape(b * s, D_Q), zp, xp, w_pool_bf, pscale, w_out_bf,
                                  g_attn_post, g_mlp_pre, bm=256, seq=s)
    yp = _mlp(h2p, x1p, w_up_bf, w_down_bf, g_mlp_post, bm=1024, fc=1024)

    kv_shape = lambda nb: (1, nb, WINDOW, N_KV_HEADS, HEAD_DIM)
    return (yp.reshape(b, s, D_MODEL), ys.reshape(bs, t, D_MODEL),
            kp.reshape(kv_shape(b)), vp.reshape(kv_shape(b)), poolp[None],
            ks.reshape(kv_shape(bs)), vs.reshape(kv_shape(bs)), pools[None])
```

```python
import functools

import jax
import jax.numpy as jnp
from jax import lax
from jax.experimental import pallas as pl
from jax.experimental.pallas import tpu as pltpu

F32 = jnp.float32
BF16 = jnp.bfloat16

D_MODEL = 2048
HEAD_DIM = 64
N_KV_HEADS = 4
GQA_GROUP = 8
N_Q_HEADS = N_KV_HEADS * GQA_GROUP
WINDOW = 128
D_Q = 2048
D_KV = 256
POOL_WINDOWS = (2, 4, 8, 16)
D_POOL = 1024
POOL_GROUP = 256
POOL_OUT_GROUP = 512
POOL_HIST = 15
D_FF = 8192
D_IN = D_Q + 2 * D_KV + D_POOL + 2 * D_MODEL
PAST_LEN = 8192
EPS = 1e-6
NEG_INF = -1e30
SCALE = HEAD_DIM ** -0.5
LOG2E = 1.4426950408889634

LANES = 128
PAIR = 2
N_PAIRS = GQA_GROUP // PAIR
ROW_CHUNK = 16

OFF_Q, OFF_GA, OFF_GP, OFF_U, OFF_K, OFF_V = 0, 2048, 4096, 6144, 7168, 7424
HALO = 16
POOL_PAD = 8

VMEM_LIMIT = 60 * 1024 * 1024
MXU_TILE = 256


class Tiles:
    IN_PROJ_ROWS, IN_PROJ_COLS = 1024, 6 * MXU_TILE
    IN_PROJ_CAST_COLS = 2 * MXU_TILE
    MIX_ROWS = 256
    MLP_ROWS, MLP_FF_COLS = 1024, 4 * MXU_TILE
    MLP_CAST_FF_COLS = 2 * MXU_TILE
    ATTN_SAMPLE_SEQS = 16
    ATTN_SAMPLE_UNROLL = 8
    MIX_SAMPLE_SEQS = 32

NT_DIMS = (((1,), (1,)), ((), ()))
TN_DIMS = (((0,), (0,)), ((), ()))


def _rmsnorm(x, g):
    r = lax.rsqrt(jnp.mean(x * x, axis=-1, keepdims=True) + EPS)
    return x * r * g


def _const_spec(shape):
    return pl.BlockSpec(shape, lambda *_: (0,) * len(shape), pipeline_mode=pl.Buffered(1))


def _in_proj_kernel(x_ref, g_ref, w_ref, z_ref, h_ref):
    @pl.when(pl.program_id(1) == 0)
    def _():
        h_ref[...] = _rmsnorm(x_ref[...], g_ref[...]).astype(BF16)

    z_ref[...] = jnp.dot(h_ref[...], w_ref[...], preferred_element_type=F32)


def _in_proj_cast_kernel(x_ref, g_ref, w_ref, z_ref, wbf_ref, h_ref):
    @pl.when(pl.program_id(0) == 0)
    def _():
        h_ref[...] = _rmsnorm(x_ref[...], g_ref[...]).astype(BF16)

    w = w_ref[...].astype(BF16)
    wbf_ref[...] = w
    z_ref[...] = jnp.dot(h_ref[...], w, preferred_element_type=F32)


def _in_proj_cast(x2d, g, w_f32):
    m = x2d.shape[0]
    bn = Tiles.IN_PROJ_CAST_COLS
    src_bounds = (D_Q, D_Q + 2 * D_KV, D_Q + 2 * D_KV + D_POOL, D_Q + 2 * D_KV + D_POOL + D_MODEL)
    assert all(off % bn == 0 for off in src_bounds + (OFF_GA, OFF_GP, OFF_U, OFF_K))

    def reordered(j):
        q_end, kv_end, u_end, ga_end = (off // bn for off in src_bounds)
        return jnp.where(
            j < q_end, j, jnp.where(
                j < kv_end, OFF_K // bn + (j - q_end), jnp.where(
                    j < u_end, OFF_U // bn + (j - kv_end), jnp.where(
                        j < ga_end, OFF_GA // bn + (j - u_end), OFF_GP // bn + (j - ga_end)))))

    return pl.pallas_call(
        _in_proj_cast_kernel,
        grid=(D_IN // bn,),
        in_specs=[
            _const_spec((m, D_MODEL)),
            _const_spec((1, D_MODEL)),
            pl.BlockSpec((D_MODEL, bn), lambda j: (0, j)),
        ],
        out_specs=[pl.BlockSpec((m, bn), lambda j: (0, reordered(j))),
                   pl.BlockSpec((D_MODEL, bn), lambda j: (0, reordered(j)))],
        out_shape=[jax.ShapeDtypeStruct((m, D_IN), F32),
                   jax.ShapeDtypeStruct((D_MODEL, D_IN), BF16)],
        scratch_shapes=[pltpu.VMEM((m, D_MODEL), BF16)],
        compiler_params=pltpu.CompilerParams(
            dimension_semantics=("arbitrary",), vmem_limit_bytes=VMEM_LIMIT),
        name="in_proj_cast",
    )(x2d, g, w_f32)


def _in_proj(x2d, g, w_bf, *, bm, bn):
    m = x2d.shape[0]
    return pl.pallas_call(
        _in_proj_kernel,
        grid=(m // bm, D_IN // bn),
        in_specs=[
            pl.BlockSpec((bm, D_MODEL), lambda i, j: (i, 0)),
            _const_spec((1, D_MODEL)),
            pl.BlockSpec((D_MODEL, bn), lambda i, j: (0, j)),
        ],
        out_specs=pl.BlockSpec((bm, bn), lambda i, j: (i, j)),
        out_shape=jax.ShapeDtypeStruct((m, D_IN), F32),
        scratch_shapes=[pltpu.VMEM((bm, D_MODEL), BF16)],
        compiler_params=pltpu.CompilerParams(
            dimension_semantics=("parallel", "arbitrary"), vmem_limit_bytes=VMEM_LIMIT),
        name="in_proj",
    )(x2d, g, w_bf)


def _own_and_swapped(slab, odd, lo):
    own = jnp.where(lo != odd, slab, 0.0)
    swapped = pltpu.roll(own, HEAD_DIM, axis=1)
    return (swapped, own) if odd else (own, swapped)


ATTN_BLOCKS = 8
ATTN_BUFFERS = 8


def _attn_prompt_kernel(sinks_ref, q_ref, kp_ref, kc_ref, vp_ref, vc_ref,
                        a_ref, kwin_ref, vwin_ref, s_ref, p_ref, e_ref):
    n = pl.program_id(1)
    k_rows = jnp.concatenate([kp_ref[0], kc_ref[0]], axis=0)
    v_rows = jnp.concatenate([vp_ref[0], vc_ref[0]], axis=0)
    first_prev_bias = jnp.where(n > 0, 0.0, NEG_INF).astype(F32)
    lo_kv = lax.broadcasted_iota(jnp.int32, (2 * WINDOW, LANES), 1) < HEAD_DIM
    ones_lo = jnp.where(lo_kv, 1.0, 0.0)
    ones_hi = 1.0 - ones_lo
    lo_c = lax.broadcasted_iota(jnp.int32, (ROW_CHUNK, LANES), 1) < HEAD_DIM
    lane_c = lax.broadcasted_iota(jnp.int32, (ROW_CHUNK, LANES), 1)
    row_c = lax.broadcasted_iota(jnp.int32, (ROW_CHUNK, LANES), 0)

    for blk in range(ATTN_BLOCKS):
        k_all = k_rows[blk * WINDOW:(blk + 2) * WINDOW]
        v_all = v_rows[blk * WINDOW:(blk + 2) * WINDOW]
        q_rows = pl.ds(blk * WINDOW, WINDOW)
        for h in range(N_KV_HEADS):
            buf = (blk * N_KV_HEADS + h) % ATTN_BUFFERS
            odd = (h % PAIR) == 1
            kv_col = slice((h // PAIR) * LANES, (h // PAIR + 1) * LANES)
            k_l, k_r = _own_and_swapped(k_all[:, kv_col], odd, lo_kv)
            v_l, v_r = _own_and_swapped(v_all[:, kv_col], odd, lo_kv)
            wk = jnp.concatenate([k_l, k_r], axis=0).astype(BF16)
            vext = jnp.concatenate(
                [jnp.concatenate([v_l, ones_lo], axis=1),
                 jnp.concatenate([v_r, ones_hi], axis=1)], axis=0).astype(BF16)
            qh = jnp.concatenate(
                [q_ref[0, q_rows, pl.ds((h * N_PAIRS + j) * LANES, LANES)] for j in range(N_PAIRS)],
                axis=0)
            qh = (qh * (SCALE * LOG2E)).astype(BF16)
            s_ref[buf] = lax.dot_general(qh, wk, NT_DIMS, preferred_element_type=F32)

            for j in range(N_PAIRS):
                sinks = [sinks_ref[h, PAIR * j] * LOG2E, sinks_ref[h, PAIR * j + 1] * LOG2E]
                for c in range(WINDOW // ROW_CHUNK):
                    rows = pl.ds(j * WINDOW + c * ROW_CHUNK, ROW_CHUNK)
                    mask = lane_c <= (row_c + c * ROW_CHUNK)
                    e_parts = []
                    for gi in range(PAIR):
                        base = gi * 2 * WINDOW
                        cur = s_ref[buf, rows, pl.ds(base + WINDOW, WINDOW)]
                        prev = s_ref[buf, rows, pl.ds(base, WINDOW)]
                        if blk == 0:
                            prev = prev + first_prev_bias
                        s = jnp.where(mask, cur, prev)
                        m = jnp.maximum(jnp.max(s, axis=-1, keepdims=True), sinks[gi])
                        p = jnp.exp2(s - m)
                        p_ref[buf, rows, pl.ds(base, WINDOW)] = jnp.where(mask, 0.0, p).astype(BF16)
                        p_ref[buf, rows, pl.ds(base + WINDOW, WINDOW)] = (
                            jnp.where(mask, p, 0.0).astype(BF16))
                        e_parts.append(jnp.broadcast_to(jnp.exp2(sinks[gi] - m), (ROW_CHUNK, LANES)))
                    e_ref[buf, rows, :] = jnp.where(lo_c, e_parts[0], e_parts[1])

            o_ext = jnp.dot(p_ref[buf], vext, preferred_element_type=F32)
            o = o_ext[:, :LANES] / (o_ext[:, LANES:] + e_ref[buf])
            for j in range(N_PAIRS):
                a_ref[0, q_rows, pl.ds((h * N_PAIRS + j) * LANES, LANES)] = o[j * WINDOW:(j + 1) * WINDOW]

    @pl.when(n == pl.num_programs(1) - 1)
    def _():
        kwin_ref[0] = kc_ref[0, pl.ds((ATTN_BLOCKS - 1) * WINDOW, WINDOW), :]
        vwin_ref[0] = vc_ref[0, pl.ds((ATTN_BLOCKS - 1) * WINDOW, WINDOW), :]


def _attn_prompt(z3, sinks):
    b, s, _ = z3.shape
    step_rows = ATTN_BLOCKS * WINDOW
    kcol, vcol = OFF_K // D_KV, OFF_V // D_KV
    prev = lambda col: (lambda bi, n: (bi, jnp.maximum(n * ATTN_BLOCKS - 1, 0), col))
    cur = lambda col: (lambda bi, n: (bi, n, col))
    win_spec = pl.BlockSpec((1, WINDOW, D_KV), lambda bi, n: (bi, 0, 0))
    rows = N_PAIRS * WINDOW
    n_buf = ATTN_BUFFERS
    return pl.pallas_call(
        _attn_prompt_kernel,
        grid=(b, s // step_rows),
        in_specs=[
            pl.BlockSpec(memory_space=pltpu.SMEM),
            pl.BlockSpec((1, step_rows, D_Q), lambda bi, n: (bi, n, 0)),
            pl.BlockSpec((1, WINDOW, D_KV), prev(kcol)),
            pl.BlockSpec((1, step_rows, D_KV), cur(kcol)),
            pl.BlockSpec((1, WINDOW, D_KV), prev(vcol)),
            pl.BlockSpec((1, step_rows, D_KV), cur(vcol)),
        ],
        out_specs=[pl.BlockSpec((1, step_rows, D_Q), lambda bi, n: (bi, n, 0)), win_spec, win_spec],
        out_shape=[jax.ShapeDtypeStruct((b, s, D_Q), F32),
                   jax.ShapeDtypeStruct((b, WINDOW, D_KV), F32),
                   jax.ShapeDtypeStruct((b, WINDOW, D_KV), F32)],
        scratch_shapes=[pltpu.VMEM((n_buf, rows, PAIR * 2 * WINDOW), F32),
                        pltpu.VMEM((n_buf, rows, PAIR * 2 * WINDOW), BF16),
                        pltpu.VMEM((n_buf, rows, LANES), F32)],
        compiler_params=pltpu.CompilerParams(
            dimension_semantics=("parallel", "arbitrary"), vmem_limit_bytes=VMEM_LIMIT),
        name="attn_prompt",
    )(sinks, z3, z3, z3, z3, z3)


def _attn_sample_kernel(sinkrow_ref, q_ref, kn_ref, vn_ref, ck_ref, cv_ref,
                        a_ref, kwin_ref, vwin_ref, *, bt, t):
    pad_rows = 2 * WINDOW - WINDOW - t
    pad = jnp.zeros((pad_rows, D_KV), F32)
    p_pad = jnp.zeros((pad_rows, N_Q_HEADS * t), F32)
    zeros_col = jnp.zeros((t, LANES), F32)
    lo = lax.broadcasted_iota(jnp.int32, (t, LANES), 1) < HEAD_DIM
    tok = lax.broadcasted_iota(jnp.int32, (t, N_Q_HEADS * t), 1) % t
    key = lax.broadcasted_iota(jnp.int32, (t, N_Q_HEADS * t), 0)
    new_mask = key <= tok
    sinkrow = sinkrow_ref[...]

    def place(piece_col, src_odd, dst_odd):
        own = jnp.where(lo != src_odd, piece_col, 0.0)
        return own if src_odd == dst_odd else pltpu.roll(own, HEAD_DIM, axis=1)

    def body(s, carry):
        kn, vn, ck, cv = kn_ref[s], vn_ref[s], ck_ref[s], cv_ref[s]
        k_all = jnp.concatenate([ck, kn, pad], axis=0).astype(BF16)
        v_all = jnp.concatenate([cv, vn, pad], axis=0).astype(BF16)
        q = q_ref[s] * SCALE
        blocks = []
        for h in range(N_KV_HEADS):
            for g in range(GQA_GROUP):
                head = h * GQA_GROUP + g
                piece = place(q[:, (head // PAIR) * LANES:(head // PAIR + 1) * LANES],
                              head % PAIR == 1, h % PAIR == 1)
                cols = [zeros_col] * (D_KV // LANES)
                cols[h // PAIR] = piece
                blocks.append(jnp.concatenate(cols, axis=1))
        wq_t = jnp.concatenate(blocks, axis=0).astype(BF16)
        s_t = lax.dot_general(k_all, wq_t, NT_DIMS, preferred_element_type=F32)
        top = jnp.where(new_mask, s_t[WINDOW:WINDOW + t], s_t[0:t])
        s_m = jnp.concatenate([top, s_t[t:WINDOW]], axis=0)
        m = jnp.maximum(jnp.max(s_m, axis=0, keepdims=True), sinkrow)
        p = jnp.exp(s_m - m)
        denom = jnp.sum(p, axis=0, keepdims=True) + jnp.exp(sinkrow - m)
        probs = p * (1.0 / denom)
        p_all = jnp.concatenate(
            [jnp.where(new_mask, 0.0, probs[0:t]), probs[t:WINDOW],
             jnp.where(new_mask, probs[0:t], 0.0), p_pad], axis=0).astype(BF16)
        o_full = lax.dot_general(p_all, v_all, TN_DIMS, preferred_element_type=F32)
        out_cols = []
        for c in range(N_Q_HEADS // PAIR):
            h = (PAIR * c) // GQA_GROUP
            kv_col = slice((h // PAIR) * LANES, (h // PAIR + 1) * LANES)
            even = place(o_full[(PAIR * c) * t:(PAIR * c + 1) * t, kv_col], h % PAIR == 1, False)
            odd = place(o_full[(PAIR * c + 1) * t:(PAIR * c + 2) * t, kv_col], h % PAIR == 1, True)
            out_cols.append(even + odd)
        a_ref[s] = jnp.concatenate(out_cols, axis=1)
        kwin_ref[s, 0:WINDOW - t, :] = ck[t:, :]
        kwin_ref[s, WINDOW - t:WINDOW, :] = kn
        vwin_ref[s, 0:WINDOW - t, :] = cv[t:, :]
        vwin_ref[s, WINDOW - t:WINDOW, :] = vn
        return carry

    lax.fori_loop(0, bt, body, 0, unroll=Tiles.ATTN_SAMPLE_UNROLL)


def _attn_sample(z3, cache_k, cache_v, sinks, *, bt):
    b, t, _ = z3.shape
    kcol, vcol = OFF_K // D_KV, OFF_V // D_KV
    cache_spec = pl.BlockSpec((bt, WINDOW, D_KV), lambda i: (i, 0, 0))
    sinkrow = jnp.repeat(sinks.reshape(1, N_Q_HEADS), t, axis=1)
    return pl.pallas_call(
        functools.partial(_attn_sample_kernel, bt=bt, t=t),
        grid=(b // bt,),
        in_specs=[
            _const_spec((1, N_Q_HEADS * t)),
            pl.BlockSpec((bt, t, D_Q), lambda i: (i, 0, 0)),
            pl.BlockSpec((bt, t, D_KV), lambda i: (i, 0, kcol)),
            pl.BlockSpec((bt, t, D_KV), lambda i: (i, 0, vcol)),
            cache_spec, cache_spec,
        ],
        out_specs=[pl.BlockSpec((bt, t, D_Q), lambda i: (i, 0, 0)), cache_spec, cache_spec],
        out_shape=[jax.ShapeDtypeStruct((b, t, D_Q), F32),
                   jax.ShapeDtypeStruct((b, WINDOW, D_KV), F32),
                   jax.ShapeDtypeStruct((b, WINDOW, D_KV), F32)],
        compiler_params=pltpu.CompilerParams(
            dimension_semantics=("parallel",), vmem_limit_bytes=VMEM_LIMIT),
        name="attn_sample",
    )(sinkrow, z3, z3, z3, cache_k, cache_v)


def _mix_tail(pooled_parts, a, ga, gp, x, wpool_ref, pscale_ref, wout_ref, gpost_ref, gpre_ref,
              x1_ref, h2_ref):
    parts = [jnp.dot(pooled_parts[g].astype(BF16), wpool_ref[g], preferred_element_type=F32)
             for g in range(len(POOL_WINDOWS))]
    p = jnp.concatenate(parts, axis=-1) * pscale_ref[...]
    mixed = jax.nn.sigmoid(ga) * a + jax.nn.sigmoid(gp) * p
    y = jnp.dot(mixed.astype(BF16), wout_ref[...], preferred_element_type=F32)
    x1 = x + _rmsnorm(y, gpost_ref[...])
    x1_ref[...] = x1
    h2_ref[...] = _rmsnorm(x1, gpre_ref[...]).astype(BF16)


def _mix_prompt_kernel(a_ref, ga_ref, gp_ref, u_ref, halo_ref, x_ref, wpool_ref, pscale_ref,
                       wout_ref, gpost_ref, gpre_ref, x1_ref, h2_ref, pool_ref,
                       ext_ref, sum2_ref, sum4_ref, sum8_ref, *, bm, blocks_per_seq):
    assert POOL_WINDOWS == (2, 4, 8, 16)
    blk = pl.program_id(0) % blocks_per_seq
    pad = jnp.zeros((POOL_PAD, D_POOL), F32)
    ext_ref[0:POOL_PAD, :] = pad
    sum2_ref[0:POOL_PAD, :] = pad
    sum4_ref[0:POOL_PAD, :] = pad[:, POOL_GROUP:]
    ext_ref[POOL_PAD:POOL_PAD + HALO, :] = jnp.where(blk == 0, 0.0, halo_ref[...])
    ext_ref[POOL_PAD + HALO:, :] = u_ref[...]
    n = HALO + bm
    body = pl.ds(POOL_PAD, n)
    shifted = lambda s: pl.ds(POOL_PAD - s, n)
    g1 = pl.ds(POOL_GROUP, D_POOL - POOL_GROUP)
    sum2_ref[body, :] = ext_ref[body, :] + ext_ref[shifted(1), :]
    sum4_ref[body, :] = sum2_ref[body, g1] + sum2_ref[shifted(2), g1]
    sum8_ref[body, :] = sum4_ref[body, POOL_GROUP:] + sum4_ref[shifted(4), POOL_GROUP:]
    first = POOL_PAD + HALO
    rows = pl.ds(first, bm)
    sum16 = sum8_ref[rows, POOL_GROUP:] + sum8_ref[pl.ds(first - 8, bm), POOL_GROUP:]
    wsums = [sum2_ref[rows, 0:POOL_GROUP], sum4_ref[rows, 0:POOL_GROUP],
             sum8_ref[rows, 0:POOL_GROUP], sum16]
    pos = blk * bm + lax.broadcasted_iota(jnp.int32, (bm, 1), 0)
    pooled = []
    for g, w in enumerate(POOL_WINDOWS):
        inv_cnt = 1.0 / jnp.minimum(pos + 1, w).astype(F32)
        pooled.append(wsums[g] * inv_cnt - u_ref[:, pl.ds(g * POOL_GROUP, POOL_GROUP)])
    _mix_tail(pooled, a_ref[...], ga_ref[...], gp_ref[...], x_ref[...], wpool_ref, pscale_ref,
              wout_ref, gpost_ref, gpre_ref, x1_ref, h2_ref)

    @pl.when(blk == blocks_per_seq - 1)
    def _():
        pool_ref[0] = ext_ref[pl.ds(first + bm - POOL_HIST, POOL_HIST), :]


def _mix_prompt(a2d, z2d, x2d, wpool_bf, pscale, wout_bf, gpost, gpre, *, bm, seq):
    m = x2d.shape[0]
    blocks_per_seq = seq // bm
    row_spec = lambda col: pl.BlockSpec((bm, D_MODEL), lambda i: (i, col))
    halo_blocks = bm // HALO
    return pl.pallas_call(
        functools.partial(_mix_prompt_kernel, bm=bm, blocks_per_seq=blocks_per_seq),
        grid=(m // bm,),
        in_specs=[
            row_spec(0),
            row_spec(OFF_GA // D_MODEL),
            row_spec(OFF_GP // D_MODEL),
            pl.BlockSpec((bm, D_POOL), lambda i: (i, OFF_U // D_POOL)),
            pl.BlockSpec((HALO, D_POOL),
                         lambda i: (jnp.maximum(i * halo_blocks - 1, 0), OFF_U // D_POOL)),
            row_spec(0),
            _const_spec((len(POOL_WINDOWS), POOL_GROUP, POOL_OUT_GROUP)),
            _const_spec((1, D_MODEL)),
            _const_spec((D_MODEL, D_MODEL)),
            _const_spec((1, D_MODEL)),
            _const_spec((1, D_MODEL)),
        ],
        out_specs=[row_spec(0), row_spec(0),
                   pl.BlockSpec((1, POOL_HIST, D_POOL), lambda i: (i // blocks_per_seq, 0, 0))],
        out_shape=[jax.ShapeDtypeStruct((m, D_MODEL), F32),
                   jax.ShapeDtypeStruct((m, D_MODEL), BF16),
                   jax.ShapeDtypeStruct((m // seq, POOL_HIST, D_POOL), F32)],
        scratch_shapes=[pltpu.VMEM((POOL_PAD + HALO + bm, D_POOL), F32),
                        pltpu.VMEM((POOL_PAD + HALO + bm, D_POOL), F32),
                        pltpu.VMEM((POOL_PAD + HALO + bm, D_POOL - POOL_GROUP), F32),
                        pltpu.VMEM((POOL_PAD + HALO + bm, D_POOL - 2 * POOL_GROUP), F32)],
        compiler_params=pltpu.CompilerParams(
            dimension_semantics=("arbitrary",), vmem_limit_bytes=VMEM_LIMIT),
        name="mix_prompt",
    )(a2d, z2d, z2d, z2d, z2d, x2d, wpool_bf, pscale, wout_bf, gpost, gpre)


def _mix_sample_kernel(a_ref, ga_ref, gp_ref, u_ref, hist_ref, x_ref, wpool_ref, pscale_ref,
                       wout_ref, gpost_ref, gpre_ref, x1_ref, h2_ref, pool_ref, ext_ref,
                       *, bt, t, pos0):
    ext_ref[:, HALO - POOL_HIST:HALO, :] = hist_ref[...]
    ext_ref[:, HALO:, :] = u_ref[...].reshape(bt, t, D_POOL)
    pooled = []
    for g, w in enumerate(POOL_WINDOWS):
        lanes = pl.ds(g * POOL_GROUP, POOL_GROUP)
        wsum = ext_ref[:, pl.ds(HALO, t), lanes]
        for s in range(1, w):
            wsum = wsum + ext_ref[:, pl.ds(HALO - s, t), lanes]
        assert pos0 + 1 >= w
        pooled.append(wsum.reshape(bt * t, POOL_GROUP) * (1.0 / w) - u_ref[:, lanes])
    _mix_tail(pooled, a_ref[...], ga_ref[...], gp_ref[...], x_ref[...], wpool_ref, pscale_ref,
              wout_ref, gpost_ref, gpre_ref, x1_ref, h2_ref)
    pool_ref[...] = ext_ref[:, pl.ds(HALO + t - POOL_HIST, POOL_HIST), :]


def _mix_sample(a2d, z2d, x2d, hist, wpool_bf, pscale, wout_bf, gpost, gpre, *, bt, t):
    m = x2d.shape[0]
    bm = bt * t
    row_spec = lambda col: pl.BlockSpec((bm, D_MODEL), lambda i: (i, col))
    return pl.pallas_call(
        functools.partial(_mix_sample_kernel, bt=bt, t=t, pos0=PAST_LEN),
        grid=(m // bm,),
        in_specs=[
            row_spec(0),
            row_spec(OFF_GA // D_MODEL),
            row_spec(OFF_GP // D_MODEL),
            pl.BlockSpec((bm, D_POOL), lambda i: (i, OFF_U // D_POOL)),
            pl.BlockSpec((bt, POOL_HIST, D_POOL), lambda i: (i, 0, 0)),
            row_spec(0),
            _const_spec((len(POOL_WINDOWS), POOL_GROUP, POOL_OUT_GROUP)),
            _const_spec((1, D_MODEL)),
            _const_spec((D_MODEL, D_MODEL)),
            _const_spec((1, D_MODEL)),
            _const_spec((1, D_MODEL)),
        ],
        out_specs=[row_spec(0), row_spec(0),
                   pl.BlockSpec((bt, POOL_HIST, D_POOL), lambda i: (i, 0, 0))],
        out_shape=[jax.ShapeDtypeStruct((m, D_MODEL), F32),
                   jax.ShapeDtypeStruct((m, D_MODEL), BF16),
                   jax.ShapeDtypeStruct((m // t, POOL_HIST, D_POOL), F32)],
        scratch_shapes=[pltpu.VMEM((bt, HALO + t, D_POOL), F32)],
        compiler_params=pltpu.CompilerParams(
            dimension_semantics=("parallel",), vmem_limit_bytes=VMEM_LIMIT),
        name="mix_sample",
    )(a2d, z2d, z2d, z2d, hist, x2d, wpool_bf, pscale, wout_bf, gpost, gpre)


MLP_TAIL_ROWS = 64


def _mlp_kernel(h2_ref, x1_slice_ref, wup_ref, wdn_ref, g_ref, y_ref, x1_ref, *, bm, slice_rows):
    j = pl.program_id(1)

    @pl.when(j == 0)
    def _():
        y_ref[...] = jnp.zeros_like(y_ref)

    x1_ref[pl.ds(pl.multiple_of(j * slice_rows, slice_rows), slice_rows), :] = x1_slice_ref[...]
    hid = jnp.dot(h2_ref[...], wup_ref[...], preferred_element_type=F32)
    hid = jnp.square(jnp.maximum(hid, 0.0)).astype(BF16)
    y_ref[...] += jnp.dot(hid, wdn_ref[...], preferred_element_type=F32)

    @pl.when(j == pl.num_programs(1) - 1)
    def _():
        def tail(c, carry):
            rows = pl.ds(pl.multiple_of(c * MLP_TAIL_ROWS, MLP_TAIL_ROWS), MLP_TAIL_ROWS)
            y_ref[rows, :] = x1_ref[rows, :] + _rmsnorm(y_ref[rows, :], g_ref[...])
            return carry

        lax.fori_loop(0, bm // MLP_TAIL_ROWS, tail, 0)


def _mlp_cast_kernel(h2_ref, x1_ref, wup_ref, wdn_ref, g_ref, y_ref, wup_bf_ref, wdn_bf_ref):
    j = pl.program_id(0)

    @pl.when(j == 0)
    def _():
        y_ref[...] = jnp.zeros_like(y_ref)

    wup = wup_ref[...].astype(BF16)
    wdn = wdn_ref[...].astype(BF16)
    wup_bf_ref[...] = wup
    wdn_bf_ref[...] = wdn
    hid = jnp.dot(h2_ref[...], wup, preferred_element_type=F32)
    hid = jnp.square(jnp.maximum(hid, 0.0)).astype(BF16)
    y_ref[...] += jnp.dot(hid, wdn, preferred_element_type=F32)

    @pl.when(j == pl.num_programs(0) - 1)
    def _():
        y_ref[...] = x1_ref[...] + _rmsnorm(y_ref[...], g_ref[...])


def _mlp_cast(h2, x1, wup_f32, wdn_f32, g, *, fc):
    m = x1.shape[0]
    return pl.pallas_call(
        _mlp_cast_kernel,
        grid=(D_FF // fc,),
        in_specs=[
            _const_spec((m, D_MODEL)),
            _const_spec((m, D_MODEL)),
            pl.BlockSpec((D_MODEL, fc), lambda j: (0, j)),
            pl.BlockSpec((fc, D_MODEL), lambda j: (j, 0)),
            _const_spec((1, D_MODEL)),
        ],
        out_specs=[pl.BlockSpec((m, D_MODEL), lambda j: (0, 0)),
                   pl.BlockSpec((D_MODEL, fc), lambda j: (0, j)),
                   pl.BlockSpec((fc, D_MODEL), lambda j: (j, 0))],
        out_shape=[jax.ShapeDtypeStruct((m, D_MODEL), F32),
                   jax.ShapeDtypeStruct((D_MODEL, D_FF), BF16),
                   jax.ShapeDtypeStruct((D_FF, D_MODEL), BF16)],
        compiler_params=pltpu.CompilerParams(
            dimension_semantics=("arbitrary",), vmem_limit_bytes=VMEM_LIMIT),
        name="mlp_cast",
    )(h2, x1, wup_f32, wdn_f32, g)


def _mlp(h2, x1, wup_bf, wdn_bf, g, *, bm, fc):
    m = x1.shape[0]
    n_ff = D_FF // fc
    slice_rows = bm // n_ff
    return pl.pallas_call(
        functools.partial(_mlp_kernel, bm=bm, slice_rows=slice_rows),
        grid=(m // bm, n_ff),
        in_specs=[
            pl.BlockSpec((bm, D_MODEL), lambda i, j: (i, 0)),
            pl.BlockSpec((slice_rows, D_MODEL), lambda i, j: (i * n_ff + j, 0)),
            pl.BlockSpec((D_MODEL, fc), lambda i, j: (0, j)),
            pl.BlockSpec((fc, D_MODEL), lambda i, j: (j, 0)),
            _const_spec((1, D_MODEL)),
        ],
        out_specs=pl.BlockSpec((bm, D_MODEL), lambda i, j: (i, 0)),
        out_shape=jax.ShapeDtypeStruct((m, D_MODEL), F32),
        scratch_shapes=[pltpu.VMEM((bm, D_MODEL), F32)],
        compiler_params=pltpu.CompilerParams(
            dimension_semantics=("parallel", "arbitrary"), vmem_limit_bytes=VMEM_LIMIT),
        name="mlp",
    )(h2, x1, wup_bf, wdn_bf, g)


def kernel(x_prompt, x_sample, cache_k_win, cache_v_win, state_pool, norm_attn_pre, norm_attn_post,
           w_in, attn_sinks, w_pool, pool_scale, w_out, norm_mlp_pre, norm_mlp_post, w_up, w_down):
    depth = w_in.shape[0]
    assert depth == 1
    b, s, _ = x_prompt.shape
    bs, t, _ = x_sample.shape

    l = 0
    w_pool_bf = w_pool[l].astype(BF16)
    w_out_bf = w_out[l].astype(BF16)
    row = lambda v: v[l].reshape(1, D_MODEL)
    g_attn_pre, g_attn_post = row(norm_attn_pre), row(norm_attn_post)
    g_mlp_pre, g_mlp_post = row(norm_mlp_pre), row(norm_mlp_post)
    pscale = row(pool_scale)
    sinks = attn_sinks[l]

    xs = x_sample.reshape(bs * t, D_MODEL)
    zs, w_in_bf = _in_proj_cast(xs, g_attn_pre, w_in[l])
    ck = cache_k_win[l].reshape(bs, WINDOW, D_KV)
    cv = cache_v_win[l].reshape(bs, WINDOW, D_KV)
    as_, ks, vs = _attn_sample(zs.reshape(bs, t, D_IN), ck, cv, sinks, bt=Tiles.ATTN_SAMPLE_SEQS)
    x1s, h2s, pools = _mix_sample(as_.reshape(bs * t, D_Q), zs, xs, state_pool[l], w_pool_bf, pscale,
                                  w_out_bf, g_attn_post, g_mlp_pre, bt=Tiles.MIX_SAMPLE_SEQS, t=t)
    ys, w_up_bf, w_down_bf = _mlp_cast(h2s, x1s, w_up[l], w_down[l], g_mlp_post,
                                       fc=Tiles.MLP_CAST_FF_COLS)

    xp = x_prompt.reshape(b * s, D_MODEL)
    zp = _in_proj(xp, g_attn_pre, w_in_bf, bm=Tiles.IN_PROJ_ROWS, bn=Tiles.IN_PROJ_COLS)
    ap, kp, vp = _attn_prompt(zp.reshape(b, s, D_IN), sinks)
    x1p, h2p, poolp = _mix_prompt(ap.reshape(b * s, D_Q), zp, xp, w_pool_bf, pscale, w_out_bf,
                                  g_attn_post, g_mlp_pre, bm=Tiles.MIX_ROWS, seq=s)
    yp = _mlp(h2p, x1p, w_up_bf, w_down_bf, g_mlp_post, bm=Tiles.MLP_ROWS, fc=Tiles.MLP_FF_COLS)

    kv_shape = lambda nb: (1, nb, WINDOW, N_KV_HEADS, HEAD_DIM)
    return (yp.reshape(b, s, D_MODEL), ys.reshape(bs, t, D_MODEL),
            kp.reshape(kv_shape(b)), vp.reshape(kv_shape(b)), poolp[None],
            ks.reshape(kv_shape(bs)), vs.reshape(kv_shape(bs)), pools[None])
```

```python
import functools

import jax
import jax.numpy as jnp
from jax import lax
from jax.experimental import pallas as pl
from jax.experimental.pallas import tpu as pltpu

F32 = jnp.float32
BF16 = jnp.bfloat16

D_MODEL = 2048
HEAD_DIM = 64
N_KV_HEADS = 4
GQA_GROUP = 8
N_Q_HEADS = N_KV_HEADS * GQA_GROUP
WINDOW = 128
D_Q = 2048
D_KV = 256
POOL_WINDOWS = (2, 4, 8, 16)
D_POOL = 1024
POOL_GROUP = 256
POOL_OUT_GROUP = 512
POOL_HIST = 15
D_FF = 8192
D_IN = D_Q + 2 * D_KV + D_POOL + 2 * D_MODEL
PAST_LEN = 8192
EPS = 1e-6
NEG_INF = -1e30
SCALE = HEAD_DIM ** -0.5
LOG2E = 1.4426950408889634

LANES = 128
PAIR = 2
N_PAIRS = GQA_GROUP // PAIR
ROW_CHUNK = 16

OFF_Q, OFF_GA, OFF_GP, OFF_U, OFF_K, OFF_V = 0, 2048, 4096, 6144, 7168, 7424
HALO = 16
POOL_PAD = 8

VMEM_LIMIT = 60 * 1024 * 1024
MXU_TILE = 256


class Tiles:
    IN_PROJ_ROWS, IN_PROJ_COLS = 1024, 6 * MXU_TILE
    IN_PROJ_CAST_COLS = 2 * MXU_TILE
    MIX_ROWS = 256
    MLP_ROWS, MLP_FF_COLS = 1024, 4 * MXU_TILE
    MLP_CAST_FF_COLS = 2 * MXU_TILE
    ATTN_SAMPLE_SEQS = 16
    ATTN_SAMPLE_UNROLL = 8
    MIX_SAMPLE_SEQS = 32

NT_DIMS = (((1,), (1,)), ((), ()))
TN_DIMS = (((0,), (0,)), ((), ()))


def _rmsnorm(x, g):
    r = lax.rsqrt(jnp.mean(x * x, axis=-1, keepdims=True) + EPS)
    return x * r * g


def _const_spec(shape):
    return pl.BlockSpec(shape, lambda *_: (0,) * len(shape), pipeline_mode=pl.Buffered(1))


def _in_proj_kernel(x_ref, g_ref, w_ref, z_ref, h_ref):
    @pl.when(pl.program_id(1) == 0)
    def _():
        h_ref[...] = _rmsnorm(x_ref[...], g_ref[...]).astype(BF16)

    z_ref[...] = jnp.dot(h_ref[...], w_ref[...], preferred_element_type=F32)


def _in_proj_cast_kernel(x_ref, g_ref, w_ref, z_ref, wbf_ref, h_ref):
    @pl.when(pl.program_id(0) == 0)
    def _():
        h_ref[...] = _rmsnorm(x_ref[...], g_ref[...]).astype(BF16)

    w = w_ref[...].astype(BF16)
    wbf_ref[...] = w
    z_ref[...] = jnp.dot(h_ref[...], w, preferred_element_type=F32)


def _in_proj_cast(x2d, g, w_f32):
    m = x2d.shape[0]
    bn = Tiles.IN_PROJ_CAST_COLS
    src_bounds = (D_Q, D_Q + 2 * D_KV, D_Q + 2 * D_KV + D_POOL, D_Q + 2 * D_KV + D_POOL + D_MODEL)
    assert all(off % bn == 0 for off in src_bounds + (OFF_GA, OFF_GP, OFF_U, OFF_K))

    def reordered(j):
        q_end, kv_end, u_end, ga_end = (off // bn for off in src_bounds)
        return jnp.where(
            j < q_end, j, jnp.where(
                j < kv_end, OFF_K // bn + (j - q_end), jnp.where(
                    j < u_end, OFF_U // bn + (j - kv_end), jnp.where(
                        j < ga_end, OFF_GA // bn + (j - u_end), OFF_GP // bn + (j - ga_end)))))

    return pl.pallas_call(
        _in_proj_cast_kernel,
        grid=(D_IN // bn,),
        in_specs=[
            _const_spec((m, D_MODEL)),
            _const_spec((1, D_MODEL)),
            pl.BlockSpec((D_MODEL, bn), lambda j: (0, j)),
        ],
        out_specs=[pl.BlockSpec((m, bn), lambda j: (0, reordered(j))),
                   pl.BlockSpec((D_MODEL, bn), lambda j: (0, reordered(j)))],
        out_shape=[jax.ShapeDtypeStruct((m, D_IN), F32),
                   jax.ShapeDtypeStruct((D_MODEL, D_IN), BF16)],
        scratch_shapes=[pltpu.VMEM((m, D_MODEL), BF16)],
        compiler_params=pltpu.CompilerParams(
            dimension_semantics=("arbitrary",), vmem_limit_bytes=VMEM_LIMIT),
        name="in_proj_cast",
    )(x2d, g, w_f32)


def _in_proj(x2d, g, w_bf, *, bm, bn):
    m = x2d.shape[0]
    return pl.pallas_call(
        _in_proj_kernel,
        grid=(m // bm, D_IN // bn),
        in_specs=[
            pl.BlockSpec((bm, D_MODEL), lambda i, j: (i, 0)),
            _const_spec((1, D_MODEL)),
            pl.BlockSpec((D_MODEL, bn), lambda i, j: (0, j)),
        ],
        out_specs=pl.BlockSpec((bm, bn), lambda i, j: (i, j)),
        out_shape=jax.ShapeDtypeStruct((m, D_IN), F32),
        scratch_shapes=[pltpu.VMEM((bm, D_MODEL), BF16)],
        compiler_params=pltpu.CompilerParams(
            dimension_semantics=("parallel", "arbitrary"), vmem_limit_bytes=VMEM_LIMIT),
        name="in_proj",
    )(x2d, g, w_bf)


def _own_and_swapped(slab, odd, lo):
    own = jnp.where(lo != odd, slab, 0.0)
    swapped = pltpu.roll(own, HEAD_DIM, axis=1)
    return (swapped, own) if odd else (own, swapped)


ATTN_BLOCKS = 8
ATTN_BUFFERS = 8


def _attn_prompt_kernel(sinks_ref, q_ref, kp_ref, kc_ref, vp_ref, vc_ref,
                        a_ref, kwin_ref, vwin_ref, s_ref, p_ref, e_ref):
    n = pl.program_id(1)
    k_rows = jnp.concatenate([kp_ref[0], kc_ref[0]], axis=0)
    v_rows = jnp.concatenate([vp_ref[0], vc_ref[0]], axis=0)
    first_prev_bias = jnp.where(n > 0, 0.0, NEG_INF).astype(F32)
    lo_kv = lax.broadcasted_iota(jnp.int32, (2 * WINDOW, LANES), 1) < HEAD_DIM
    ones_lo = jnp.where(lo_kv, 1.0, 0.0)
    ones_hi = 1.0 - ones_lo
    lo_c = lax.broadcasted_iota(jnp.int32, (ROW_CHUNK, LANES), 1) < HEAD_DIM
    lane_c = lax.broadcasted_iota(jnp.int32, (ROW_CHUNK, LANES), 1)
    row_c = lax.broadcasted_iota(jnp.int32, (ROW_CHUNK, LANES), 0)

    for blk in range(ATTN_BLOCKS):
        k_all = k_rows[blk * WINDOW:(blk + 2) * WINDOW]
        v_all = v_rows[blk * WINDOW:(blk + 2) * WINDOW]
        q_rows = pl.ds(blk * WINDOW, WINDOW)
        for h in range(N_KV_HEADS):
            buf = (blk * N_KV_HEADS + h) % ATTN_BUFFERS
            odd = (h % PAIR) == 1
            kv_col = slice((h // PAIR) * LANES, (h // PAIR + 1) * LANES)
            k_l, k_r = _own_and_swapped(k_all[:, kv_col], odd, lo_kv)
            v_l, v_r = _own_and_swapped(v_all[:, kv_col], odd, lo_kv)
            wk = jnp.concatenate([k_l, k_r], axis=0).astype(BF16)
            vext = jnp.concatenate(
                [jnp.concatenate([v_l, ones_lo], axis=1),
                 jnp.concatenate([v_r, ones_hi], axis=1)], axis=0).astype(BF16)
            qh = jnp.concatenate(
                [q_ref[0, q_rows, pl.ds((h * N_PAIRS + j) * LANES, LANES)] for j in range(N_PAIRS)],
                axis=0)
            qh = (qh * (SCALE * LOG2E)).astype(BF16)
            s_ref[buf] = lax.dot_general(qh, wk, NT_DIMS, preferred_element_type=F32)

            for j in range(N_PAIRS):
                sinks = [sinks_ref[h, PAIR * j] * LOG2E, sinks_ref[h, PAIR * j + 1] * LOG2E]
                for c in range(WINDOW // ROW_CHUNK):
                    rows = pl.ds(j * WINDOW + c * ROW_CHUNK, ROW_CHUNK)
                    mask = lane_c <= (row_c + c * ROW_CHUNK)
                    e_parts = []
                    for gi in range(PAIR):
                        base = gi * 2 * WINDOW
                        cur = s_ref[buf, rows, pl.ds(base + WINDOW, WINDOW)]
                        prev = s_ref[buf, rows, pl.ds(base, WINDOW)]
                        if blk == 0:
                            prev = prev + first_prev_bias
                        s = jnp.where(mask, cur, prev)
                        m = jnp.maximum(jnp.max(s, axis=-1, keepdims=True), sinks[gi])
                        p = jnp.exp2(s - m)
                        p_ref[buf, rows, pl.ds(base, WINDOW)] = jnp.where(mask, 0.0, p).astype(BF16)
                        p_ref[buf, rows, pl.ds(base + WINDOW, WINDOW)] = (
                            jnp.where(mask, p, 0.0).astype(BF16))
                        e_parts.append(jnp.broadcast_to(jnp.exp2(sinks[gi] - m), (ROW_CHUNK, LANES)))
                    e_ref[buf, rows, :] = jnp.where(lo_c, e_parts[0], e_parts[1])

            o_ext = jnp.dot(p_ref[buf], vext, preferred_element_type=F32)
            o = o_ext[:, :LANES] / (o_ext[:, LANES:] + e_ref[buf])
            for j in range(N_PAIRS):
                a_ref[0, q_rows, pl.ds((h * N_PAIRS + j) * LANES, LANES)] = o[j * WINDOW:(j + 1) * WINDOW]

    @pl.when(n == pl.num_programs(1) - 1)
    def _():
        kwin_ref[0] = kc_ref[0, pl.ds((ATTN_BLOCKS - 1) * WINDOW, WINDOW), :]
        vwin_ref[0] = vc_ref[0, pl.ds((ATTN_BLOCKS - 1) * WINDOW, WINDOW), :]


def _attn_prompt(z3, sinks):
    b, s, _ = z3.shape
    step_rows = ATTN_BLOCKS * WINDOW
    kcol, vcol = OFF_K // D_KV, OFF_V // D_KV
    prev = lambda col: (lambda bi, n: (bi, jnp.maximum(n * ATTN_BLOCKS - 1, 0), col))
    cur = lambda col: (lambda bi, n: (bi, n, col))
    win_spec = pl.BlockSpec((1, WINDOW, D_KV), lambda bi, n: (bi, 0, 0))
    rows = N_PAIRS * WINDOW
    n_buf = ATTN_BUFFERS
    return pl.pallas_call(
        _attn_prompt_kernel,
        grid=(b, s // step_rows),
        in_specs=[
            pl.BlockSpec(memory_space=pltpu.SMEM),
            pl.BlockSpec((1, step_rows, D_Q), lambda bi, n: (bi, n, 0)),
            pl.BlockSpec((1, WINDOW, D_KV), prev(kcol)),
            pl.BlockSpec((1, step_rows, D_KV), cur(kcol)),
            pl.BlockSpec((1, WINDOW, D_KV), prev(vcol)),
            pl.BlockSpec((1, step_rows, D_KV), cur(vcol)),
        ],
        out_specs=[pl.BlockSpec((1, step_rows, D_Q), lambda bi, n: (bi, n, 0)), win_spec, win_spec],
        out_shape=[jax.ShapeDtypeStruct((b, s, D_Q), F32),
                   jax.ShapeDtypeStruct((b, WINDOW, D_KV), F32),
                   jax.ShapeDtypeStruct((b, WINDOW, D_KV), F32)],
        scratch_shapes=[pltpu.VMEM((n_buf, rows, PAIR * 2 * WINDOW), F32),
                        pltpu.VMEM((n_buf, rows, PAIR * 2 * WINDOW), BF16),
                        pltpu.VMEM((n_buf, rows, LANES), F32)],
        compiler_params=pltpu.CompilerParams(
            dimension_semantics=("parallel", "arbitrary"), vmem_limit_bytes=VMEM_LIMIT),
        name="attn_prompt",
    )(sinks, z3, z3, z3, z3, z3)


def _attn_sample_kernel(sinkrow_ref, q_ref, kn_ref, vn_ref, ck_ref, cv_ref,
                        a_ref, kwin_ref, vwin_ref, *, bt, t):
    pad_rows = 2 * WINDOW - WINDOW - t
    pad = jnp.zeros((pad_rows, D_KV), F32)
    p_pad = jnp.zeros((pad_rows, N_Q_HEADS * t), F32)
    zeros_col = jnp.zeros((t, LANES), F32)
    lo = lax.broadcasted_iota(jnp.int32, (t, LANES), 1) < HEAD_DIM
    tok = lax.broadcasted_iota(jnp.int32, (t, N_Q_HEADS * t), 1) % t
    key = lax.broadcasted_iota(jnp.int32, (t, N_Q_HEADS * t), 0)
    new_mask = key <= tok
    sinkrow = sinkrow_ref[...]

    def place(piece_col, src_odd, dst_odd):
        own = jnp.where(lo != src_odd, piece_col, 0.0)
        return own if src_odd == dst_odd else pltpu.roll(own, HEAD_DIM, axis=1)

    def body(s, carry):
        kn, vn, ck, cv = kn_ref[s], vn_ref[s], ck_ref[s], cv_ref[s]
        k_all = jnp.concatenate([ck, kn, pad], axis=0).astype(BF16)
        v_all = jnp.concatenate([cv, vn, pad], axis=0).astype(BF16)
        q = q_ref[s] * SCALE
        blocks = []
        for h in range(N_KV_HEADS):
            for g in range(GQA_GROUP):
                head = h * GQA_GROUP + g
                piece = place(q[:, (head // PAIR) * LANES:(head // PAIR + 1) * LANES],
                              head % PAIR == 1, h % PAIR == 1)
                cols = [zeros_col] * (D_KV // LANES)
                cols[h // PAIR] = piece
                blocks.append(jnp.concatenate(cols, axis=1))
        wq_t = jnp.concatenate(blocks, axis=0).astype(BF16)
        s_t = lax.dot_general(k_all, wq_t, NT_DIMS, preferred_element_type=F32)
        top = jnp.where(new_mask, s_t[WINDOW:WINDOW + t], s_t[0:t])
        s_m = jnp.concatenate([top, s_t[t:WINDOW]], axis=0)
        m = jnp.maximum(jnp.max(s_m, axis=0, keepdims=True), sinkrow)
        p = jnp.exp(s_m - m)
        denom = jnp.sum(p, axis=0, keepdims=True) + jnp.exp(sinkrow - m)
        probs = p * (1.0 / denom)
        p_all = jnp.concatenate(
            [jnp.where(new_mask, 0.0, probs[0:t]), probs[t:WINDOW],
             jnp.where(new_mask, probs[0:t], 0.0), p_pad], axis=0).astype(BF16)
        o_full = lax.dot_general(p_all, v_all, TN_DIMS, preferred_element_type=F32)
        out_cols = []
        for c in range(N_Q_HEADS // PAIR):
            h = (PAIR * c) // GQA_GROUP
            kv_col = slice((h // PAIR) * LANES, (h // PAIR + 1) * LANES)
            even = place(o_full[(PAIR * c) * t:(PAIR * c + 1) * t, kv_col], h % PAIR == 1, False)
            odd = place(o_full[(PAIR * c + 1) * t:(PAIR * c + 2) * t, kv_col], h % PAIR == 1, True)
            out_cols.append(even + odd)
        a_ref[s] = jnp.concatenate(out_cols, axis=1)
        kwin_ref[s, 0:WINDOW - t, :] = ck[t:, :]
        kwin_ref[s, WINDOW - t:WINDOW, :] = kn
        vwin_ref[s, 0:WINDOW - t, :] = cv[t:, :]
        vwin_ref[s, WINDOW - t:WINDOW, :] = vn
        return carry

    lax.fori_loop(0, bt, body, 0, unroll=Tiles.ATTN_SAMPLE_UNROLL)


def _attn_sample(z3, cache_k, cache_v, sinks, *, bt):
    b, t, _ = z3.shape
    kcol, vcol = OFF_K // D_KV, OFF_V // D_KV
    cache_spec = pl.BlockSpec((bt, WINDOW, D_KV), lambda i: (i, 0, 0))
    sinkrow = jnp.repeat(sinks.reshape(1, N_Q_HEADS), t, axis=1)
    return pl.pallas_call(
        functools.partial(_attn_sample_kernel, bt=bt, t=t),
        grid=(b // bt,),
        in_specs=[
            _const_spec((1, N_Q_HEADS * t)),
            pl.BlockSpec((bt, t, D_Q), lambda i: (i, 0, 0)),
            pl.BlockSpec((bt, t, D_KV), lambda i: (i, 0, kcol)),
            pl.BlockSpec((bt, t, D_KV), lambda i: (i, 0, vcol)),
            cache_spec, cache_spec,
        ],
        out_specs=[pl.BlockSpec((bt, t, D_Q), lambda i: (i, 0, 0)), cache_spec, cache_spec],
        out_shape=[jax.ShapeDtypeStruct((b, t, D_Q), F32),
                   jax.ShapeDtypeStruct((b, WINDOW, D_KV), F32),
                   jax.ShapeDtypeStruct((b, WINDOW, D_KV), F32)],
        compiler_params=pltpu.CompilerParams(
            dimension_semantics=("parallel",), vmem_limit_bytes=VMEM_LIMIT),
        name="attn_sample",
    )(sinkrow, z3, z3, z3, cache_k, cache_v)


def _mix_tail(pooled_parts, a, ga, gp, x, wpool_ref, pscale_ref, wout_ref, gpost_ref, gpre_ref,
              x1_ref, h2_ref):
    parts = [jnp.dot(pooled_parts[g].astype(BF16), wpool_ref[g], preferred_element_type=F32)
             for g in range(len(POOL_WINDOWS))]
    p = jnp.concatenate(parts, axis=-1) * pscale_ref[...]
    mixed = jax.nn.sigmoid(ga) * a + jax.nn.sigmoid(gp) * p
    y = jnp.dot(mixed.astype(BF16), wout_ref[...], preferred_element_type=F32)
    x1 = x + _rmsnorm(y, gpost_ref[...])
    x1_ref[...] = x1
    h2_ref[...] = _rmsnorm(x1, gpre_ref[...]).astype(BF16)


def _mix_prompt_kernel(a_ref, ga_ref, gp_ref, u_ref, halo_ref, x_ref, wpool_ref, pscale_ref,
                       wout_ref, gpost_ref, gpre_ref, x1_ref, h2_ref, pool_ref,
                       ext_ref, sum2_ref, sum4_ref, sum8_ref, *, bm, blocks_per_seq):
    assert POOL_WINDOWS == (2, 4, 8, 16)
    blk = pl.program_id(0) % blocks_per_seq
    pad = jnp.zeros((POOL_PAD, D_POOL), F32)
    ext_ref[0:POOL_PAD, :] = pad
    sum2_ref[0:POOL_PAD, :] = pad
    sum4_ref[0:POOL_PAD, :] = pad[:, POOL_GROUP:]
    ext_ref[POOL_PAD:POOL_PAD + HALO, :] = jnp.where(blk == 0, 0.0, halo_ref[...])
    ext_ref[POOL_PAD + HALO:, :] = u_ref[...]
    n = HALO + bm
    body = pl.ds(POOL_PAD, n)
    shifted = lambda s: pl.ds(POOL_PAD - s, n)
    g1 = pl.ds(POOL_GROUP, D_POOL - POOL_GROUP)
    sum2_ref[body, :] = ext_ref[body, :] + ext_ref[shifted(1), :]
    sum4_ref[body, :] = sum2_ref[body, g1] + sum2_ref[shifted(2), g1]
    sum8_ref[body, :] = sum4_ref[body, POOL_GROUP:] + sum4_ref[shifted(4), POOL_GROUP:]
    first = POOL_PAD + HALO
    rows = pl.ds(first, bm)
    sum16 = sum8_ref[rows, POOL_GROUP:] + sum8_ref[pl.ds(first - 8, bm), POOL_GROUP:]
    wsums = [sum2_ref[rows, 0:POOL_GROUP], sum4_ref[rows, 0:POOL_GROUP],
             sum8_ref[rows, 0:POOL_GROUP], sum16]
    pos = blk * bm + lax.broadcasted_iota(jnp.int32, (bm, 1), 0)
    pooled = []
    for g, w in enumerate(POOL_WINDOWS):
        inv_cnt = 1.0 / jnp.minimum(pos + 1, w).astype(F32)
        pooled.append(wsums[g] * inv_cnt - u_ref[:, pl.ds(g * POOL_GROUP, POOL_GROUP)])
    _mix_tail(pooled, a_ref[...], ga_ref[...], gp_ref[...], x_ref[...], wpool_ref, pscale_ref,
              wout_ref, gpost_ref, gpre_ref, x1_ref, h2_ref)

    @pl.when(blk == blocks_per_seq - 1)
    def _():
        pool_ref[0] = ext_ref[pl.ds(first + bm - POOL_HIST, POOL_HIST), :]


def _mix_prompt(a2d, z2d, x2d, wpool_bf, pscale, wout_bf, gpost, gpre, *, bm, seq):
    m = x2d.shape[0]
    blocks_per_seq = seq // bm
    row_spec = lambda col: pl.BlockSpec((bm, D_MODEL), lambda i: (i, col))
    halo_blocks = bm // HALO
    return pl.pallas_call(
        functools.partial(_mix_prompt_kernel, bm=bm, blocks_per_seq=blocks_per_seq),
        grid=(m // bm,),
        in_specs=[
            row_spec(0),
            row_spec(OFF_GA // D_MODEL),
            row_spec(OFF_GP // D_MODEL),
            pl.BlockSpec((bm, D_POOL), lambda i: (i, OFF_U // D_POOL)),
            pl.BlockSpec((HALO, D_POOL),
                         lambda i: (jnp.maximum(i * halo_blocks - 1, 0), OFF_U // D_POOL)),
            row_spec(0),
            _const_spec((len(POOL_WINDOWS), POOL_GROUP, POOL_OUT_GROUP)),
            _const_spec((1, D_MODEL)),
            _const_spec((D_MODEL, D_MODEL)),
            _const_spec((1, D_MODEL)),
            _const_spec((1, D_MODEL)),
        ],
        out_specs=[row_spec(0), row_spec(0),
                   pl.BlockSpec((1, POOL_HIST, D_POOL), lambda i: (i // blocks_per_seq, 0, 0))],
        out_shape=[jax.ShapeDtypeStruct((m, D_MODEL), F32),
                   jax.ShapeDtypeStruct((m, D_MODEL), BF16),
                   jax.ShapeDtypeStruct((m // seq, POOL_HIST, D_POOL), F32)],
        scratch_shapes=[pltpu.VMEM((POOL_PAD + HALO + bm, D_POOL), F32),
                        pltpu.VMEM((POOL_PAD + HALO + bm, D_POOL), F32),
                        pltpu.VMEM((POOL_PAD + HALO + bm, D_POOL - POOL_GROUP), F32),
                        pltpu.VMEM((POOL_PAD + HALO + bm, D_POOL - 2 * POOL_GROUP), F32)],
        compiler_params=pltpu.CompilerParams(
            dimension_semantics=("arbitrary",), vmem_limit_bytes=VMEM_LIMIT),
        name="mix_prompt",
    )(a2d, z2d, z2d, z2d, z2d, x2d, wpool_bf, pscale, wout_bf, gpost, gpre)


def _mix_sample_kernel(a_ref, ga_ref, gp_ref, u_ref, hist_ref, x_ref, wpool_ref, pscale_ref,
                       wout_ref, gpost_ref, gpre_ref, x1_ref, h2_ref, pool_ref, *, bt, t, pos0):
    assert t == 8 and max(POOL_WINDOWS) >= t and pos0 + 1 >= max(POOL_WINDOWS)
    u3 = u_ref[...].reshape(bt, t, D_POOL)
    def shifted(x, k):
        tok = lax.broadcasted_iota(jnp.int32, x.shape, 1)
        return jnp.where(tok >= k, pltpu.roll(x, k, axis=1), 0.0)

    sum2 = u3 + shifted(u3, 1)
    sum4 = sum2[..., POOL_GROUP:] + shifted(sum2[..., POOL_GROUP:], 2)
    sum8 = sum4[..., POOL_GROUP:] + shifted(sum4[..., POOL_GROUP:], 4)
    new_sums = [sum2[..., :POOL_GROUP], sum4[..., :POOL_GROUP], sum8[..., :POOL_GROUP],
                sum8[..., POOL_GROUP:]]

    zero_plane = jnp.zeros((bt, POOL_GROUP), F32)
    pooled = []
    for g, w in enumerate(POOL_WINDOWS):
        lanes = pl.ds(g * POOL_GROUP, POOL_GROUP)
        planes = [zero_plane] * t
        suffix = None
        for r in range(POOL_HIST - 1, -1, -1):
            token = r - (POOL_HIST + 1 - w)
            if token < 0:
                break
            row = hist_ref[r, :, lanes]
            suffix = row if suffix is None else suffix + row
            if token < t:
                planes[token] = suffix
        hist_sums = jnp.swapaxes(jnp.stack(planes, axis=0), 0, 1)
        wsum = (new_sums[g] + hist_sums).reshape(bt * t, POOL_GROUP)
        pooled.append(wsum * (1.0 / w) - u_ref[:, lanes])
    _mix_tail(pooled, a_ref[...], ga_ref[...], gp_ref[...], x_ref[...], wpool_ref, pscale_ref,
              wout_ref, gpost_ref, gpre_ref, x1_ref, h2_ref)
    keep = POOL_HIST - t
    pool_ref[0:keep] = hist_ref[POOL_HIST - keep:POOL_HIST]
    pool_ref[keep:POOL_HIST] = jnp.swapaxes(u3, 0, 1)


def _mix_sample(a2d, z2d, x2d, hist, wpool_bf, pscale, wout_bf, gpost, gpre, *, bt, t):
    m = x2d.shape[0]
    bm = bt * t
    row_spec = lambda col: pl.BlockSpec((bm, D_MODEL), lambda i: (i, col))
    return pl.pallas_call(
        functools.partial(_mix_sample_kernel, bt=bt, t=t, pos0=PAST_LEN),
        grid=(m // bm,),
        in_specs=[
            row_spec(0),
            row_spec(OFF_GA // D_MODEL),
            row_spec(OFF_GP // D_MODEL),
            pl.BlockSpec((bm, D_POOL), lambda i: (i, OFF_U // D_POOL)),
            pl.BlockSpec((POOL_HIST, bt, D_POOL), lambda i: (0, i, 0)),
            row_spec(0),
            _const_spec((len(POOL_WINDOWS), POOL_GROUP, POOL_OUT_GROUP)),
            _const_spec((1, D_MODEL)),
            _const_spec((D_MODEL, D_MODEL)),
            _const_spec((1, D_MODEL)),
            _const_spec((1, D_MODEL)),
        ],
        out_specs=[row_spec(0), row_spec(0),
                   pl.BlockSpec((POOL_HIST, bt, D_POOL), lambda i: (0, i, 0))],
        out_shape=[jax.ShapeDtypeStruct((m, D_MODEL), F32),
                   jax.ShapeDtypeStruct((m, D_MODEL), BF16),
                   jax.ShapeDtypeStruct((POOL_HIST, m // t, D_POOL), F32)],
        compiler_params=pltpu.CompilerParams(
            dimension_semantics=("parallel",), vmem_limit_bytes=VMEM_LIMIT),
        name="mix_sample",
    )(a2d, z2d, z2d, z2d, hist, x2d, wpool_bf, pscale, wout_bf, gpost, gpre)


MLP_TAIL_ROWS = 64


def _mlp_kernel(h2_ref, x1_slice_ref, wup_ref, wdn_ref, g_ref, y_ref, x1_ref, *, bm, slice_rows):
    j = pl.program_id(1)

    @pl.when(j == 0)
    def _():
        y_ref[...] = jnp.zeros_like(y_ref)

    x1_ref[pl.ds(pl.multiple_of(j * slice_rows, slice_rows), slice_rows), :] = x1_slice_ref[...]
    hid = jnp.dot(h2_ref[...], wup_ref[...], preferred_element_type=F32)
    hid = jnp.square(jnp.maximum(hid, 0.0)).astype(BF16)
    y_ref[...] += jnp.dot(hid, wdn_ref[...], preferred_element_type=F32)

    @pl.when(j == pl.num_programs(1) - 1)
    def _():
        def tail(c, carry):
            rows = pl.ds(pl.multiple_of(c * MLP_TAIL_ROWS, MLP_TAIL_ROWS), MLP_TAIL_ROWS)
            y_ref[rows, :] = x1_ref[rows, :] + _rmsnorm(y_ref[rows, :], g_ref[...])
            return carry

        lax.fori_loop(0, bm // MLP_TAIL_ROWS, tail, 0)


def _mlp_cast_kernel(h2_ref, x1_ref, wup_ref, wdn_ref, g_ref, y_ref, wup_bf_ref, wdn_bf_ref):
    j = pl.program_id(0)

    @pl.when(j == 0)
    def _():
        y_ref[...] = jnp.zeros_like(y_ref)

    wup = wup_ref[...].astype(BF16)
    wdn = wdn_ref[...].astype(BF16)
    wup_bf_ref[...] = wup
    wdn_bf_ref[...] = wdn
    hid = jnp.dot(h2_ref[...], wup, preferred_element_type=F32)
    hid = jnp.square(jnp.maximum(hid, 0.0)).astype(BF16)
    y_ref[...] += jnp.dot(hid, wdn, preferred_element_type=F32)

    @pl.when(j == pl.num_programs(0) - 1)
    def _():
        y_ref[...] = x1_ref[...] + _rmsnorm(y_ref[...], g_ref[...])


def _mlp_cast(h2, x1, wup_f32, wdn_f32, g, *, fc):
    m = x1.shape[0]
    return pl.pallas_call(
        _mlp_cast_kernel,
        grid=(D_FF // fc,),
        in_specs=[
            _const_spec((m, D_MODEL)),
            _const_spec((m, D_MODEL)),
            pl.BlockSpec((D_MODEL, fc), lambda j: (0, j)),
            pl.BlockSpec((fc, D_MODEL), lambda j: (j, 0)),
            _const_spec((1, D_MODEL)),
        ],
        out_specs=[pl.BlockSpec((m, D_MODEL), lambda j: (0, 0)),
                   pl.BlockSpec((D_MODEL, fc), lambda j: (0, j)),
                   pl.BlockSpec((fc, D_MODEL), lambda j: (j, 0))],
        out_shape=[jax.ShapeDtypeStruct((m, D_MODEL), F32),
                   jax.ShapeDtypeStruct((D_MODEL, D_FF), BF16),
                   jax.ShapeDtypeStruct((D_FF, D_MODEL), BF16)],
        compiler_params=pltpu.CompilerParams(
            dimension_semantics=("arbitrary",), vmem_limit_bytes=VMEM_LIMIT),
        name="mlp_cast",
    )(h2, x1, wup_f32, wdn_f32, g)


def _mlp(h2, x1, wup_bf, wdn_bf, g, *, bm, fc):
    m = x1.shape[0]
    n_ff = D_FF // fc
    slice_rows = bm // n_ff
    return pl.pallas_call(
        functools.partial(_mlp_kernel, bm=bm, slice_rows=slice_rows),
        grid=(m // bm, n_ff),
        in_specs=[
            pl.BlockSpec((bm, D_MODEL), lambda i, j: (i, 0)),
            pl.BlockSpec((slice_rows, D_MODEL), lambda i, j: (i * n_ff + j, 0)),
            pl.BlockSpec((D_MODEL, fc), lambda i, j: (0, j)),
            pl.BlockSpec((fc, D_MODEL), lambda i, j: (j, 0)),
            _const_spec((1, D_MODEL)),
        ],
        out_specs=pl.BlockSpec((bm, D_MODEL), lambda i, j: (i, 0)),
        out_shape=jax.ShapeDtypeStruct((m, D_MODEL), F32),
        scratch_shapes=[pltpu.VMEM((bm, D_MODEL), F32)],
        compiler_params=pltpu.CompilerParams(
            dimension_semantics=("parallel", "arbitrary"), vmem_limit_bytes=VMEM_LIMIT),
        name="mlp",
    )(h2, x1, wup_bf, wdn_bf, g)


def kernel(x_prompt, x_sample, cache_k_win, cache_v_win, state_pool, norm_attn_pre, norm_attn_post,
           w_in, attn_sinks, w_pool, pool_scale, w_out, norm_mlp_pre, norm_mlp_post, w_up, w_down):
    depth = w_in.shape[0]
    assert depth == 1
    b, s, _ = x_prompt.shape
    bs, t, _ = x_sample.shape

    l = 0
    w_pool_bf = w_pool[l].astype(BF16)
    w_out_bf = w_out[l].astype(BF16)
    row = lambda v: v[l].reshape(1, D_MODEL)
    g_attn_pre, g_attn_post = row(norm_attn_pre), row(norm_attn_post)
    g_mlp_pre, g_mlp_post = row(norm_mlp_pre), row(norm_mlp_post)
    pscale = row(pool_scale)
    sinks = attn_sinks[l]

    xs = x_sample.reshape(bs * t, D_MODEL)
    zs, w_in_bf = _in_proj_cast(xs, g_attn_pre, w_in[l])
    ck = cache_k_win[l].reshape(bs, WINDOW, D_KV)
    cv = cache_v_win[l].reshape(bs, WINDOW, D_KV)
    as_, ks, vs = _attn_sample(zs.reshape(bs, t, D_IN), ck, cv, sinks, bt=Tiles.ATTN_SAMPLE_SEQS)
    hist_rows = jnp.transpose(state_pool[l], (1, 0, 2))
    x1s, h2s, pools = _mix_sample(as_.reshape(bs * t, D_Q), zs, xs, hist_rows, w_pool_bf, pscale,
                                  w_out_bf, g_attn_post, g_mlp_pre, bt=Tiles.MIX_SAMPLE_SEQS, t=t)
    ys, w_up_bf, w_down_bf = _mlp_cast(h2s, x1s, w_up[l], w_down[l], g_mlp_post,
                                       fc=Tiles.MLP_CAST_FF_COLS)

    xp = x_prompt.reshape(b * s, D_MODEL)
    zp = _in_proj(xp, g_attn_pre, w_in_bf, bm=Tiles.IN_PROJ_ROWS, bn=Tiles.IN_PROJ_COLS)
    ap, kp, vp = _attn_prompt(zp.reshape(b, s, D_IN), sinks)
    x1p, h2p, poolp = _mix_prompt(ap.reshape(b * s, D_Q), zp, xp, w_pool_bf, pscale, w_out_bf,
                                  g_attn_post, g_mlp_pre, bm=Tiles.MIX_ROWS, seq=s)
    yp = _mlp(h2p, x1p, w_up_bf, w_down_bf, g_mlp_post, bm=Tiles.MLP_ROWS, fc=Tiles.MLP_FF_COLS)

    kv_shape = lambda nb: (1, nb, WINDOW, N_KV_HEADS, HEAD_DIM)
    return (yp.reshape(b, s, D_MODEL), ys.reshape(bs, t, D_MODEL),
            kp.reshape(kv_shape(b)), vp.reshape(kv_shape(b)), poolp[None],
            ks.reshape(kv_shape(bs)), vs.reshape(kv_shape(bs)),
            jnp.transpose(pools, (1, 0, 2))[None])
```

```python
import functools

import jax
import jax.numpy as jnp
from jax import lax
from jax.experimental import pallas as pl
from jax.experimental.pallas import tpu as pltpu

F32 = jnp.float32
BF16 = jnp.bfloat16

D_MODEL = 2048
HEAD_DIM = 64
N_KV_HEADS = 4
GQA_GROUP = 8
N_Q_HEADS = N_KV_HEADS * GQA_GROUP
WINDOW = 128
D_Q = 2048
D_KV = 256
POOL_WINDOWS = (2, 4, 8, 16)
D_POOL = 1024
POOL_GROUP = 256
POOL_OUT_GROUP = 512
POOL_HIST = 15
D_FF = 8192
D_IN = D_Q + 2 * D_KV + D_POOL + 2 * D_MODEL
PAST_LEN = 8192
EPS = 1e-6
NEG_INF = -1e30
SCALE = HEAD_DIM ** -0.5
LOG2E = 1.4426950408889634

LANES = 128
PAIR = 2
N_PAIRS = GQA_GROUP // PAIR
ROW_CHUNK = 16

OFF_Q, OFF_GA, OFF_GP, OFF_U, OFF_K, OFF_V = 0, 2048, 4096, 6144, 7168, 7424
HALO = 16
POOL_PAD = 8

VMEM_LIMIT = 60 * 1024 * 1024
MXU_TILE = 256


class Tiles:
    IN_PROJ_ROWS, IN_PROJ_COLS = 1024, 6 * MXU_TILE
    IN_PROJ_CAST_COLS = 2 * MXU_TILE
    MIX_ROWS = 256
    MLP_ROWS, MLP_FF_COLS = 1024, 4 * MXU_TILE
    MLP_CAST_FF_COLS = 2 * MXU_TILE
    ATTN_SAMPLE_SEQS = 16
    ATTN_SAMPLE_UNROLL = 8
    MIX_SAMPLE_SEQS = 32

NT_DIMS = (((1,), (1,)), ((), ()))
TN_DIMS = (((0,), (0,)), ((), ()))


def _rmsnorm(x, g):
    r = lax.rsqrt(jnp.mean(x * x, axis=-1, keepdims=True) + EPS)
    return x * r * g


def _const_spec(shape):
    return pl.BlockSpec(shape, lambda *_: (0,) * len(shape), pipeline_mode=pl.Buffered(1))


def _in_proj_kernel(x_ref, g_ref, w_ref, z_ref, h_ref):
    @pl.when(pl.program_id(1) == 0)
    def _():
        h_ref[...] = _rmsnorm(x_ref[...], g_ref[...]).astype(BF16)

    z_ref[...] = jnp.dot(h_ref[...], w_ref[...], preferred_element_type=F32)


def _in_proj_cast_kernel(x_ref, g_ref, w_ref, z_ref, wbf_ref, h_ref):
    @pl.when(pl.program_id(0) == 0)
    def _():
        h_ref[...] = _rmsnorm(x_ref[...], g_ref[...]).astype(BF16)

    w = w_ref[...].astype(BF16)
    wbf_ref[...] = w
    z_ref[...] = jnp.dot(h_ref[...], w, preferred_element_type=F32)


def _in_proj_cast(x2d, g, w_f32):
    m = x2d.shape[0]
    bn = Tiles.IN_PROJ_CAST_COLS
    src_bounds = (D_Q, D_Q + 2 * D_KV, D_Q + 2 * D_KV + D_POOL, D_Q + 2 * D_KV + D_POOL + D_MODEL)
    assert all(off % bn == 0 for off in src_bounds + (OFF_GA, OFF_GP, OFF_U, OFF_K))

    def reordered(j):
        q_end, kv_end, u_end, ga_end = (off // bn for off in src_bounds)
        return jnp.where(
            j < q_end, j, jnp.where(
                j < kv_end, OFF_K // bn + (j - q_end), jnp.where(
                    j < u_end, OFF_U // bn + (j - kv_end), jnp.where(
                        j < ga_end, OFF_GA // bn + (j - u_end), OFF_GP // bn + (j - ga_end)))))

    return pl.pallas_call(
        _in_proj_cast_kernel,
        grid=(D_IN // bn,),
        in_specs=[
            _const_spec((m, D_MODEL)),
            _const_spec((1, D_MODEL)),
            pl.BlockSpec((D_MODEL, bn), lambda j: (0, j)),
        ],
        out_specs=[pl.BlockSpec((m, bn), lambda j: (0, reordered(j))),
                   pl.BlockSpec((D_MODEL, bn), lambda j: (0, reordered(j)))],
        out_shape=[jax.ShapeDtypeStruct((m, D_IN), F32),
                   jax.ShapeDtypeStruct((D_MODEL, D_IN), BF16)],
        scratch_shapes=[pltpu.VMEM((m, D_MODEL), BF16)],
        compiler_params=pltpu.CompilerParams(
            dimension_semantics=("arbitrary",), vmem_limit_bytes=VMEM_LIMIT),
        name="in_proj_cast",
    )(x2d, g, w_f32)


def _in_proj(x2d, g, w_bf, *, bm, bn):
    m = x2d.shape[0]
    return pl.pallas_call(
        _in_proj_kernel,
        grid=(m // bm, D_IN // bn),
        in_specs=[
            pl.BlockSpec((bm, D_MODEL), lambda i, j: (i, 0)),
            _const_spec((1, D_MODEL)),
            pl.BlockSpec((D_MODEL, bn), lambda i, j: (0, j)),
        ],
        out_specs=pl.BlockSpec((bm, bn), lambda i, j: (i, j)),
        out_shape=jax.ShapeDtypeStruct((m, D_IN), F32),
        scratch_shapes=[pltpu.VMEM((bm, D_MODEL), BF16)],
        compiler_params=pltpu.CompilerParams(
            dimension_semantics=("parallel", "arbitrary"), vmem_limit_bytes=VMEM_LIMIT),
        name="in_proj",
    )(x2d, g, w_bf)


def _own_and_swapped(slab, odd, lo):
    own = jnp.where(lo != odd, slab, 0.0)
    swapped = pltpu.roll(own, HEAD_DIM, axis=1)
    return (swapped, own) if odd else (own, swapped)


ATTN_BLOCKS = 8
ATTN_BUFFERS = 8


def _attn_prompt_kernel(sinks_ref, q_ref, kp_ref, kc_ref, vp_ref, vc_ref,
                        a_ref, kwin_ref, vwin_ref, s_ref, p_ref, e_ref):
    n = pl.program_id(1)
    k_rows = jnp.concatenate([kp_ref[0], kc_ref[0]], axis=0)
    v_rows = jnp.concatenate([vp_ref[0], vc_ref[0]], axis=0)
    first_prev_bias = jnp.where(n > 0, 0.0, NEG_INF).astype(F32)
    lo_kv = lax.broadcasted_iota(jnp.int32, (2 * WINDOW, LANES), 1) < HEAD_DIM
    ones_lo = jnp.where(lo_kv, 1.0, 0.0)
    ones_hi = 1.0 - ones_lo
    lo_c = lax.broadcasted_iota(jnp.int32, (ROW_CHUNK, LANES), 1) < HEAD_DIM
    lane_c = lax.broadcasted_iota(jnp.int32, (ROW_CHUNK, LANES), 1)
    row_c = lax.broadcasted_iota(jnp.int32, (ROW_CHUNK, LANES), 0)

    for blk in range(ATTN_BLOCKS):
        k_all = k_rows[blk * WINDOW:(blk + 2) * WINDOW]
        v_all = v_rows[blk * WINDOW:(blk + 2) * WINDOW]
        q_rows = pl.ds(blk * WINDOW, WINDOW)
        for h in range(N_KV_HEADS):
            buf = (blk * N_KV_HEADS + h) % ATTN_BUFFERS
            odd = (h % PAIR) == 1
            kv_col = slice((h // PAIR) * LANES, (h // PAIR + 1) * LANES)
            k_l, k_r = _own_and_swapped(k_all[:, kv_col], odd, lo_kv)
            v_l, v_r = _own_and_swapped(v_all[:, kv_col], odd, lo_kv)
            wk = jnp.concatenate([k_l, k_r], axis=0).astype(BF16)
            vext = jnp.concatenate(
                [jnp.concatenate([v_l, ones_lo], axis=1),
                 jnp.concatenate([v_r, ones_hi], axis=1)], axis=0).astype(BF16)
            qh = jnp.concatenate(
                [q_ref[0, q_rows, pl.ds((h * N_PAIRS + j) * LANES, LANES)] for j in range(N_PAIRS)],
                axis=0)
            qh = (qh * (SCALE * LOG2E)).astype(BF16)
            s_ref[buf] = lax.dot_general(qh, wk, NT_DIMS, preferred_element_type=F32)

            for j in range(N_PAIRS):
                sinks = [sinks_ref[h, PAIR * j] * LOG2E, sinks_ref[h, PAIR * j + 1] * LOG2E]
                for c in range(WINDOW // ROW_CHUNK):
                    rows = pl.ds(j * WINDOW + c * ROW_CHUNK, ROW_CHUNK)
                    mask = lane_c <= (row_c + c * ROW_CHUNK)
                    e_parts = []
                    for gi in range(PAIR):
                        base = gi * 2 * WINDOW
                        cur = s_ref[buf, rows, pl.ds(base + WINDOW, WINDOW)]
                        prev = s_ref[buf, rows, pl.ds(base, WINDOW)]
                        if blk == 0:
                            prev = prev + first_prev_bias
                        s = jnp.where(mask, cur, prev)
                        m = jnp.maximum(jnp.max(s, axis=-1, keepdims=True), sinks[gi])
                        p = jnp.exp2(s - m)
                        p_ref[buf, rows, pl.ds(base, WINDOW)] = jnp.where(mask, 0.0, p).astype(BF16)
                        p_ref[buf, rows, pl.ds(base + WINDOW, WINDOW)] = (
                            jnp.where(mask, p, 0.0).astype(BF16))
                        e_parts.append(jnp.broadcast_to(jnp.exp2(sinks[gi] - m), (ROW_CHUNK, LANES)))
                    e_ref[buf, rows, :] = jnp.where(lo_c, e_parts[0], e_parts[1])

            o_ext = jnp.dot(p_ref[buf], vext, preferred_element_type=F32)
            o = o_ext[:, :LANES] / (o_ext[:, LANES:] + e_ref[buf])
            for j in range(N_PAIRS):
                a_ref[0, q_rows, pl.ds((h * N_PAIRS + j) * LANES, LANES)] = o[j * WINDOW:(j + 1) * WINDOW]

    @pl.when(n == pl.num_programs(1) - 1)
    def _():
        kwin_ref[0] = kc_ref[0, pl.ds((ATTN_BLOCKS - 1) * WINDOW, WINDOW), :]
        vwin_ref[0] = vc_ref[0, pl.ds((ATTN_BLOCKS - 1) * WINDOW, WINDOW), :]


def _attn_prompt(z3, sinks):
    b, s, _ = z3.shape
    step_rows = ATTN_BLOCKS * WINDOW
    kcol, vcol = OFF_K // D_KV, OFF_V // D_KV
    prev = lambda col: (lambda bi, n: (bi, jnp.maximum(n * ATTN_BLOCKS - 1, 0), col))
    cur = lambda col: (lambda bi, n: (bi, n, col))
    win_spec = pl.BlockSpec((1, WINDOW, D_KV), lambda bi, n: (bi, 0, 0))
    rows = N_PAIRS * WINDOW
    n_buf = ATTN_BUFFERS
    return pl.pallas_call(
        _attn_prompt_kernel,
        grid=(b, s // step_rows),
        in_specs=[
            pl.BlockSpec(memory_space=pltpu.SMEM),
            pl.BlockSpec((1, step_rows, D_Q), lambda bi, n: (bi, n, 0)),
            pl.BlockSpec((1, WINDOW, D_KV), prev(kcol)),
            pl.BlockSpec((1, step_rows, D_KV), cur(kcol)),
            pl.BlockSpec((1, WINDOW, D_KV), prev(vcol)),
            pl.BlockSpec((1, step_rows, D_KV), cur(vcol)),
        ],
        out_specs=[pl.BlockSpec((1, step_rows, D_Q), lambda bi, n: (bi, n, 0)), win_spec, win_spec],
        out_shape=[jax.ShapeDtypeStruct((b, s, D_Q), F32),
                   jax.ShapeDtypeStruct((b, WINDOW, D_KV), F32),
                   jax.ShapeDtypeStruct((b, WINDOW, D_KV), F32)],
        scratch_shapes=[pltpu.VMEM((n_buf, rows, PAIR * 2 * WINDOW), F32),
                        pltpu.VMEM((n_buf, rows, PAIR * 2 * WINDOW), BF16),
                        pltpu.VMEM((n_buf, rows, LANES), F32)],
        compiler_params=pltpu.CompilerParams(
            dimension_semantics=("parallel", "arbitrary"), vmem_limit_bytes=VMEM_LIMIT),
        name="attn_prompt",
    )(sinks, z3, z3, z3, z3, z3)


def _attn_sample_kernel(sinkrow_ref, q_ref, kn_ref, vn_ref, ck_ref, cv_ref,
                        a_ref, kwin_ref, vwin_ref, *, bt, t):
    pad_rows = 2 * WINDOW - WINDOW - t
    pad = jnp.zeros((pad_rows, D_KV), F32)
    p_pad = jnp.zeros((pad_rows, N_Q_HEADS * t), F32)
    zeros_col = jnp.zeros((t, LANES), F32)
    lo = lax.broadcasted_iota(jnp.int32, (t, LANES), 1) < HEAD_DIM
    tok = lax.broadcasted_iota(jnp.int32, (t, N_Q_HEADS * t), 1) % t
    key = lax.broadcasted_iota(jnp.int32, (t, N_Q_HEADS * t), 0)
    new_mask = key <= tok
    sinkrow = sinkrow_ref[...]

    def place(piece_col, src_odd, dst_odd):
        own = jnp.where(lo != src_odd, piece_col, 0.0)
        return own if src_odd == dst_odd else pltpu.roll(own, HEAD_DIM, axis=1)

    def body(s, carry):
        kn, vn = kn_ref[s], vn_ref[s]
        ck, cv = ck_ref[s].T, cv_ref[s].T
        k_all = jnp.concatenate([ck, kn, pad], axis=0).astype(BF16)
        v_all = jnp.concatenate([cv, vn, pad], axis=0).astype(BF16)
        q = q_ref[s] * SCALE
        blocks = []
        for h in range(N_KV_HEADS):
            for g in range(GQA_GROUP):
                head = h * GQA_GROUP + g
                piece = place(q[:, (head // PAIR) * LANES:(head // PAIR + 1) * LANES],
                              head % PAIR == 1, h % PAIR == 1)
                cols = [zeros_col] * (D_KV // LANES)
                cols[h // PAIR] = piece
                blocks.append(jnp.concatenate(cols, axis=1))
        wq_t = jnp.concatenate(blocks, axis=0).astype(BF16)
        s_t = lax.dot_general(k_all, wq_t, NT_DIMS, preferred_element_type=F32)
        top = jnp.where(new_mask, s_t[WINDOW:WINDOW + t], s_t[0:t])
        s_m = jnp.concatenate([top, s_t[t:WINDOW]], axis=0)
        m = jnp.maximum(jnp.max(s_m, axis=0, keepdims=True), sinkrow)
        p = jnp.exp(s_m - m)
        denom = jnp.sum(p, axis=0, keepdims=True) + jnp.exp(sinkrow - m)
        probs = p * (1.0 / denom)
        p_all = jnp.concatenate(
            [jnp.where(new_mask, 0.0, probs[0:t]), probs[t:WINDOW],
             jnp.where(new_mask, probs[0:t], 0.0), p_pad], axis=0).astype(BF16)
        o_full = lax.dot_general(p_all, v_all, TN_DIMS, preferred_element_type=F32)
        out_cols = []
        for c in range(N_Q_HEADS // PAIR):
            h = (PAIR * c) // GQA_GROUP
            kv_col = slice((h // PAIR) * LANES, (h // PAIR + 1) * LANES)
            even = place(o_full[(PAIR * c) * t:(PAIR * c + 1) * t, kv_col], h % PAIR == 1, False)
            odd = place(o_full[(PAIR * c + 1) * t:(PAIR * c + 2) * t, kv_col], h % PAIR == 1, True)
            out_cols.append(even + odd)
        a_ref[s] = jnp.concatenate(out_cols, axis=1)
        kwin_ref[s, 0:WINDOW - t, :] = ck[t:, :]
        kwin_ref[s, WINDOW - t:WINDOW, :] = kn
        vwin_ref[s, 0:WINDOW - t, :] = cv[t:, :]
        vwin_ref[s, WINDOW - t:WINDOW, :] = vn
        return carry

    lax.fori_loop(0, bt, body, 0, unroll=Tiles.ATTN_SAMPLE_UNROLL)


def _attn_sample(z3, cache_k, cache_v, sinks, *, bt):
    b, t, _ = z3.shape
    kcol, vcol = OFF_K // D_KV, OFF_V // D_KV
    cache_spec = pl.BlockSpec((bt, WINDOW, D_KV), lambda i: (i, 0, 0))
    cache_t_spec = pl.BlockSpec((bt, D_KV, WINDOW), lambda i: (i, 0, 0))
    sinkrow = jnp.repeat(sinks.reshape(1, N_Q_HEADS), t, axis=1)
    return pl.pallas_call(
        functools.partial(_attn_sample_kernel, bt=bt, t=t),
        grid=(b // bt,),
        in_specs=[
            _const_spec((1, N_Q_HEADS * t)),
            pl.BlockSpec((bt, t, D_Q), lambda i: (i, 0, 0)),
            pl.BlockSpec((bt, t, D_KV), lambda i: (i, 0, kcol)),
            pl.BlockSpec((bt, t, D_KV), lambda i: (i, 0, vcol)),
            cache_t_spec, cache_t_spec,
        ],
        out_specs=[pl.BlockSpec((bt, t, D_Q), lambda i: (i, 0, 0)), cache_spec, cache_spec],
        out_shape=[jax.ShapeDtypeStruct((b, t, D_Q), F32),
                   jax.ShapeDtypeStruct((b, WINDOW, D_KV), F32),
                   jax.ShapeDtypeStruct((b, WINDOW, D_KV), F32)],
        compiler_params=pltpu.CompilerParams(
            dimension_semantics=("parallel",), vmem_limit_bytes=VMEM_LIMIT),
        name="attn_sample",
    )(sinkrow, z3, z3, z3, cache_k, cache_v)


def _mix_tail(pooled_parts, a, ga, gp, x, wpool_ref, pscale_ref, wout_ref, gpost_ref, gpre_ref,
              x1_ref, h2_ref):
    parts = [jnp.dot(pooled_parts[g].astype(BF16), wpool_ref[g], preferred_element_type=F32)
             for g in range(len(POOL_WINDOWS))]
    p = jnp.concatenate(parts, axis=-1) * pscale_ref[...]
    mixed = jax.nn.sigmoid(ga) * a + jax.nn.sigmoid(gp) * p
    y = jnp.dot(mixed.astype(BF16), wout_ref[...], preferred_element_type=F32)
    x1 = x + _rmsnorm(y, gpost_ref[...])
    x1_ref[...] = x1
    h2_ref[...] = _rmsnorm(x1, gpre_ref[...]).astype(BF16)


def _mix_prompt_kernel(a_ref, ga_ref, gp_ref, u_ref, halo_ref, x_ref, wpool_ref, pscale_ref,
                       wout_ref, gpost_ref, gpre_ref, x1_ref, h2_ref, pool_ref,
                       ext_ref, sum2_ref, sum4_ref, sum8_ref, *, bm, blocks_per_seq):
    assert POOL_WINDOWS == (2, 4, 8, 16)
    blk = pl.program_id(0) % blocks_per_seq
    pad = jnp.zeros((POOL_PAD, D_POOL), F32)
    ext_ref[0:POOL_PAD, :] = pad
    sum2_ref[0:POOL_PAD, :] = pad
    sum4_ref[0:POOL_PAD, :] = pad[:, POOL_GROUP:]
    ext_ref[POOL_PAD:POOL_PAD + HALO, :] = jnp.where(blk == 0, 0.0, halo_ref[...])
    ext_ref[POOL_PAD + HALO:, :] = u_ref[...]
    n = HALO + bm
    body = pl.ds(POOL_PAD, n)
    shifted = lambda s: pl.ds(POOL_PAD - s, n)
    g1 = pl.ds(POOL_GROUP, D_POOL - POOL_GROUP)
    sum2_ref[body, :] = ext_ref[body, :] + ext_ref[shifted(1), :]
    sum4_ref[body, :] = sum2_ref[body, g1] + sum2_ref[shifted(2), g1]
    sum8_ref[body, :] = sum4_ref[body, POOL_GROUP:] + sum4_ref[shifted(4), POOL_GROUP:]
    first = POOL_PAD + HALO
    rows = pl.ds(first, bm)
    sum16 = sum8_ref[rows, POOL_GROUP:] + sum8_ref[pl.ds(first - 8, bm), POOL_GROUP:]
    wsums = [sum2_ref[rows, 0:POOL_GROUP], sum4_ref[rows, 0:POOL_GROUP],
             sum8_ref[rows, 0:POOL_GROUP], sum16]
    pos = blk * bm + lax.broadcasted_iota(jnp.int32, (bm, 1), 0)
    pooled = []
    for g, w in enumerate(POOL_WINDOWS):
        inv_cnt = 1.0 / jnp.minimum(pos + 1, w).astype(F32)
        pooled.append(wsums[g] * inv_cnt - u_ref[:, pl.ds(g * POOL_GROUP, POOL_GROUP)])
    _mix_tail(pooled, a_ref[...], ga_ref[...], gp_ref[...], x_ref[...], wpool_ref, pscale_ref,
              wout_ref, gpost_ref, gpre_ref, x1_ref, h2_ref)

    @pl.when(blk == blocks_per_seq - 1)
    def _():
        pool_ref[0] = ext_ref[pl.ds(first + bm - POOL_HIST, POOL_HIST), :]


def _mix_prompt(a2d, z2d, x2d, wpool_bf, pscale, wout_bf, gpost, gpre, *, bm, seq):
    m = x2d.shape[0]
    blocks_per_seq = seq // bm
    row_spec = lambda col: pl.BlockSpec((bm, D_MODEL), lambda i: (i, col))
    halo_blocks = bm // HALO
    return pl.pallas_call(
        functools.partial(_mix_prompt_kernel, bm=bm, blocks_per_seq=blocks_per_seq),
        grid=(m // bm,),
        in_specs=[
            row_spec(0),
            row_spec(OFF_GA // D_MODEL),
            row_spec(OFF_GP // D_MODEL),
            pl.BlockSpec((bm, D_POOL), lambda i: (i, OFF_U // D_POOL)),
            pl.BlockSpec((HALO, D_POOL),
                         lambda i: (jnp.maximum(i * halo_blocks - 1, 0), OFF_U // D_POOL)),
            row_spec(0),
            _const_spec((len(POOL_WINDOWS), POOL_GROUP, POOL_OUT_GROUP)),
            _const_spec((1, D_MODEL)),
            _const_spec((D_MODEL, D_MODEL)),
            _const_spec((1, D_MODEL)),
            _const_spec((1, D_MODEL)),
        ],
        out_specs=[row_spec(0), row_spec(0),
                   pl.BlockSpec((1, POOL_HIST, D_POOL), lambda i: (i // blocks_per_seq, 0, 0))],
        out_shape=[jax.ShapeDtypeStruct((m, D_MODEL), F32),
                   jax.ShapeDtypeStruct((m, D_MODEL), BF16),
                   jax.ShapeDtypeStruct((m // seq, POOL_HIST, D_POOL), F32)],
        scratch_shapes=[pltpu.VMEM((POOL_PAD + HALO + bm, D_POOL), F32),
                        pltpu.VMEM((POOL_PAD + HALO + bm, D_POOL), F32),
                        pltpu.VMEM((POOL_PAD + HALO + bm, D_POOL - POOL_GROUP), F32),
                        pltpu.VMEM((POOL_PAD + HALO + bm, D_POOL - 2 * POOL_GROUP), F32)],
        compiler_params=pltpu.CompilerParams(
            dimension_semantics=("arbitrary",), vmem_limit_bytes=VMEM_LIMIT),
        name="mix_prompt",
    )(a2d, z2d, z2d, z2d, z2d, x2d, wpool_bf, pscale, wout_bf, gpost, gpre)


def _mix_sample_kernel(a_ref, ga_ref, gp_ref, u_ref, hist_ref, x_ref, wpool_ref, pscale_ref,
                       wout_ref, gpost_ref, gpre_ref, x1_ref, h2_ref, pool_ref, *, bt, t, pos0):
    assert t == 8 and max(POOL_WINDOWS) >= t and pos0 + 1 >= max(POOL_WINDOWS)
    u3 = u_ref[...].reshape(bt, t, D_POOL)
    def shifted(x, k):
        tok = lax.broadcasted_iota(jnp.int32, x.shape, 1)
        return jnp.where(tok >= k, pltpu.roll(x, k, axis=1), 0.0)

    sum2 = u3 + shifted(u3, 1)
    sum4 = sum2[..., POOL_GROUP:] + shifted(sum2[..., POOL_GROUP:], 2)
    sum8 = sum4[..., POOL_GROUP:] + shifted(sum4[..., POOL_GROUP:], 4)
    new_sums = [sum2[..., :POOL_GROUP], sum4[..., :POOL_GROUP], sum8[..., :POOL_GROUP],
                sum8[..., POOL_GROUP:]]

    zero_plane = jnp.zeros((bt, POOL_GROUP), F32)
    pooled = []
    for g, w in enumerate(POOL_WINDOWS):
        lanes = pl.ds(g * POOL_GROUP, POOL_GROUP)
        planes = [zero_plane] * t
        suffix = None
        for r in range(POOL_HIST - 1, -1, -1):
            token = r - (POOL_HIST + 1 - w)
            if token < 0:
                break
            row = hist_ref[r, :, lanes]
            suffix = row if suffix is None else suffix + row
            if token < t:
                planes[token] = suffix
        hist_sums = jnp.swapaxes(jnp.stack(planes, axis=0), 0, 1)
        wsum = (new_sums[g] + hist_sums).reshape(bt * t, POOL_GROUP)
        pooled.append(wsum * (1.0 / w) - u_ref[:, lanes])
    _mix_tail(pooled, a_ref[...], ga_ref[...], gp_ref[...], x_ref[...], wpool_ref, pscale_ref,
              wout_ref, gpost_ref, gpre_ref, x1_ref, h2_ref)
    keep = POOL_HIST - t
    pool_ref[0:keep] = hist_ref[POOL_HIST - keep:POOL_HIST]
    pool_ref[keep:POOL_HIST] = jnp.swapaxes(u3, 0, 1)


def _mix_sample(a2d, z2d, x2d, hist, wpool_bf, pscale, wout_bf, gpost, gpre, *, bt, t):
    m = x2d.shape[0]
    bm = bt * t
    row_spec = lambda col: pl.BlockSpec((bm, D_MODEL), lambda i: (i, col))
    return pl.pallas_call(
        functools.partial(_mix_sample_kernel, bt=bt, t=t, pos0=PAST_LEN),
        grid=(m // bm,),
        in_specs=[
            row_spec(0),
            row_spec(OFF_GA // D_MODEL),
            row_spec(OFF_GP // D_MODEL),
            pl.BlockSpec((bm, D_POOL), lambda i: (i, OFF_U // D_POOL)),
            pl.BlockSpec((POOL_HIST, bt, D_POOL), lambda i: (0, i, 0)),
            row_spec(0),
            _const_spec((len(POOL_WINDOWS), POOL_GROUP, POOL_OUT_GROUP)),
            _const_spec((1, D_MODEL)),
            _const_spec((D_MODEL, D_MODEL)),
            _const_spec((1, D_MODEL)),
            _const_spec((1, D_MODEL)),
        ],
        out_specs=[row_spec(0), row_spec(0),
                   pl.BlockSpec((POOL_HIST, bt, D_POOL), lambda i: (0, i, 0))],
        out_shape=[jax.ShapeDtypeStruct((m, D_MODEL), F32),
                   jax.ShapeDtypeStruct((m, D_MODEL), BF16),
                   jax.ShapeDtypeStruct((POOL_HIST, m // t, D_POOL), F32)],
        compiler_params=pltpu.CompilerParams(
            dimension_semantics=("parallel",), vmem_limit_bytes=VMEM_LIMIT),
        name="mix_sample",
    )(a2d, z2d, z2d, z2d, hist, x2d, wpool_bf, pscale, wout_bf, gpost, gpre)


MLP_TAIL_ROWS = 64


def _mlp_kernel(h2_ref, x1_slice_ref, wup_ref, wdn_ref, g_ref, y_ref, x1_ref, *, bm, slice_rows):
    j = pl.program_id(1)

    @pl.when(j == 0)
    def _():
        y_ref[...] = jnp.zeros_like(y_ref)

    x1_ref[pl.ds(pl.multiple_of(j * slice_rows, slice_rows), slice_rows), :] = x1_slice_ref[...]
    hid = jnp.dot(h2_ref[...], wup_ref[...], preferred_element_type=F32)
    hid = jnp.square(jnp.maximum(hid, 0.0)).astype(BF16)
    y_ref[...] += jnp.dot(hid, wdn_ref[...], preferred_element_type=F32)

    @pl.when(j == pl.num_programs(1) - 1)
    def _():
        def tail(c, carry):
            rows = pl.ds(pl.multiple_of(c * MLP_TAIL_ROWS, MLP_TAIL_ROWS), MLP_TAIL_ROWS)
            y_ref[rows, :] = x1_ref[rows, :] + _rmsnorm(y_ref[rows, :], g_ref[...])
            return carry

        lax.fori_loop(0, bm // MLP_TAIL_ROWS, tail, 0)


def _mlp_cast_kernel(h2_ref, x1_ref, wup_ref, wdn_ref, g_ref, y_ref, wup_bf_ref, wdn_bf_ref):
    j = pl.program_id(0)

    @pl.when(j == 0)
    def _():
        y_ref[...] = jnp.zeros_like(y_ref)

    wup = wup_ref[...].astype(BF16)
    wdn = wdn_ref[...].astype(BF16)
    wup_bf_ref[...] = wup
    wdn_bf_ref[...] = wdn
    hid = jnp.dot(h2_ref[...], wup, preferred_element_type=F32)
    hid = jnp.square(jnp.maximum(hid, 0.0)).astype(BF16)
    y_ref[...] += jnp.dot(hid, wdn, preferred_element_type=F32)

    @pl.when(j == pl.num_programs(0) - 1)
    def _():
        y_ref[...] = x1_ref[...] + _rmsnorm(y_ref[...], g_ref[...])


def _mlp_cast(h2, x1, wup_f32, wdn_f32, g, *, fc):
    m = x1.shape[0]
    return pl.pallas_call(
        _mlp_cast_kernel,
        grid=(D_FF // fc,),
        in_specs=[
            _const_spec((m, D_MODEL)),
            _const_spec((m, D_MODEL)),
            pl.BlockSpec((D_MODEL, fc), lambda j: (0, j)),
            pl.BlockSpec((fc, D_MODEL), lambda j: (j, 0)),
            _const_spec((1, D_MODEL)),
        ],
        out_specs=[pl.BlockSpec((m, D_MODEL), lambda j: (0, 0)),
                   pl.BlockSpec((D_MODEL, fc), lambda j: (0, j)),
                   pl.BlockSpec((fc, D_MODEL), lambda j: (j, 0))],
        out_shape=[jax.ShapeDtypeStruct((m, D_MODEL), F32),
                   jax.ShapeDtypeStruct((D_MODEL, D_FF), BF16),
                   jax.ShapeDtypeStruct((D_FF, D_MODEL), BF16)],
        compiler_params=pltpu.CompilerParams(
            dimension_semantics=("arbitrary",), vmem_limit_bytes=VMEM_LIMIT),
        name="mlp_cast",
    )(h2, x1, wup_f32, wdn_f32, g)


def _mlp(h2, x1, wup_bf, wdn_bf, g, *, bm, fc):
    m = x1.shape[0]
    n_ff = D_FF // fc
    slice_rows = bm // n_ff
    return pl.pallas_call(
        functools.partial(_mlp_kernel, bm=bm, slice_rows=slice_rows),
        grid=(m // bm, n_ff),
        in_specs=[
            pl.BlockSpec((bm, D_MODEL), lambda i, j: (i, 0)),
            pl.BlockSpec((slice_rows, D_MODEL), lambda i, j: (i * n_ff + j, 0)),
            pl.BlockSpec((D_MODEL, fc), lambda i, j: (0, j)),
            pl.BlockSpec((fc, D_MODEL), lambda i, j: (j, 0)),
            _const_spec((1, D_MODEL)),
        ],
        out_specs=pl.BlockSpec((bm, D_MODEL), lambda i, j: (i, 0)),
        out_shape=jax.ShapeDtypeStruct((m, D_MODEL), F32),
        scratch_shapes=[pltpu.VMEM((bm, D_MODEL), F32)],
        compiler_params=pltpu.CompilerParams(
            dimension_semantics=("parallel", "arbitrary"), vmem_limit_bytes=VMEM_LIMIT),
        name="mlp",
    )(h2, x1, wup_bf, wdn_bf, g)


def kernel(x_prompt, x_sample, cache_k_win, cache_v_win, state_pool, norm_attn_pre, norm_attn_post,
           w_in, attn_sinks, w_pool, pool_scale, w_out, norm_mlp_pre, norm_mlp_post, w_up, w_down):
    depth = w_in.shape[0]
    assert depth == 1
    b, s, _ = x_prompt.shape
    bs, t, _ = x_sample.shape

    l = 0
    w_pool_bf = w_pool[l].astype(BF16)
    w_out_bf = w_out[l].astype(BF16)
    row = lambda v: v[l].reshape(1, D_MODEL)
    g_attn_pre, g_attn_post = row(norm_attn_pre), row(norm_attn_post)
    g_mlp_pre, g_mlp_post = row(norm_mlp_pre), row(norm_mlp_post)
    pscale = row(pool_scale)
    sinks = attn_sinks[l]

    xs = x_sample.reshape(bs * t, D_MODEL)
    zs, w_in_bf = _in_proj_cast(xs, g_attn_pre, w_in[l])
    stored_order = lambda c: jnp.transpose(c, (0, 2, 3, 1)).reshape(bs, D_KV, WINDOW)
    as_, ks, vs = _attn_sample(zs.reshape(bs, t, D_IN), stored_order(cache_k_win[l]),
                               stored_order(cache_v_win[l]), sinks, bt=Tiles.ATTN_SAMPLE_SEQS)
    hist_rows = jnp.transpose(state_pool[l], (1, 0, 2))
    x1s, h2s, pools = _mix_sample(as_.reshape(bs * t, D_Q), zs, xs, hist_rows, w_pool_bf, pscale,
                                  w_out_bf, g_attn_post, g_mlp_pre, bt=Tiles.MIX_SAMPLE_SEQS, t=t)
    ys, w_up_bf, w_down_bf = _mlp_cast(h2s, x1s, w_up[l], w_down[l], g_mlp_post,
                                       fc=Tiles.MLP_CAST_FF_COLS)

    xp = x_prompt.reshape(b * s, D_MODEL)
    zp = _in_proj(xp, g_attn_pre, w_in_bf, bm=Tiles.IN_PROJ_ROWS, bn=Tiles.IN_PROJ_COLS)
    ap, kp, vp = _attn_prompt(zp.reshape(b, s, D_IN), sinks)
    x1p, h2p, poolp = _mix_prompt(ap.reshape(b * s, D_Q), zp, xp, w_pool_bf, pscale, w_out_bf,
                                  g_attn_post, g_mlp_pre, bm=Tiles.MIX_ROWS, seq=s)
    yp = _mlp(h2p, x1p, w_up_bf, w_down_bf, g_mlp_post, bm=Tiles.MLP_ROWS, fc=Tiles.MLP_FF_COLS)

    kv_shape = lambda nb: (1, nb, WINDOW, N_KV_HEADS, HEAD_DIM)
    return (yp.reshape(b, s, D_MODEL), ys.reshape(bs, t, D_MODEL),
            kp.reshape(kv_shape(b)), vp.reshape(kv_shape(b)), poolp[None],
            ks.reshape(kv_shape(bs)), vs.reshape(kv_shape(bs)),
            jnp.transpose(pools, (1, 0, 2))[None])
```

```python
import functools

import jax
import jax.numpy as jnp
from jax import lax
from jax.experimental import pallas as pl
from jax.experimental.pallas import tpu as pltpu

F32 = jnp.float32
BF16 = jnp.bfloat16

D_MODEL = 2048
HEAD_DIM = 64
N_KV_HEADS = 4
GQA_GROUP = 8
N_Q_HEADS = N_KV_HEADS * GQA_GROUP
WINDOW = 128
D_Q = 2048
D_KV = 256
POOL_WINDOWS = (2, 4, 8, 16)
D_POOL = 1024
POOL_GROUP = 256
POOL_OUT_GROUP = 512
POOL_HIST = 15
D_FF = 8192
D_IN = D_Q + 2 * D_KV + D_POOL + 2 * D_MODEL
PAST_LEN = 8192
EPS = 1e-6
NEG_INF = -1e30
SCALE = HEAD_DIM ** -0.5
LOG2E = 1.4426950408889634

LANES = 128
PAIR = 2
N_PAIRS = GQA_GROUP // PAIR
ROW_CHUNK = 16

OFF_Q, OFF_GA, OFF_GP, OFF_U, OFF_K, OFF_V = 0, 2048, 4096, 6144, 7168, 7424
HALO = 16
POOL_PAD = 8

VMEM_LIMIT = 60 * 1024 * 1024
MXU_TILE = 256


class Tiles:
    IN_PROJ_ROWS, IN_PROJ_COLS = 1024, 6 * MXU_TILE
    IN_PROJ_CAST_COLS = 2 * MXU_TILE
    MIX_ROWS = 256
    MLP_ROWS, MLP_FF_COLS = 1024, 4 * MXU_TILE
    MLP_CAST_FF_COLS = 2 * MXU_TILE
    ATTN_SAMPLE_SEQS = 16
    ATTN_SAMPLE_UNROLL = 8
    MIX_SAMPLE_SEQS = 32

NT_DIMS = (((1,), (1,)), ((), ()))
TN_DIMS = (((0,), (0,)), ((), ()))


def _rmsnorm(x, g):
    r = lax.rsqrt(jnp.mean(x * x, axis=-1, keepdims=True) + EPS)
    return x * r * g


def _const_spec(shape):
    return pl.BlockSpec(shape, lambda *_: (0,) * len(shape), pipeline_mode=pl.Buffered(1))


def _in_proj_kernel(x_ref, g_ref, w_ref, z_ref, h_ref):
    @pl.when(pl.program_id(1) == 0)
    def _():
        h_ref[...] = _rmsnorm(x_ref[...], g_ref[...]).astype(BF16)

    z_ref[...] = jnp.dot(h_ref[...], w_ref[...], preferred_element_type=F32)


def _in_proj_cast_kernel(x_ref, g_ref, w_ref, z_ref, wbf_ref, h_ref):
    @pl.when(pl.program_id(0) == 0)
    def _():
        h_ref[...] = _rmsnorm(x_ref[...], g_ref[...]).astype(BF16)

    w = w_ref[...].astype(BF16)
    wbf_ref[...] = w
    z_ref[...] = jnp.dot(h_ref[...], w, preferred_element_type=F32)


def _in_proj_cast(x2d, g, w_f32):
    m = x2d.shape[0]
    bn = Tiles.IN_PROJ_CAST_COLS
    src_bounds = (D_Q, D_Q + 2 * D_KV, D_Q + 2 * D_KV + D_POOL, D_Q + 2 * D_KV + D_POOL + D_MODEL)
    assert all(off % bn == 0 for off in src_bounds + (OFF_GA, OFF_GP, OFF_U, OFF_K))

    def reordered(j):
        q_end, kv_end, u_end, ga_end = (off // bn for off in src_bounds)
        return jnp.where(
            j < q_end, j, jnp.where(
                j < kv_end, OFF_K // bn + (j - q_end), jnp.where(
                    j < u_end, OFF_U // bn + (j - kv_end), jnp.where(
                        j < ga_end, OFF_GA // bn + (j - u_end), OFF_GP // bn + (j - ga_end)))))

    return pl.pallas_call(
        _in_proj_cast_kernel,
        grid=(D_IN // bn,),
        in_specs=[
            _const_spec((m, D_MODEL)),
            _const_spec((1, D_MODEL)),
            pl.BlockSpec((D_MODEL, bn), lambda j: (0, j)),
        ],
        out_specs=[pl.BlockSpec((m, bn), lambda j: (0, reordered(j))),
                   pl.BlockSpec((D_MODEL, bn), lambda j: (0, reordered(j)))],
        out_shape=[jax.ShapeDtypeStruct((m, D_IN), F32),
                   jax.ShapeDtypeStruct((D_MODEL, D_IN), BF16)],
        scratch_shapes=[pltpu.VMEM((m, D_MODEL), BF16)],
        compiler_params=pltpu.CompilerParams(
            dimension_semantics=("arbitrary",), vmem_limit_bytes=VMEM_LIMIT),
        name="in_proj_cast",
    )(x2d, g, w_f32)


def _in_proj(x2d, g, w_bf, *, bm, bn):
    m = x2d.shape[0]
    return pl.pallas_call(
        _in_proj_kernel,
        grid=(m // bm, D_IN // bn),
        in_specs=[
            pl.BlockSpec((bm, D_MODEL), lambda i, j: (i, 0)),
            _const_spec((1, D_MODEL)),
            pl.BlockSpec((D_MODEL, bn), lambda i, j: (0, j)),
        ],
        out_specs=pl.BlockSpec((bm, bn), lambda i, j: (i, j)),
        out_shape=jax.ShapeDtypeStruct((m, D_IN), F32),
        scratch_shapes=[pltpu.VMEM((bm, D_MODEL), BF16)],
        compiler_params=pltpu.CompilerParams(
            dimension_semantics=("parallel", "arbitrary"), vmem_limit_bytes=VMEM_LIMIT),
        name="in_proj",
    )(x2d, g, w_bf)


def _own_and_swapped(slab, odd, lo):
    own = jnp.where(lo != odd, slab, 0.0)
    swapped = pltpu.roll(own, HEAD_DIM, axis=1)
    return (swapped, own) if odd else (own, swapped)


ATTN_BLOCKS = 8
ATTN_BUFFERS = 8


def _attn_prompt_kernel(sinks_ref, q_ref, kp_ref, kc_ref, vp_ref, vc_ref,
                        a_ref, kwin_ref, vwin_ref, s_ref, p_ref, e_ref):
    n = pl.program_id(1)
    k_rows = jnp.concatenate([kp_ref[0], kc_ref[0]], axis=0)
    v_rows = jnp.concatenate([vp_ref[0], vc_ref[0]], axis=0)
    first_prev_bias = jnp.where(n > 0, 0.0, NEG_INF).astype(F32)
    lo_kv = lax.broadcasted_iota(jnp.int32, (2 * WINDOW, LANES), 1) < HEAD_DIM
    ones_lo = jnp.where(lo_kv, 1.0, 0.0)
    ones_hi = 1.0 - ones_lo
    lo_c = lax.broadcasted_iota(jnp.int32, (ROW_CHUNK, LANES), 1) < HEAD_DIM
    lane_c = lax.broadcasted_iota(jnp.int32, (ROW_CHUNK, LANES), 1)
    row_c = lax.broadcasted_iota(jnp.int32, (ROW_CHUNK, LANES), 0)

    for blk in range(ATTN_BLOCKS):
        k_all = k_rows[blk * WINDOW:(blk + 2) * WINDOW]
        v_all = v_rows[blk * WINDOW:(blk + 2) * WINDOW]
        q_rows = pl.ds(blk * WINDOW, WINDOW)
        for h in range(N_KV_HEADS):
            buf = (blk * N_KV_HEADS + h) % ATTN_BUFFERS
            odd = (h % PAIR) == 1
            kv_col = slice((h // PAIR) * LANES, (h // PAIR + 1) * LANES)
            k_l, k_r = _own_and_swapped(k_all[:, kv_col], odd, lo_kv)
            v_l, v_r = _own_and_swapped(v_all[:, kv_col], odd, lo_kv)
            wk = jnp.concatenate([k_l, k_r], axis=0).astype(BF16)
            vext = jnp.concatenate(
                [jnp.concatenate([v_l, ones_lo], axis=1),
                 jnp.concatenate([v_r, ones_hi], axis=1)], axis=0).astype(BF16)
            qh = jnp.concatenate(
                [q_ref[0, q_rows, pl.ds((h * N_PAIRS + j) * LANES, LANES)] for j in range(N_PAIRS)],
                axis=0)
            qh = (qh * (SCALE * LOG2E)).astype(BF16)
            s_ref[buf] = lax.dot_general(qh, wk, NT_DIMS, preferred_element_type=F32)

            for j in range(N_PAIRS):
                sinks = [sinks_ref[h, PAIR * j] * LOG2E, sinks_ref[h, PAIR * j + 1] * LOG2E]
                for c in range(WINDOW // ROW_CHUNK):
                    rows = pl.ds(j * WINDOW + c * ROW_CHUNK, ROW_CHUNK)
                    mask = lane_c <= (row_c + c * ROW_CHUNK)
                    e_parts = []
                    for gi in range(PAIR):
                        base = gi * 2 * WINDOW
                        cur = s_ref[buf, rows, pl.ds(base + WINDOW, WINDOW)]
                        prev = s_ref[buf, rows, pl.ds(base, WINDOW)]
                        if blk == 0:
                            prev = prev + first_prev_bias
                        s = jnp.where(mask, cur, prev)
                        m = jnp.maximum(jnp.max(s, axis=-1, keepdims=True), sinks[gi])
                        p = jnp.exp2(s - m)
                        p_ref[buf, rows, pl.ds(base, WINDOW)] = jnp.where(mask, 0.0, p).astype(BF16)
                        p_ref[buf, rows, pl.ds(base + WINDOW, WINDOW)] = (
                            jnp.where(mask, p, 0.0).astype(BF16))
                        e_parts.append(jnp.broadcast_to(jnp.exp2(sinks[gi] - m), (ROW_CHUNK, LANES)))
                    e_ref[buf, rows, :] = jnp.where(lo_c, e_parts[0], e_parts[1])

            o_ext = jnp.dot(p_ref[buf], vext, preferred_element_type=F32)
            o = o_ext[:, :LANES] / (o_ext[:, LANES:] + e_ref[buf])
            for j in range(N_PAIRS):
                a_ref[0, q_rows, pl.ds((h * N_PAIRS + j) * LANES, LANES)] = o[j * WINDOW:(j + 1) * WINDOW]

    @pl.when(n == pl.num_programs(1) - 1)
    def _():
        kwin_ref[0] = kc_ref[0, pl.ds((ATTN_BLOCKS - 1) * WINDOW, WINDOW), :]
        vwin_ref[0] = vc_ref[0, pl.ds((ATTN_BLOCKS - 1) * WINDOW, WINDOW), :]


def _attn_prompt(z3, sinks):
    b, s, _ = z3.shape
    step_rows = ATTN_BLOCKS * WINDOW
    kcol, vcol = OFF_K // D_KV, OFF_V // D_KV
    prev = lambda col: (lambda bi, n: (bi, jnp.maximum(n * ATTN_BLOCKS - 1, 0), col))
    cur = lambda col: (lambda bi, n: (bi, n, col))
    win_spec = pl.BlockSpec((1, WINDOW, D_KV), lambda bi, n: (bi, 0, 0))
    rows = N_PAIRS * WINDOW
    n_buf = ATTN_BUFFERS
    return pl.pallas_call(
        _attn_prompt_kernel,
        grid=(b, s // step_rows),
        in_specs=[
            pl.BlockSpec(memory_space=pltpu.SMEM),
            pl.BlockSpec((1, step_rows, D_Q), lambda bi, n: (bi, n, 0)),
            pl.BlockSpec((1, WINDOW, D_KV), prev(kcol)),
            pl.BlockSpec((1, step_rows, D_KV), cur(kcol)),
            pl.BlockSpec((1, WINDOW, D_KV), prev(vcol)),
            pl.BlockSpec((1, step_rows, D_KV), cur(vcol)),
        ],
        out_specs=[pl.BlockSpec((1, step_rows, D_Q), lambda bi, n: (bi, n, 0)), win_spec, win_spec],
        out_shape=[jax.ShapeDtypeStruct((b, s, D_Q), F32),
                   jax.ShapeDtypeStruct((b, WINDOW, D_KV), F32),
                   jax.ShapeDtypeStruct((b, WINDOW, D_KV), F32)],
        scratch_shapes=[pltpu.VMEM((n_buf, rows, PAIR * 2 * WINDOW), F32),
                        pltpu.VMEM((n_buf, rows, PAIR * 2 * WINDOW), BF16),
                        pltpu.VMEM((n_buf, rows, LANES), F32)],
        compiler_params=pltpu.CompilerParams(
            dimension_semantics=("parallel", "arbitrary"), vmem_limit_bytes=VMEM_LIMIT),
        name="attn_prompt",
    )(sinks, z3, z3, z3, z3, z3)


def _attn_sample_kernel(sinkrow_ref, q_ref, kn_ref, vn_ref, ck_ref, cv_ref,
                        a_ref, kwin_ref, vwin_ref, *, bt, t):
    pad_rows = 2 * WINDOW - WINDOW - t
    pad = jnp.zeros((pad_rows, D_KV), F32)
    p_pad = jnp.zeros((pad_rows, N_Q_HEADS * t), F32)
    zeros_col = jnp.zeros((t, LANES), F32)
    lo = lax.broadcasted_iota(jnp.int32, (t, LANES), 1) < HEAD_DIM
    tok = lax.broadcasted_iota(jnp.int32, (t, N_Q_HEADS * t), 1) % t
    key = lax.broadcasted_iota(jnp.int32, (t, N_Q_HEADS * t), 0)
    new_mask = key <= tok
    sinkrow = sinkrow_ref[...]

    def place(piece_col, src_odd, dst_odd):
        own = jnp.where(lo != src_odd, piece_col, 0.0)
        return own if src_odd == dst_odd else pltpu.roll(own, HEAD_DIM, axis=1)

    def body(s, carry):
        kn, vn = kn_ref[s], vn_ref[s]
        ck, cv = ck_ref[s].T, cv_ref[s].T
        k_all = jnp.concatenate([ck, kn, pad], axis=0).astype(BF16)
        v_all = jnp.concatenate([cv, vn, pad], axis=0).astype(BF16)
        q = q_ref[s] * SCALE
        blocks = []
        for h in range(N_KV_HEADS):
            for g in range(GQA_GROUP):
                head = h * GQA_GROUP + g
                piece = place(q[:, (head // PAIR) * LANES:(head // PAIR + 1) * LANES],
                              head % PAIR == 1, h % PAIR == 1)
                cols = [zeros_col] * (D_KV // LANES)
                cols[h // PAIR] = piece
                blocks.append(jnp.concatenate(cols, axis=1))
        wq_t = jnp.concatenate(blocks, axis=0).astype(BF16)
        s_t = lax.dot_general(k_all, wq_t, NT_DIMS, preferred_element_type=F32)
        top = jnp.where(new_mask, s_t[WINDOW:WINDOW + t], s_t[0:t])
        s_m = jnp.concatenate([top, s_t[t:WINDOW]], axis=0)
        m = jnp.maximum(jnp.max(s_m, axis=0, keepdims=True), sinkrow)
        p = jnp.exp(s_m - m)
        denom = jnp.sum(p, axis=0, keepdims=True) + jnp.exp(sinkrow - m)
        probs = p * (1.0 / denom)
        p_all = jnp.concatenate(
            [jnp.where(new_mask, 0.0, probs[0:t]), probs[t:WINDOW],
             jnp.where(new_mask, probs[0:t], 0.0), p_pad], axis=0).astype(BF16)
        o_full = lax.dot_general(p_all, v_all, TN_DIMS, preferred_element_type=F32)
        out_cols = []
        for c in range(N_Q_HEADS // PAIR):
            h = (PAIR * c) // GQA_GROUP
            kv_col = slice((h // PAIR) * LANES, (h // PAIR + 1) * LANES)
            even = place(o_full[(PAIR * c) * t:(PAIR * c + 1) * t, kv_col], h % PAIR == 1, False)
            odd = place(o_full[(PAIR * c + 1) * t:(PAIR * c + 2) * t, kv_col], h % PAIR == 1, True)
            out_cols.append(even + odd)
        a_ref[s] = jnp.concatenate(out_cols, axis=1)
        kwin_ref[s, 0:WINDOW - t, :] = ck[t:, :]
        kwin_ref[s, WINDOW - t:WINDOW, :] = kn
        vwin_ref[s, 0:WINDOW - t, :] = cv[t:, :]
        vwin_ref[s, WINDOW - t:WINDOW, :] = vn
        return carry

    lax.fori_loop(0, bt, body, 0, unroll=Tiles.ATTN_SAMPLE_UNROLL)


def _attn_sample(z3, cache_k, cache_v, sinks, *, bt):
    b, t, _ = z3.shape
    kcol, vcol = OFF_K // D_KV, OFF_V // D_KV
    cache_spec = pl.BlockSpec((bt, WINDOW, D_KV), lambda i: (i, 0, 0))
    cache_t_spec = pl.BlockSpec((bt, D_KV, WINDOW), lambda i: (i, 0, 0))
    sinkrow = jnp.repeat(sinks.reshape(1, N_Q_HEADS), t, axis=1)
    return pl.pallas_call(
        functools.partial(_attn_sample_kernel, bt=bt, t=t),
        grid=(b // bt,),
        in_specs=[
            _const_spec((1, N_Q_HEADS * t)),
            pl.BlockSpec((bt, t, D_Q), lambda i: (i, 0, 0)),
            pl.BlockSpec((bt, t, D_KV), lambda i: (i, 0, kcol)),
            pl.BlockSpec((bt, t, D_KV), lambda i: (i, 0, vcol)),
            cache_t_spec, cache_t_spec,
        ],
        out_specs=[pl.BlockSpec((bt, t, D_Q), lambda i: (i, 0, 0)), cache_spec, cache_spec],
        out_shape=[jax.ShapeDtypeStruct((b, t, D_Q), F32),
                   jax.ShapeDtypeStruct((b, WINDOW, D_KV), F32),
                   jax.ShapeDtypeStruct((b, WINDOW, D_KV), F32)],
        compiler_params=pltpu.CompilerParams(
            dimension_semantics=("parallel",), vmem_limit_bytes=VMEM_LIMIT),
        name="attn_sample",
    )(sinkrow, z3, z3, z3, cache_k, cache_v)


def _mix_tail(pooled_parts, a, ga, gp, x, wpool_ref, pscale_ref, wout_ref, gpost_ref, gpre_ref,
              x1_ref, h2_ref):
    parts = [jnp.dot(pooled_parts[g].astype(BF16), wpool_ref[g], preferred_element_type=F32)
             for g in range(len(POOL_WINDOWS))]
    p = jnp.concatenate(parts, axis=-1) * pscale_ref[...]
    mixed = jax.nn.sigmoid(ga) * a + jax.nn.sigmoid(gp) * p
    y = jnp.dot(mixed.astype(BF16), wout_ref[...], preferred_element_type=F32)
    x1 = x + _rmsnorm(y, gpost_ref[...])
    x1_ref[...] = x1
    h2_ref[...] = _rmsnorm(x1, gpre_ref[...]).astype(BF16)


MIX_RING = 3


def _mix_prompt_kernel(a_hbm, z_hbm, x_hbm, halo_ref, wpool_ref, pscale_ref,
                       wout_ref, gpost_ref, gpre_ref, x1_ref, h2_ref, pool_ref,
                       ext_ref, sum2_ref, sum4_ref, sum8_ref,
                       a_buf, ga_buf, gp_buf, u_buf, x_buf, sem, *, bm, blocks_per_seq):
    step = pl.program_id(0)
    n_steps = pl.num_programs(0)

    def row_block_copies(block):
        slot = block % MIX_RING
        rows = pl.ds(block * bm, bm)
        z_cols = lambda off, width: z_hbm.at[rows, pl.ds(off, width)]
        pairs = [(a_hbm.at[rows, :], a_buf), (z_cols(OFF_GA, D_MODEL), ga_buf),
                 (z_cols(OFF_GP, D_MODEL), gp_buf), (z_cols(OFF_U, D_POOL), u_buf),
                 (x_hbm.at[rows, :], x_buf)]
        return [pltpu.make_async_copy(src, buf.at[slot], sem.at[slot, k])
                for k, (src, buf) in enumerate(pairs)]

    @pl.when(step == 0)
    def _():
        for block in range(MIX_RING - 1):
            for copy in row_block_copies(block):
                copy.start()

    @pl.when(step + MIX_RING - 1 < n_steps)
    def _():
        for copy in row_block_copies(step + MIX_RING - 1):
            copy.start()

    for copy in row_block_copies(step):
        copy.wait()
    slot = step % MIX_RING
    u_ref = u_buf.at[slot]

    assert POOL_WINDOWS == (2, 4, 8, 16)
    blk = step % blocks_per_seq
    pad = jnp.zeros((POOL_PAD, D_POOL), F32)
    ext_ref[0:POOL_PAD, :] = pad
    sum2_ref[0:POOL_PAD, :] = pad
    sum4_ref[0:POOL_PAD, :] = pad[:, POOL_GROUP:]
    ext_ref[POOL_PAD:POOL_PAD + HALO, :] = jnp.where(blk == 0, 0.0, halo_ref[...])
    ext_ref[POOL_PAD + HALO:, :] = u_ref[...]
    n = HALO + bm
    body = pl.ds(POOL_PAD, n)
    shifted = lambda s: pl.ds(POOL_PAD - s, n)
    g1 = pl.ds(POOL_GROUP, D_POOL - POOL_GROUP)
    sum2_ref[body, :] = ext_ref[body, :] + ext_ref[shifted(1), :]
    sum4_ref[body, :] = sum2_ref[body, g1] + sum2_ref[shifted(2), g1]
    sum8_ref[body, :] = sum4_ref[body, POOL_GROUP:] + sum4_ref[shifted(4), POOL_GROUP:]
    first = POOL_PAD + HALO
    rows = pl.ds(first, bm)
    sum16 = sum8_ref[rows, POOL_GROUP:] + sum8_ref[pl.ds(first - 8, bm), POOL_GROUP:]
    wsums = [sum2_ref[rows, 0:POOL_GROUP], sum4_ref[rows, 0:POOL_GROUP],
             sum8_ref[rows, 0:POOL_GROUP], sum16]
    pos = blk * bm + lax.broadcasted_iota(jnp.int32, (bm, 1), 0)
    pooled = []
    for g, w in enumerate(POOL_WINDOWS):
        inv_cnt = 1.0 / jnp.minimum(pos + 1, w).astype(F32)
        pooled.append(wsums[g] * inv_cnt - u_ref[:, pl.ds(g * POOL_GROUP, POOL_GROUP)])
    _mix_tail(pooled, a_buf[slot], ga_buf[slot], gp_buf[slot], x_buf[slot], wpool_ref, pscale_ref,
              wout_ref, gpost_ref, gpre_ref, x1_ref, h2_ref)

    @pl.when(blk == blocks_per_seq - 1)
    def _():
        pool_ref[0] = ext_ref[pl.ds(first + bm - POOL_HIST, POOL_HIST), :]


def _mix_prompt(a2d, z2d, x2d, wpool_bf, pscale, wout_bf, gpost, gpre, *, bm, seq):
    m = x2d.shape[0]
    blocks_per_seq = seq // bm
    row_spec = lambda col: pl.BlockSpec((bm, D_MODEL), lambda i: (i, col))
    halo_blocks = bm // HALO
    assert m // bm >= MIX_RING
    in_hbm = pl.BlockSpec(memory_space=pl.ANY)
    ring = lambda width: pltpu.VMEM((MIX_RING, bm, width), F32)
    return pl.pallas_call(
        functools.partial(_mix_prompt_kernel, bm=bm, blocks_per_seq=blocks_per_seq),
        grid=(m // bm,),
        in_specs=[
            in_hbm,
            in_hbm,
            in_hbm,
            pl.BlockSpec((HALO, D_POOL),
                         lambda i: (jnp.maximum(i * halo_blocks - 1, 0), OFF_U // D_POOL)),
            _const_spec((len(POOL_WINDOWS), POOL_GROUP, POOL_OUT_GROUP)),
            _const_spec((1, D_MODEL)),
            _const_spec((D_MODEL, D_MODEL)),
            _const_spec((1, D_MODEL)),
            _const_spec((1, D_MODEL)),
        ],
        out_specs=[row_spec(0), row_spec(0),
                   pl.BlockSpec((1, POOL_HIST, D_POOL), lambda i: (i // blocks_per_seq, 0, 0))],
        out_shape=[jax.ShapeDtypeStruct((m, D_MODEL), F32),
                   jax.ShapeDtypeStruct((m, D_MODEL), BF16),
                   jax.ShapeDtypeStruct((m // seq, POOL_HIST, D_POOL), F32)],
        scratch_shapes=[pltpu.VMEM((POOL_PAD + HALO + bm, D_POOL), F32),
                        pltpu.VMEM((POOL_PAD + HALO + bm, D_POOL), F32),
                        pltpu.VMEM((POOL_PAD + HALO + bm, D_POOL - POOL_GROUP), F32),
                        pltpu.VMEM((POOL_PAD + HALO + bm, D_POOL - 2 * POOL_GROUP), F32),
                        ring(D_MODEL), ring(D_MODEL), ring(D_MODEL), ring(D_POOL), ring(D_MODEL),
                        pltpu.SemaphoreType.DMA((MIX_RING, 5))],
        compiler_params=pltpu.CompilerParams(
            dimension_semantics=("arbitrary",), vmem_limit_bytes=VMEM_LIMIT),
        name="mix_prompt",
    )(a2d, z2d, x2d, z2d, wpool_bf, pscale, wout_bf, gpost, gpre)


def _mix_sample_kernel(a_ref, ga_ref, gp_ref, u_ref, hist_ref, x_ref, wpool_ref, pscale_ref,
                       wout_ref, gpost_ref, gpre_ref, x1_ref, h2_ref, pool_ref, *, bt, t, pos0):
    assert t == 8 and max(POOL_WINDOWS) >= t and pos0 + 1 >= max(POOL_WINDOWS)
    u3 = u_ref[...].reshape(bt, t, D_POOL)
    def shifted(x, k):
        tok = lax.broadcasted_iota(jnp.int32, x.shape, 1)
        return jnp.where(tok >= k, pltpu.roll(x, k, axis=1), 0.0)

    sum2 = u3 + shifted(u3, 1)
    sum4 = sum2[..., POOL_GROUP:] + shifted(sum2[..., POOL_GROUP:], 2)
    sum8 = sum4[..., POOL_GROUP:] + shifted(sum4[..., POOL_GROUP:], 4)
    new_sums = [sum2[..., :POOL_GROUP], sum4[..., :POOL_GROUP], sum8[..., :POOL_GROUP],
                sum8[..., POOL_GROUP:]]

    zero_plane = jnp.zeros((bt, POOL_GROUP), F32)
    pooled = []
    for g, w in enumerate(POOL_WINDOWS):
        lanes = pl.ds(g * POOL_GROUP, POOL_GROUP)
        planes = [zero_plane] * t
        suffix = None
        for r in range(POOL_HIST - 1, -1, -1):
            token = r - (POOL_HIST + 1 - w)
            if token < 0:
                break
            row = hist_ref[r, :, lanes]
            suffix = row if suffix is None else suffix + row
            if token < t:
                planes[token] = suffix
        hist_sums = jnp.swapaxes(jnp.stack(planes, axis=0), 0, 1)
        wsum = (new_sums[g] + hist_sums).reshape(bt * t, POOL_GROUP)
        pooled.append(wsum * (1.0 / w) - u_ref[:, lanes])
    _mix_tail(pooled, a_ref[...], ga_ref[...], gp_ref[...], x_ref[...], wpool_ref, pscale_ref,
              wout_ref, gpost_ref, gpre_ref, x1_ref, h2_ref)
    keep = POOL_HIST - t
    pool_ref[0:keep] = hist_ref[POOL_HIST - keep:POOL_HIST]
    pool_ref[keep:POOL_HIST] = jnp.swapaxes(u3, 0, 1)


def _mix_sample(a2d, z2d, x2d, hist, wpool_bf, pscale, wout_bf, gpost, gpre, *, bt, t):
    m = x2d.shape[0]
    bm = bt * t
    row_spec = lambda col: pl.BlockSpec((bm, D_MODEL), lambda i: (i, col))
    return pl.pallas_call(
        functools.partial(_mix_sample_kernel, bt=bt, t=t, pos0=PAST_LEN),
        grid=(m // bm,),
        in_specs=[
            row_spec(0),
            row_spec(OFF_GA // D_MODEL),
            row_spec(OFF_GP // D_MODEL),
            pl.BlockSpec((bm, D_POOL), lambda i: (i, OFF_U // D_POOL)),
            pl.BlockSpec((POOL_HIST, bt, D_POOL), lambda i: (0, i, 0)),
            row_spec(0),
            _const_spec((len(POOL_WINDOWS), POOL_GROUP, POOL_OUT_GROUP)),
            _const_spec((1, D_MODEL)),
            _const_spec((D_MODEL, D_MODEL)),
            _const_spec((1, D_MODEL)),
            _const_spec((1, D_MODEL)),
        ],
        out_specs=[row_spec(0), row_spec(0),
                   pl.BlockSpec((POOL_HIST, bt, D_POOL), lambda i: (0, i, 0))],
        out_shape=[jax.ShapeDtypeStruct((m, D_MODEL), F32),
                   jax.ShapeDtypeStruct((m, D_MODEL), BF16),
                   jax.ShapeDtypeStruct((POOL_HIST, m // t, D_POOL), F32)],
        compiler_params=pltpu.CompilerParams(
            dimension_semantics=("parallel",), vmem_limit_bytes=VMEM_LIMIT),
        name="mix_sample",
    )(a2d, z2d, z2d, z2d, hist, x2d, wpool_bf, pscale, wout_bf, gpost, gpre)


MLP_TAIL_ROWS = 64


def _mlp_kernel(h2_ref, x1_slice_ref, wup_ref, wdn_ref, g_ref, y_ref, x1_ref, *, bm, slice_rows):
    j = pl.program_id(1)

    @pl.when(j == 0)
    def _():
        y_ref[...] = jnp.zeros_like(y_ref)

    x1_ref[pl.ds(pl.multiple_of(j * slice_rows, slice_rows), slice_rows), :] = x1_slice_ref[...]
    hid = jnp.dot(h2_ref[...], wup_ref[...], preferred_element_type=F32)
    hid = jnp.square(jnp.maximum(hid, 0.0)).astype(BF16)
    y_ref[...] += jnp.dot(hid, wdn_ref[...], preferred_element_type=F32)

    @pl.when(j == pl.num_programs(1) - 1)
    def _():
        def tail(c, carry):
            rows = pl.ds(pl.multiple_of(c * MLP_TAIL_ROWS, MLP_TAIL_ROWS), MLP_TAIL_ROWS)
            y_ref[rows, :] = x1_ref[rows, :] + _rmsnorm(y_ref[rows, :], g_ref[...])
            return carry

        lax.fori_loop(0, bm // MLP_TAIL_ROWS, tail, 0)


def _mlp_cast_kernel(h2_ref, x1_ref, wup_ref, wdn_ref, g_ref, y_ref, wup_bf_ref, wdn_bf_ref):
    j = pl.program_id(0)

    @pl.when(j == 0)
    def _():
        y_ref[...] = jnp.zeros_like(y_ref)

    wup = wup_ref[...].astype(BF16)
    wdn = wdn_ref[...].astype(BF16)
    wup_bf_ref[...] = wup
    wdn_bf_ref[...] = wdn
    hid = jnp.dot(h2_ref[...], wup, preferred_element_type=F32)
    hid = jnp.square(jnp.maximum(hid, 0.0)).astype(BF16)
    y_ref[...] += jnp.dot(hid, wdn, preferred_element_type=F32)

    @pl.when(j == pl.num_programs(0) - 1)
    def _():
        y_ref[...] = x1_ref[...] + _rmsnorm(y_ref[...], g_ref[...])


def _mlp_cast(h2, x1, wup_f32, wdn_f32, g, *, fc):
    m = x1.shape[0]
    return pl.pallas_call(
        _mlp_cast_kernel,
        grid=(D_FF // fc,),
        in_specs=[
            _const_spec((m, D_MODEL)),
            _const_spec((m, D_MODEL)),
            pl.BlockSpec((D_MODEL, fc), lambda j: (0, j)),
            pl.BlockSpec((fc, D_MODEL), lambda j: (j, 0)),
            _const_spec((1, D_MODEL)),
        ],
        out_specs=[pl.BlockSpec((m, D_MODEL), lambda j: (0, 0)),
                   pl.BlockSpec((D_MODEL, fc), lambda j: (0, j)),
                   pl.BlockSpec((fc, D_MODEL), lambda j: (j, 0))],
        out_shape=[jax.ShapeDtypeStruct((m, D_MODEL), F32),
                   jax.ShapeDtypeStruct((D_MODEL, D_FF), BF16),
                   jax.ShapeDtypeStruct((D_FF, D_MODEL), BF16)],
        compiler_params=pltpu.CompilerParams(
            dimension_semantics=("arbitrary",), vmem_limit_bytes=VMEM_LIMIT),
        name="mlp_cast",
    )(h2, x1, wup_f32, wdn_f32, g)


def _mlp(h2, x1, wup_bf, wdn_bf, g, *, bm, fc):
    m = x1.shape[0]
    n_ff = D_FF // fc
    slice_rows = bm // n_ff
    return pl.pallas_call(
        functools.partial(_mlp_kernel, bm=bm, slice_rows=slice_rows),
        grid=(m // bm, n_ff),
        in_specs=[
            pl.BlockSpec((bm, D_MODEL), lambda i, j: (i, 0)),
            pl.BlockSpec((slice_rows, D_MODEL), lambda i, j: (i * n_ff + j, 0)),
            pl.BlockSpec((D_MODEL, fc), lambda i, j: (0, j)),
            pl.BlockSpec((fc, D_MODEL), lambda i, j: (j, 0)),
            _const_spec((1, D_MODEL)),
        ],
        out_specs=pl.BlockSpec((bm, D_MODEL), lambda i, j: (i, 0)),
        out_shape=jax.ShapeDtypeStruct((m, D_MODEL), F32),
        scratch_shapes=[pltpu.VMEM((bm, D_MODEL), F32)],
        compiler_params=pltpu.CompilerParams(
            dimension_semantics=("parallel", "arbitrary"), vmem_limit_bytes=VMEM_LIMIT),
        name="mlp",
    )(h2, x1, wup_bf, wdn_bf, g)


def kernel(x_prompt, x_sample, cache_k_win, cache_v_win, state_pool, norm_attn_pre, norm_attn_post,
           w_in, attn_sinks, w_pool, pool_scale, w_out, norm_mlp_pre, norm_mlp_post, w_up, w_down):
    depth = w_in.shape[0]
    assert depth == 1
    b, s, _ = x_prompt.shape
    bs, t, _ = x_sample.shape

    l = 0
    w_pool_bf = w_pool[l].astype(BF16)
    w_out_bf = w_out[l].astype(BF16)
    row = lambda v: v[l].reshape(1, D_MODEL)
    g_attn_pre, g_attn_post = row(norm_attn_pre), row(norm_attn_post)
    g_mlp_pre, g_mlp_post = row(norm_mlp_pre), row(norm_mlp_post)
    pscale = row(pool_scale)
    sinks = attn_sinks[l]

    xs = x_sample.reshape(bs * t, D_MODEL)
    zs, w_in_bf = _in_proj_cast(xs, g_attn_pre, w_in[l])
    stored_order = lambda c: jnp.transpose(c, (0, 2, 3, 1)).reshape(bs, D_KV, WINDOW)
    as_, ks, vs = _attn_sample(zs.reshape(bs, t, D_IN), stored_order(cache_k_win[l]),
                               stored_order(cache_v_win[l]), sinks, bt=Tiles.ATTN_SAMPLE_SEQS)
    hist_rows = jnp.transpose(state_pool[l], (1, 0, 2))
    x1s, h2s, pools = _mix_sample(as_.reshape(bs * t, D_Q), zs, xs, hist_rows, w_pool_bf, pscale,
                                  w_out_bf, g_attn_post, g_mlp_pre, bt=Tiles.MIX_SAMPLE_SEQS, t=t)
    ys, w_up_bf, w_down_bf = _mlp_cast(h2s, x1s, w_up[l], w_down[l], g_mlp_post,
                                       fc=Tiles.MLP_CAST_FF_COLS)

    xp = x_prompt.reshape(b * s, D_MODEL)
    zp = _in_proj(xp, g_attn_pre, w_in_bf, bm=Tiles.IN_PROJ_ROWS, bn=Tiles.IN_PROJ_COLS)
    ap, kp, vp = _attn_prompt(zp.reshape(b, s, D_IN), sinks)
    x1p, h2p, poolp = _mix_prompt(ap.reshape(b * s, D_Q), zp, xp, w_pool_bf, pscale, w_out_bf,
                                  g_attn_post, g_mlp_pre, bm=Tiles.MIX_ROWS, seq=s)
    yp = _mlp(h2p, x1p, w_up_bf, w_down_bf, g_mlp_post, bm=Tiles.MLP_ROWS, fc=Tiles.MLP_FF_COLS)

    kv_shape = lambda nb: (1, nb, WINDOW, N_KV_HEADS, HEAD_DIM)
    return (yp.reshape(b, s, D_MODEL), ys.reshape(bs, t, D_MODEL),
            kp.reshape(kv_shape(b)), vp.reshape(kv_shape(b)), poolp[None],
            ks.reshape(kv_shape(bs)), vs.reshape(kv_shape(bs)),
            jnp.transpose(pools, (1, 0, 2))[None])
```

```python
import functools

import jax
import jax.numpy as jnp
from jax import lax
from jax.experimental import pallas as pl
from jax.experimental.pallas import tpu as pltpu

F32 = jnp.float32
BF16 = jnp.bfloat16

D_MODEL = 2048
HEAD_DIM = 64
N_KV_HEADS = 4
GQA_GROUP = 8
N_Q_HEADS = N_KV_HEADS * GQA_GROUP
WINDOW = 128
D_Q = 2048
D_KV = 256
POOL_WINDOWS = (2, 4, 8, 16)
D_POOL = 1024
POOL_GROUP = 256
POOL_OUT_GROUP = 512
POOL_HIST = 15
D_FF = 8192
D_IN = D_Q + 2 * D_KV + D_POOL + 2 * D_MODEL
PAST_LEN = 8192
EPS = 1e-6
NEG_INF = -1e30
SCALE = HEAD_DIM ** -0.5
LOG2E = 1.4426950408889634

LANES = 128
PAIR = 2
N_PAIRS = GQA_GROUP // PAIR
ROW_CHUNK = 16

OFF_Q, OFF_GA, OFF_GP, OFF_U, OFF_K, OFF_V = 0, 2048, 4096, 6144, 7168, 7424
HALO = 16
POOL_PAD = 8

VMEM_LIMIT = 60 * 1024 * 1024
MXU_TILE = 256


class Tiles:
    IN_PROJ_ROWS, IN_PROJ_COLS = 1024, 6 * MXU_TILE
    IN_PROJ_CAST_COLS = 2 * MXU_TILE
    MIX_ROWS = 256
    MLP_ROWS, MLP_FF_COLS = 1024, 4 * MXU_TILE
    MLP_CAST_FF_COLS = 2 * MXU_TILE
    ATTN_SAMPLE_SEQS = 16
    ATTN_SAMPLE_UNROLL = 8
    MIX_SAMPLE_SEQS = 32

NT_DIMS = (((1,), (1,)), ((), ()))
TN_DIMS = (((0,), (0,)), ((), ()))


def _rmsnorm(x, g):
    r = lax.rsqrt(jnp.mean(x * x, axis=-1, keepdims=True) + EPS)
    return x * r * g


def _const_spec(shape):
    return pl.BlockSpec(shape, lambda *_: (0,) * len(shape), pipeline_mode=pl.Buffered(1))


def _in_proj_kernel(x_ref, g_ref, w_ref, z_ref, h_ref):
    @pl.when(pl.program_id(1) == 0)
    def _():
        h_ref[...] = _rmsnorm(x_ref[...], g_ref[...]).astype(BF16)

    z_ref[...] = jnp.dot(h_ref[...], w_ref[...], preferred_element_type=F32)


def _in_proj_cast_kernel(x_ref, g_ref, w_ref, z_ref, wbf_ref, h_ref):
    @pl.when(pl.program_id(0) == 0)
    def _():
        h_ref[...] = _rmsnorm(x_ref[...], g_ref[...]).astype(BF16)

    w = w_ref[...].astype(BF16)
    wbf_ref[...] = w
    z_ref[...] = jnp.dot(h_ref[...], w, preferred_element_type=F32)


def _in_proj_cast(x2d, g, w_f32):
    m = x2d.shape[0]
    bn = Tiles.IN_PROJ_CAST_COLS
    src_bounds = (D_Q, D_Q + 2 * D_KV, D_Q + 2 * D_KV + D_POOL, D_Q + 2 * D_KV + D_POOL + D_MODEL)
    assert all(off % bn == 0 for off in src_bounds + (OFF_GA, OFF_GP, OFF_U, OFF_K))

    def reordered(j):
        q_end, kv_end, u_end, ga_end = (off // bn for off in src_bounds)
        return jnp.where(
            j < q_end, j, jnp.where(
                j < kv_end, OFF_K // bn + (j - q_end), jnp.where(
                    j < u_end, OFF_U // bn + (j - kv_end), jnp.where(
                        j < ga_end, OFF_GA // bn + (j - u_end), OFF_GP // bn + (j - ga_end)))))

    return pl.pallas_call(
        _in_proj_cast_kernel,
        grid=(D_IN // bn,),
        in_specs=[
            _const_spec((m, D_MODEL)),
            _const_spec((1, D_MODEL)),
            pl.BlockSpec((D_MODEL, bn), lambda j: (0, j)),
        ],
        out_specs=[pl.BlockSpec((m, bn), lambda j: (0, reordered(j))),
                   pl.BlockSpec((D_MODEL, bn), lambda j: (0, reordered(j)))],
        out_shape=[jax.ShapeDtypeStruct((m, D_IN), F32),
                   jax.ShapeDtypeStruct((D_MODEL, D_IN), BF16)],
        scratch_shapes=[pltpu.VMEM((m, D_MODEL), BF16)],
        compiler_params=pltpu.CompilerParams(
            dimension_semantics=("arbitrary",), vmem_limit_bytes=VMEM_LIMIT),
        name="in_proj_cast",
    )(x2d, g, w_f32)


def _in_proj(x2d, g, w_bf, *, bm, bn):
    m = x2d.shape[0]
    return pl.pallas_call(
        _in_proj_kernel,
        grid=(m // bm, D_IN // bn),
        in_specs=[
            pl.BlockSpec((bm, D_MODEL), lambda i, j: (i, 0)),
            _const_spec((1, D_MODEL)),
            pl.BlockSpec((D_MODEL, bn), lambda i, j: (0, j)),
        ],
        out_specs=pl.BlockSpec((bm, bn), lambda i, j: (i, j)),
        out_shape=jax.ShapeDtypeStruct((m, D_IN), F32),
        scratch_shapes=[pltpu.VMEM((bm, D_MODEL), BF16)],
        compiler_params=pltpu.CompilerParams(
            dimension_semantics=("parallel", "arbitrary"), vmem_limit_bytes=VMEM_LIMIT),
        name="in_proj",
    )(x2d, g, w_bf)


def _own_and_swapped(slab, odd, lo):
    own = jnp.where(lo != odd, slab, 0.0)
    swapped = pltpu.roll(own, HEAD_DIM, axis=1)
    return (swapped, own) if odd else (own, swapped)


ATTN_BLOCKS = 8
ATTN_BUFFERS = 8


def _attn_prompt_kernel(sinks_ref, q_ref, kp_ref, kc_ref, vp_ref, vc_ref,
                        a_ref, kwin_ref, vwin_ref, s_ref, p_ref, e_ref):
    n = pl.program_id(1)
    k_rows = jnp.concatenate([kp_ref[0], kc_ref[0]], axis=0)
    v_rows = jnp.concatenate([vp_ref[0], vc_ref[0]], axis=0)
    first_prev_bias = jnp.where(n > 0, 0.0, NEG_INF).astype(F32)
    lo_kv = lax.broadcasted_iota(jnp.int32, (2 * WINDOW, LANES), 1) < HEAD_DIM
    ones_lo = jnp.where(lo_kv, 1.0, 0.0)
    ones_hi = 1.0 - ones_lo
    lo_c = lax.broadcasted_iota(jnp.int32, (ROW_CHUNK, LANES), 1) < HEAD_DIM
    lane_c = lax.broadcasted_iota(jnp.int32, (ROW_CHUNK, LANES), 1)
    row_c = lax.broadcasted_iota(jnp.int32, (ROW_CHUNK, LANES), 0)

    for blk in range(ATTN_BLOCKS):
        k_all = k_rows[blk * WINDOW:(blk + 2) * WINDOW]
        v_all = v_rows[blk * WINDOW:(blk + 2) * WINDOW]
        q_rows = pl.ds(blk * WINDOW, WINDOW)
        for h in range(N_KV_HEADS):
            buf = (blk * N_KV_HEADS + h) % ATTN_BUFFERS
            odd = (h % PAIR) == 1
            kv_col = slice((h // PAIR) * LANES, (h // PAIR + 1) * LANES)
            k_l, k_r = _own_and_swapped(k_all[:, kv_col], odd, lo_kv)
            v_l, v_r = _own_and_swapped(v_all[:, kv_col], odd, lo_kv)
            wk = jnp.concatenate([k_l, k_r], axis=0).astype(BF16)
            vext = jnp.concatenate(
                [jnp.concatenate([v_l, ones_lo], axis=1),
                 jnp.concatenate([v_r, ones_hi], axis=1)], axis=0).astype(BF16)
            qh = jnp.concatenate(
                [q_ref[0, q_rows, pl.ds((h * N_PAIRS + j) * LANES, LANES)] for j in range(N_PAIRS)],
                axis=0)
            qh = (qh * (SCALE * LOG2E)).astype(BF16)
            s_ref[buf] = lax.dot_general(qh, wk, NT_DIMS, preferred_element_type=F32)

            for j in range(N_PAIRS):
                sinks = [sinks_ref[h, PAIR * j] * LOG2E, sinks_ref[h, PAIR * j + 1] * LOG2E]
                for c in range(WINDOW // ROW_CHUNK):
                    rows = pl.ds(j * WINDOW + c * ROW_CHUNK, ROW_CHUNK)
                    mask = lane_c <= (row_c + c * ROW_CHUNK)
                    e_parts = []
                    for gi in range(PAIR):
                        base = gi * 2 * WINDOW
                        cur = s_ref[buf, rows, pl.ds(base + WINDOW, WINDOW)]
                        prev = s_ref[buf, rows, pl.ds(base, WINDOW)]
                        if blk == 0:
                            prev = prev + first_prev_bias
                        s = jnp.where(mask, cur, prev)
                        m = jnp.maximum(jnp.max(s, axis=-1, keepdims=True), sinks[gi])
                        p = jnp.exp2(s - m)
                        p_ref[buf, rows, pl.ds(base, WINDOW)] = jnp.where(mask, 0.0, p).astype(BF16)
                        p_ref[buf, rows, pl.ds(base + WINDOW, WINDOW)] = (
                            jnp.where(mask, p, 0.0).astype(BF16))
                        e_parts.append(jnp.broadcast_to(jnp.exp2(sinks[gi] - m), (ROW_CHUNK, LANES)))
                    e_ref[buf, rows, :] = jnp.where(lo_c, e_parts[0], e_parts[1])

            o_ext = jnp.dot(p_ref[buf], vext, preferred_element_type=F32)
            o = o_ext[:, :LANES] / (o_ext[:, LANES:] + e_ref[buf])
            for j in range(N_PAIRS):
                a_ref[0, q_rows, pl.ds((h * N_PAIRS + j) * LANES, LANES)] = o[j * WINDOW:(j + 1) * WINDOW]

    @pl.when(n == pl.num_programs(1) - 1)
    def _():
        kwin_ref[0] = kc_ref[0, pl.ds((ATTN_BLOCKS - 1) * WINDOW, WINDOW), :]
        vwin_ref[0] = vc_ref[0, pl.ds((ATTN_BLOCKS - 1) * WINDOW, WINDOW), :]


def _attn_prompt(z3, sinks):
    b, s, _ = z3.shape
    step_rows = ATTN_BLOCKS * WINDOW
    kcol, vcol = OFF_K // D_KV, OFF_V // D_KV
    prev = lambda col: (lambda bi, n: (bi, jnp.maximum(n * ATTN_BLOCKS - 1, 0), col))
    cur = lambda col: (lambda bi, n: (bi, n, col))
    win_spec = pl.BlockSpec((1, WINDOW, D_KV), lambda bi, n: (bi, 0, 0))
    rows = N_PAIRS * WINDOW
    n_buf = ATTN_BUFFERS
    return pl.pallas_call(
        _attn_prompt_kernel,
        grid=(b, s // step_rows),
        in_specs=[
            pl.BlockSpec(memory_space=pltpu.SMEM),
            pl.BlockSpec((1, step_rows, D_Q), lambda bi, n: (bi, n, 0)),
            pl.BlockSpec((1, WINDOW, D_KV), prev(kcol)),
            pl.BlockSpec((1, step_rows, D_KV), cur(kcol)),
            pl.BlockSpec((1, WINDOW, D_KV), prev(vcol)),
            pl.BlockSpec((1, step_rows, D_KV), cur(vcol)),
        ],
        out_specs=[pl.BlockSpec((1, step_rows, D_Q), lambda bi, n: (bi, n, 0)), win_spec, win_spec],
        out_shape=[jax.ShapeDtypeStruct((b, s, D_Q), F32),
                   jax.ShapeDtypeStruct((b, WINDOW, D_KV), F32),
                   jax.ShapeDtypeStruct((b, WINDOW, D_KV), F32)],
        scratch_shapes=[pltpu.VMEM((n_buf, rows, PAIR * 2 * WINDOW), F32),
                        pltpu.VMEM((n_buf, rows, PAIR * 2 * WINDOW), BF16),
                        pltpu.VMEM((n_buf, rows, LANES), F32)],
        compiler_params=pltpu.CompilerParams(
            dimension_semantics=("parallel", "arbitrary"), vmem_limit_bytes=VMEM_LIMIT),
        name="attn_prompt",
    )(sinks, z3, z3, z3, z3, z3)


def _attn_sample_kernel(sinkrow_ref, q_ref, kn_ref, vn_ref, ck_ref, cv_ref,
                        a_ref, kwin_ref, vwin_ref, *, bt, t):
    pad_rows = 2 * WINDOW - WINDOW - t
    pad = jnp.zeros((pad_rows, D_KV), F32)
    p_pad = jnp.zeros((pad_rows, N_Q_HEADS * t), F32)
    zeros_col = jnp.zeros((t, LANES), F32)
    lo = lax.broadcasted_iota(jnp.int32, (t, LANES), 1) < HEAD_DIM
    tok = lax.broadcasted_iota(jnp.int32, (t, N_Q_HEADS * t), 1) % t
    key = lax.broadcasted_iota(jnp.int32, (t, N_Q_HEADS * t), 0)
    new_mask = key <= tok
    sinkrow = sinkrow_ref[...]

    def place(piece_col, src_odd, dst_odd):
        own = jnp.where(lo != src_odd, piece_col, 0.0)
        return own if src_odd == dst_odd else pltpu.roll(own, HEAD_DIM, axis=1)

    def body(s, carry):
        kn, vn = kn_ref[s], vn_ref[s]
        ck, cv = ck_ref[s].T, cv_ref[s].T
        k_all = jnp.concatenate([ck, kn, pad], axis=0).astype(BF16)
        v_all = jnp.concatenate([cv, vn, pad], axis=0).astype(BF16)
        q = q_ref[s] * SCALE
        blocks = []
        for h in range(N_KV_HEADS):
            for g in range(GQA_GROUP):
                head = h * GQA_GROUP + g
                piece = place(q[:, (head // PAIR) * LANES:(head // PAIR + 1) * LANES],
                              head % PAIR == 1, h % PAIR == 1)
                cols = [zeros_col] * (D_KV // LANES)
                cols[h // PAIR] = piece
                blocks.append(jnp.concatenate(cols, axis=1))
        wq_t = jnp.concatenate(blocks, axis=0).astype(BF16)
        s_t = lax.dot_general(k_all, wq_t, NT_DIMS, preferred_element_type=F32)
        top = jnp.where(new_mask, s_t[WINDOW:WINDOW + t], s_t[0:t])
        s_m = jnp.concatenate([top, s_t[t:WINDOW]], axis=0)
        m = jnp.maximum(jnp.max(s_m, axis=0, keepdims=True), sinkrow)
        p = jnp.exp(s_m - m)
        denom = jnp.sum(p, axis=0, keepdims=True) + jnp.exp(sinkrow - m)
        probs = p * (1.0 / denom)
        p_all = jnp.concatenate(
            [jnp.where(new_mask, 0.0, probs[0:t]), probs[t:WINDOW],
             jnp.where(new_mask, probs[0:t], 0.0), p_pad], axis=0).astype(BF16)
        o_full = lax.dot_general(p_all, v_all, TN_DIMS, preferred_element_type=F32)
        out_cols = []
        for c in range(N_Q_HEADS // PAIR):
            h = (PAIR * c) // GQA_GROUP
            kv_col = slice((h // PAIR) * LANES, (h // PAIR + 1) * LANES)
            even = place(o_full[(PAIR * c) * t:(PAIR * c + 1) * t, kv_col], h % PAIR == 1, False)
            odd = place(o_full[(PAIR * c + 1) * t:(PAIR * c + 2) * t, kv_col], h % PAIR == 1, True)
            out_cols.append(even + odd)
        a_ref[s] = jnp.concatenate(out_cols, axis=1)
        kwin_ref[s, 0:WINDOW - t, :] = ck[t:, :]
        kwin_ref[s, WINDOW - t:WINDOW, :] = kn
        vwin_ref[s, 0:WINDOW - t, :] = cv[t:, :]
        vwin_ref[s, WINDOW - t:WINDOW, :] = vn
        return carry

    lax.fori_loop(0, bt, body, 0, unroll=Tiles.ATTN_SAMPLE_UNROLL)


def _attn_sample(z3, cache_k, cache_v, sinks, *, bt):
    b, t, _ = z3.shape
    kcol, vcol = OFF_K // D_KV, OFF_V // D_KV
    cache_spec = pl.BlockSpec((bt, WINDOW, D_KV), lambda i: (i, 0, 0))
    cache_t_spec = pl.BlockSpec((bt, D_KV, WINDOW), lambda i: (i, 0, 0))
    sinkrow = jnp.repeat(sinks.reshape(1, N_Q_HEADS), t, axis=1)
    return pl.pallas_call(
        functools.partial(_attn_sample_kernel, bt=bt, t=t),
        grid=(b // bt,),
        in_specs=[
            _const_spec((1, N_Q_HEADS * t)),
            pl.BlockSpec((bt, t, D_Q), lambda i: (i, 0, 0)),
            pl.BlockSpec((bt, t, D_KV), lambda i: (i, 0, kcol)),
            pl.BlockSpec((bt, t, D_KV), lambda i: (i, 0, vcol)),
            cache_t_spec, cache_t_spec,
        ],
        out_specs=[pl.BlockSpec((bt, t, D_Q), lambda i: (i, 0, 0)), cache_spec, cache_spec],
        out_shape=[jax.ShapeDtypeStruct((b, t, D_Q), F32),
                   jax.ShapeDtypeStruct((b, WINDOW, D_KV), F32),
                   jax.ShapeDtypeStruct((b, WINDOW, D_KV), F32)],
        compiler_params=pltpu.CompilerParams(
            dimension_semantics=("parallel",), vmem_limit_bytes=VMEM_LIMIT),
        name="attn_sample",
    )(sinkrow, z3, z3, z3, cache_k, cache_v)


def _mix_tail(pooled_parts, a, ga, gp, x, wpool_ref, pscale_ref, wout_ref, gpost_ref, gpre_ref,
              x1_ref, h2_ref):
    parts = [jnp.dot(pooled_parts[g].astype(BF16), wpool_ref[g], preferred_element_type=F32)
             for g in range(len(POOL_WINDOWS))]
    p = jnp.concatenate(parts, axis=-1) * pscale_ref[...]
    mixed = jax.nn.sigmoid(ga) * a + jax.nn.sigmoid(gp) * p
    y = jnp.dot(mixed.astype(BF16), wout_ref[...], preferred_element_type=F32)
    x1 = x + _rmsnorm(y, gpost_ref[...])
    x1_ref[...] = x1
    h2_ref[...] = _rmsnorm(x1, gpre_ref[...]).astype(BF16)


MIX_RING = 3


def _mix_prompt_kernel(a_hbm, z_hbm, x_hbm, halo_ref, wpool_ref, pscale_ref,
                       wout_ref, gpost_ref, gpre_ref, x1_ref, h2_ref, pool_ref,
                       ext_ref, sum2_ref, sum4_ref, sum8_ref,
                       a_buf, ga_buf, gp_buf, u_buf, x_buf, sem, *, bm, blocks_per_seq):
    step = pl.program_id(0)
    n_steps = pl.num_programs(0)

    def row_block_copies(block):
        slot = block % MIX_RING
        rows = pl.ds(block * bm, bm)
        z_cols = lambda off, width: z_hbm.at[rows, pl.ds(off, width)]
        pairs = [(a_hbm.at[rows, :], a_buf), (z_cols(OFF_GA, D_MODEL), ga_buf),
                 (z_cols(OFF_GP, D_MODEL), gp_buf), (z_cols(OFF_U, D_POOL), u_buf),
                 (x_hbm.at[rows, :], x_buf)]
        return [pltpu.make_async_copy(src, buf.at[slot], sem.at[slot, k])
                for k, (src, buf) in enumerate(pairs)]

    @pl.when(step == 0)
    def _():
        for block in range(MIX_RING - 1):
            for copy in row_block_copies(block):
                copy.start()

    @pl.when(step + MIX_RING - 1 < n_steps)
    def _():
        for copy in row_block_copies(step + MIX_RING - 1):
            copy.start()

    for copy in row_block_copies(step):
        copy.wait()
    slot = step % MIX_RING
    u_ref = u_buf.at[slot]

    assert POOL_WINDOWS == (2, 4, 8, 16)
    blk = step % blocks_per_seq
    pad = jnp.zeros((POOL_PAD, D_POOL), F32)
    ext_ref[0:POOL_PAD, :] = pad
    sum2_ref[0:POOL_PAD, :] = pad
    sum4_ref[0:POOL_PAD, :] = pad[:, POOL_GROUP:]
    ext_ref[POOL_PAD:POOL_PAD + HALO, :] = jnp.where(blk == 0, 0.0, halo_ref[...])
    ext_ref[POOL_PAD + HALO:, :] = u_ref[...]
    n = HALO + bm
    body = pl.ds(POOL_PAD, n)
    shifted = lambda s: pl.ds(POOL_PAD - s, n)
    g1 = pl.ds(POOL_GROUP, D_POOL - POOL_GROUP)
    sum2_ref[body, :] = ext_ref[body, :] + ext_ref[shifted(1), :]
    sum4_ref[body, :] = sum2_ref[body, g1] + sum2_ref[shifted(2), g1]
    sum8_ref[body, :] = sum4_ref[body, POOL_GROUP:] + sum4_ref[shifted(4), POOL_GROUP:]
    first = POOL_PAD + HALO
    rows = pl.ds(first, bm)
    sum16 = sum8_ref[rows, POOL_GROUP:] + sum8_ref[pl.ds(first - 8, bm), POOL_GROUP:]
    wsums = [sum2_ref[rows, 0:POOL_GROUP], sum4_ref[rows, 0:POOL_GROUP],
             sum8_ref[rows, 0:POOL_GROUP], sum16]
    pos = blk * bm + lax.broadcasted_iota(jnp.int32, (bm, 1), 0)
    pooled = []
    for g, w in enumerate(POOL_WINDOWS):
        inv_cnt = 1.0 / jnp.minimum(pos + 1, w).astype(F32)
        pooled.append(wsums[g] * inv_cnt - u_ref[:, pl.ds(g * POOL_GROUP, POOL_GROUP)])
    _mix_tail(pooled, a_buf[slot], ga_buf[slot], gp_buf[slot], x_buf[slot], wpool_ref, pscale_ref,
              wout_ref, gpost_ref, gpre_ref, x1_ref, h2_ref)

    @pl.when(blk == blocks_per_seq - 1)
    def _():
        pool_ref[0] = ext_ref[pl.ds(first + bm - POOL_HIST, POOL_HIST), :]


def _mix_prompt(a2d, z2d, x2d, wpool_bf, pscale, wout_bf, gpost, gpre, *, bm, seq):
    m = x2d.shape[0]
    blocks_per_seq = seq // bm
    row_spec = lambda col: pl.BlockSpec((bm, D_MODEL), lambda i: (i, col))
    halo_blocks = bm // HALO
    assert m // bm >= MIX_RING
    in_hbm = pl.BlockSpec(memory_space=pl.ANY)
    ring = lambda width: pltpu.VMEM((MIX_RING, bm, width), F32)
    return pl.pallas_call(
        functools.partial(_mix_prompt_kernel, bm=bm, blocks_per_seq=blocks_per_seq),
        grid=(m // bm,),
        in_specs=[
            in_hbm,
            in_hbm,
            in_hbm,
            pl.BlockSpec((HALO, D_POOL),
                         lambda i: (jnp.maximum(i * halo_blocks - 1, 0), OFF_U // D_POOL)),
            _const_spec((len(POOL_WINDOWS), POOL_GROUP, POOL_OUT_GROUP)),
            _const_spec((1, D_MODEL)),
            _const_spec((D_MODEL, D_MODEL)),
            _const_spec((1, D_MODEL)),
            _const_spec((1, D_MODEL)),
        ],
        out_specs=[row_spec(0), row_spec(0),
                   pl.BlockSpec((1, POOL_HIST, D_POOL), lambda i: (i // blocks_per_seq, 0, 0))],
        out_shape=[jax.ShapeDtypeStruct((m, D_MODEL), F32),
                   jax.ShapeDtypeStruct((m, D_MODEL), BF16),
                   jax.ShapeDtypeStruct((m // seq, POOL_HIST, D_POOL), F32)],
        scratch_shapes=[pltpu.VMEM((POOL_PAD + HALO + bm, D_POOL), F32),
                        pltpu.VMEM((POOL_PAD + HALO + bm, D_POOL), F32),
                        pltpu.VMEM((POOL_PAD + HALO + bm, D_POOL - POOL_GROUP), F32),
                        pltpu.VMEM((POOL_PAD + HALO + bm, D_POOL - 2 * POOL_GROUP), F32),
                        ring(D_MODEL), ring(D_MODEL), ring(D_MODEL), ring(D_POOL), ring(D_MODEL),
                        pltpu.SemaphoreType.DMA((MIX_RING, 5))],
        compiler_params=pltpu.CompilerParams(
            dimension_semantics=("arbitrary",), vmem_limit_bytes=VMEM_LIMIT),
        name="mix_prompt",
    )(a2d, z2d, x2d, z2d, wpool_bf, pscale, wout_bf, gpost, gpre)


def _mix_sample_kernel(a_ref, ga_ref, gp_ref, u_ref, hist_ref, x_ref, wpool_ref, pscale_ref,
                       wout_ref, gpost_ref, gpre_ref, x1_ref, h2_ref, pool_ref, *, bt, t, pos0):
    assert t == 8 and max(POOL_WINDOWS) >= t and pos0 + 1 >= max(POOL_WINDOWS)
    u3 = u_ref[...].reshape(bt, t, D_POOL)
    def shifted(x, k):
        tok = lax.broadcasted_iota(jnp.int32, x.shape, 1)
        return jnp.where(tok >= k, pltpu.roll(x, k, axis=1), 0.0)

    sum2 = u3 + shifted(u3, 1)
    sum4 = sum2[..., POOL_GROUP:] + shifted(sum2[..., POOL_GROUP:], 2)
    sum8 = sum4[..., POOL_GROUP:] + shifted(sum4[..., POOL_GROUP:], 4)
    new_sums = [sum2[..., :POOL_GROUP], sum4[..., :POOL_GROUP], sum8[..., :POOL_GROUP],
                sum8[..., POOL_GROUP:]]

    zero_plane = jnp.zeros((bt, POOL_GROUP), F32)
    pooled = []
    for g, w in enumerate(POOL_WINDOWS):
        lanes = pl.ds(g * POOL_GROUP, POOL_GROUP)
        planes = [zero_plane] * t
        suffix = None
        for r in range(POOL_HIST - 1, -1, -1):
            token = r - (POOL_HIST + 1 - w)
            if token < 0:
                break
            row = hist_ref[r, :, lanes]
            suffix = row if suffix is None else suffix + row
            if token < t:
                planes[token] = suffix
        hist_sums = jnp.swapaxes(jnp.stack(planes, axis=0), 0, 1)
        wsum = (new_sums[g] + hist_sums).reshape(bt * t, POOL_GROUP)
        pooled.append(wsum * (1.0 / w) - u_ref[:, lanes])
    _mix_tail(pooled, a_ref[...], ga_ref[...], gp_ref[...], x_ref[...], wpool_ref, pscale_ref,
              wout_ref, gpost_ref, gpre_ref, x1_ref, h2_ref)
    keep = POOL_HIST - t
    pool_ref[0:keep] = hist_ref[POOL_HIST - keep:POOL_HIST]
    pool_ref[keep:POOL_HIST] = jnp.swapaxes(u3, 0, 1)


def _mix_sample(a2d, z2d, x2d, hist, wpool_bf, pscale, wout_bf, gpost, gpre, *, bt, t):
    m = x2d.shape[0]
    bm = bt * t
    row_spec = lambda col: pl.BlockSpec((bm, D_MODEL), lambda i: (i, col))
    return pl.pallas_call(
        functools.partial(_mix_sample_kernel, bt=bt, t=t, pos0=PAST_LEN),
        grid=(m // bm,),
        in_specs=[
            row_spec(0),
            row_spec(OFF_GA // D_MODEL),
            row_spec(OFF_GP // D_MODEL),
            pl.BlockSpec((bm, D_POOL), lambda i: (i, OFF_U // D_POOL)),
            pl.BlockSpec((POOL_HIST, bt, D_POOL), lambda i: (0, i, 0)),
            row_spec(0),
            _const_spec((len(POOL_WINDOWS), POOL_GROUP, POOL_OUT_GROUP)),
            _const_spec((1, D_MODEL)),
            _const_spec((D_MODEL, D_MODEL)),
            _const_spec((1, D_MODEL)),
            _const_spec((1, D_MODEL)),
        ],
        out_specs=[row_spec(0), row_spec(0),
                   pl.BlockSpec((POOL_HIST, bt, D_POOL), lambda i: (0, i, 0))],
        out_shape=[jax.ShapeDtypeStruct((m, D_MODEL), F32),
                   jax.ShapeDtypeStruct((m, D_MODEL), BF16),
                   jax.ShapeDtypeStruct((POOL_HIST, m // t, D_POOL), F32)],
        compiler_params=pltpu.CompilerParams(
            dimension_semantics=("parallel",), vmem_limit_bytes=VMEM_LIMIT),
        name="mix_sample",
    )(a2d, z2d, z2d, z2d, hist, x2d, wpool_bf, pscale, wout_bf, gpost, gpre)


MLP_TAIL_ROWS = 16


def _mlp_kernel(h2_ref, x1_slice_ref, wup_ref, wdn_ref, g_ref, y_ref, x1_ref, *, bm, slice_rows):
    j = pl.program_id(1)

    @pl.when(j == 0)
    def _():
        y_ref[...] = jnp.zeros_like(y_ref)

    x1_ref[pl.ds(pl.multiple_of(j * slice_rows, slice_rows), slice_rows), :] = x1_slice_ref[...]
    hid = jnp.dot(h2_ref[...], wup_ref[...], preferred_element_type=F32)
    hid = jnp.square(jnp.maximum(hid, 0.0)).astype(BF16)
    y_ref[...] += jnp.dot(hid, wdn_ref[...], preferred_element_type=F32)

    @pl.when(j == pl.num_programs(1) - 1)
    def _():
        for c in range(bm // MLP_TAIL_ROWS):
            rows = pl.ds(c * MLP_TAIL_ROWS, MLP_TAIL_ROWS)
            y_ref[rows, :] = x1_ref[rows, :] + _rmsnorm(y_ref[rows, :], g_ref[...])


def _mlp_cast_kernel(h2_ref, x1_ref, wup_ref, wdn_ref, g_ref, y_ref, wup_bf_ref, wdn_bf_ref):
    j = pl.program_id(0)

    @pl.when(j == 0)
    def _():
        y_ref[...] = jnp.zeros_like(y_ref)

    wup = wup_ref[...].astype(BF16)
    wdn = wdn_ref[...].astype(BF16)
    wup_bf_ref[...] = wup
    wdn_bf_ref[...] = wdn
    hid = jnp.dot(h2_ref[...], wup, preferred_element_type=F32)
    hid = jnp.square(jnp.maximum(hid, 0.0)).astype(BF16)
    y_ref[...] += jnp.dot(hid, wdn, preferred_element_type=F32)

    @pl.when(j == pl.num_programs(0) - 1)
    def _():
        y_ref[...] = x1_ref[...] + _rmsnorm(y_ref[...], g_ref[...])


def _mlp_cast(h2, x1, wup_f32, wdn_f32, g, *, fc):
    m = x1.shape[0]
    return pl.pallas_call(
        _mlp_cast_kernel,
        grid=(D_FF // fc,),
        in_specs=[
            _const_spec((m, D_MODEL)),
            _const_spec((m, D_MODEL)),
            pl.BlockSpec((D_MODEL, fc), lambda j: (0, j)),
            pl.BlockSpec((fc, D_MODEL), lambda j: (j, 0)),
            _const_spec((1, D_MODEL)),
        ],
        out_specs=[pl.BlockSpec((m, D_MODEL), lambda j: (0, 0)),
                   pl.BlockSpec((D_MODEL, fc), lambda j: (0, j)),
                   pl.BlockSpec((fc, D_MODEL), lambda j: (j, 0))],
        out_shape=[jax.ShapeDtypeStruct((m, D_MODEL), F32),
                   jax.ShapeDtypeStruct((D_MODEL, D_FF), BF16),
                   jax.ShapeDtypeStruct((D_FF, D_MODEL), BF16)],
        compiler_params=pltpu.CompilerParams(
            dimension_semantics=("arbitrary",), vmem_limit_bytes=VMEM_LIMIT),
        name="mlp_cast",
    )(h2, x1, wup_f32, wdn_f32, g)


def _mlp(h2, x1, wup_bf, wdn_bf, g, *, bm, fc):
    m = x1.shape[0]
    n_ff = D_FF // fc
    slice_rows = bm // n_ff
    return pl.pallas_call(
        functools.partial(_mlp_kernel, bm=bm, slice_rows=slice_rows),
        grid=(m // bm, n_ff),
        in_specs=[
            pl.BlockSpec((bm, D_MODEL), lambda i, j: (i, 0)),
            pl.BlockSpec((slice_rows, D_MODEL), lambda i, j: (i * n_ff + j, 0)),
            pl.BlockSpec((D_MODEL, fc), lambda i, j: (0, j)),
            pl.BlockSpec((fc, D_MODEL), lambda i, j: (j, 0)),
            _const_spec((1, D_MODEL)),
        ],
        out_specs=pl.BlockSpec((bm, D_MODEL), lambda i, j: (i, 0)),
        out_shape=jax.ShapeDtypeStruct((m, D_MODEL), F32),
        scratch_shapes=[pltpu.VMEM((bm, D_MODEL), F32)],
        compiler_params=pltpu.CompilerParams(
            dimension_semantics=("parallel", "arbitrary"), vmem_limit_bytes=VMEM_LIMIT),
        name="mlp",
    )(h2, x1, wup_bf, wdn_bf, g)


def kernel(x_prompt, x_sample, cache_k_win, cache_v_win, state_pool, norm_attn_pre, norm_attn_post,
           w_in, attn_sinks, w_pool, pool_scale, w_out, norm_mlp_pre, norm_mlp_post, w_up, w_down):
    depth = w_in.shape[0]
    assert depth == 1
    b, s, _ = x_prompt.shape
    bs, t, _ = x_sample.shape

    l = 0
    w_pool_bf = w_pool[l].astype(BF16)
    w_out_bf = w_out[l].astype(BF16)
    row = lambda v: v[l].reshape(1, D_MODEL)
    g_attn_pre, g_attn_post = row(norm_attn_pre), row(norm_attn_post)
    g_mlp_pre, g_mlp_post = row(norm_mlp_pre), row(norm_mlp_post)
    pscale = row(pool_scale)
    sinks = attn_sinks[l]

    xs = x_sample.reshape(bs * t, D_MODEL)
    zs, w_in_bf = _in_proj_cast(xs, g_attn_pre, w_in[l])
    stored_order = lambda c: jnp.transpose(c, (0, 2, 3, 1)).reshape(bs, D_KV, WINDOW)
    as_, ks, vs = _attn_sample(zs.reshape(bs, t, D_IN), stored_order(cache_k_win[l]),
                               stored_order(cache_v_win[l]), sinks, bt=Tiles.ATTN_SAMPLE_SEQS)
    hist_rows = jnp.transpose(state_pool[l], (1, 0, 2))
    x1s, h2s, pools = _mix_sample(as_.reshape(bs * t, D_Q), zs, xs, hist_rows, w_pool_bf, pscale,
                                  w_out_bf, g_attn_post, g_mlp_pre, bt=Tiles.MIX_SAMPLE_SEQS, t=t)
    ys, w_up_bf, w_down_bf = _mlp_cast(h2s, x1s, w_up[l], w_down[l], g_mlp_post,
                                       fc=Tiles.MLP_CAST_FF_COLS)

    xp = x_prompt.reshape(b * s, D_MODEL)
    zp = _in_proj(xp, g_attn_pre, w_in_bf, bm=Tiles.IN_PROJ_ROWS, bn=Tiles.IN_PROJ_COLS)
    ap, kp, vp = _attn_prompt(zp.reshape(b, s, D_IN), sinks)
    x1p, h2p, poolp = _mix_prompt(ap.reshape(b * s, D_Q), zp, xp, w_pool_bf, pscale, w_out_bf,
                                  g_attn_post, g_mlp_pre, bm=Tiles.MIX_ROWS, seq=s)
    yp = _mlp(h2p, x1p, w_up_bf, w_down_bf, g_mlp_post, bm=Tiles.MLP_ROWS, fc=Tiles.MLP_FF_COLS)

    kv_shape = lambda nb: (1, nb, WINDOW, N_KV_HEADS, HEAD_DIM)
    return (yp.reshape(b, s, D_MODEL), ys.reshape(bs, t, D_MODEL),
            kp.reshape(kv_shape(b)), vp.reshape(kv_shape(b)), poolp[None],
            ks.reshape(kv_shape(bs)), vs.reshape(kv_shape(bs)),
            jnp.transpose(pools, (1, 0, 2))[None])
```

```python
import functools

import jax
import jax.numpy as jnp
from jax import lax
from jax.experimental import pallas as pl
from jax.experimental.pallas import tpu as pltpu

F32 = jnp.float32
BF16 = jnp.bfloat16

D_MODEL = 2048
HEAD_DIM = 64
N_KV_HEADS = 4
GQA_GROUP = 8
N_Q_HEADS = N_KV_HEADS * GQA_GROUP
WINDOW = 128
D_Q = 2048
D_KV = 256
POOL_WINDOWS = (2, 4, 8, 16)
D_POOL = 1024
POOL_GROUP = 256
POOL_OUT_GROUP = 512
POOL_HIST = 15
D_FF = 8192
D_IN = D_Q + 2 * D_KV + D_POOL + 2 * D_MODEL
PAST_LEN = 8192
EPS = 1e-6
NEG_INF = -1e30
SCALE = HEAD_DIM ** -0.5
LOG2E = 1.4426950408889634

LANES = 128
PAIR = 2
N_PAIRS = GQA_GROUP // PAIR
ROW_CHUNK = 16

OFF_Q, OFF_GA, OFF_GP, OFF_U, OFF_K, OFF_V = 0, 2048, 4096, 6144, 7168, 7424
HALO = 16
POOL_PAD = 8

VMEM_LIMIT = 60 * 1024 * 1024
MXU_TILE = 256


class Tiles:
    IN_PROJ_ROWS, IN_PROJ_COLS = 1024, 6 * MXU_TILE
    IN_PROJ_CAST_COLS = 2 * MXU_TILE
    MIX_ROWS = 256
    MLP_ROWS, MLP_FF_COLS = 1024, 4 * MXU_TILE
    MLP_CAST_FF_COLS = 2 * MXU_TILE
    ATTN_SAMPLE_SEQS = 16
    ATTN_SAMPLE_UNROLL = 8
    MIX_SAMPLE_SEQS = 32

NT_DIMS = (((1,), (1,)), ((), ()))
TN_DIMS = (((0,), (0,)), ((), ()))


def _rmsnorm(x, g):
    r = lax.rsqrt(jnp.mean(x * x, axis=-1, keepdims=True) + EPS)
    return x * r * g


def _const_spec(shape):
    return pl.BlockSpec(shape, lambda *_: (0,) * len(shape), pipeline_mode=pl.Buffered(1))


def _in_proj_kernel(x_ref, g_ref, w_ref, z_ref, h_ref):
    @pl.when(pl.program_id(1) == 0)
    def _():
        h_ref[...] = _rmsnorm(x_ref[...], g_ref[...]).astype(BF16)

    z_ref[...] = jnp.dot(h_ref[...], w_ref[...], preferred_element_type=F32)


def _in_proj_cast_kernel(x_ref, g_ref, w_ref, z_ref, wbf_ref, h_ref):
    @pl.when(pl.program_id(0) == 0)
    def _():
        h_ref[...] = _rmsnorm(x_ref[...], g_ref[...]).astype(BF16)

    w = w_ref[...].astype(BF16)
    wbf_ref[...] = w
    z_ref[...] = jnp.dot(h_ref[...], w, preferred_element_type=F32)


def _in_proj_cast(x2d, g, w_f32):
    m = x2d.shape[0]
    bn = Tiles.IN_PROJ_CAST_COLS
    src_bounds = (D_Q, D_Q + 2 * D_KV, D_Q + 2 * D_KV + D_POOL, D_Q + 2 * D_KV + D_POOL + D_MODEL)
    assert all(off % bn == 0 for off in src_bounds + (OFF_GA, OFF_GP, OFF_U, OFF_K))

    def reordered(j):
        q_end, kv_end, u_end, ga_end = (off // bn for off in src_bounds)
        return jnp.where(
            j < q_end, j, jnp.where(
                j < kv_end, OFF_K // bn + (j - q_end), jnp.where(
                    j < u_end, OFF_U // bn + (j - kv_end), jnp.where(
                        j < ga_end, OFF_GA // bn + (j - u_end), OFF_GP // bn + (j - ga_end)))))

    return pl.pallas_call(
        _in_proj_cast_kernel,
        grid=(D_IN // bn,),
        in_specs=[
            _const_spec((m, D_MODEL)),
            _const_spec((1, D_MODEL)),
            pl.BlockSpec((D_MODEL, bn), lambda j: (0, j)),
        ],
        out_specs=[pl.BlockSpec((m, bn), lambda j: (0, reordered(j))),
                   pl.BlockSpec((D_MODEL, bn), lambda j: (0, reordered(j)))],
        out_shape=[jax.ShapeDtypeStruct((m, D_IN), F32),
                   jax.ShapeDtypeStruct((D_MODEL, D_IN), BF16)],
        scratch_shapes=[pltpu.VMEM((m, D_MODEL), BF16)],
        compiler_params=pltpu.CompilerParams(
            dimension_semantics=("arbitrary",), vmem_limit_bytes=VMEM_LIMIT),
        name="in_proj_cast",
    )(x2d, g, w_f32)


def _in_proj(x2d, g, w_bf, *, bm, bn):
    m = x2d.shape[0]
    return pl.pallas_call(
        _in_proj_kernel,
        grid=(m // bm, D_IN // bn),
        in_specs=[
            pl.BlockSpec((bm, D_MODEL), lambda i, j: (i, 0)),
            _const_spec((1, D_MODEL)),
            pl.BlockSpec((D_MODEL, bn), lambda i, j: (0, j)),
        ],
        out_specs=pl.BlockSpec((bm, bn), lambda i, j: (i, j)),
        out_shape=jax.ShapeDtypeStruct((m, D_IN), F32),
        scratch_shapes=[pltpu.VMEM((bm, D_MODEL), BF16)],
        compiler_params=pltpu.CompilerParams(
            dimension_semantics=("parallel", "arbitrary"), vmem_limit_bytes=VMEM_LIMIT),
        name="in_proj",
    )(x2d, g, w_bf)


def _own_and_swapped(slab, odd, lo):
    own = jnp.where(lo != odd, slab, 0.0)
    swapped = pltpu.roll(own, HEAD_DIM, axis=1)
    return (swapped, own) if odd else (own, swapped)


ATTN_BLOCKS = 8
ATTN_BUFFERS = 8


def _attn_prompt_kernel(sinks_ref, q_ref, kp_ref, kc_ref, vp_ref, vc_ref,
                        a_ref, kwin_ref, vwin_ref, s_ref, p_ref, e_ref):
    n = pl.program_id(1)
    k_rows = jnp.concatenate([kp_ref[0], kc_ref[0]], axis=0)
    v_rows = jnp.concatenate([vp_ref[0], vc_ref[0]], axis=0)
    first_prev_bias = jnp.where(n > 0, 0.0, NEG_INF).astype(F32)
    lo_kv = lax.broadcasted_iota(jnp.int32, (2 * WINDOW, LANES), 1) < HEAD_DIM
    ones_lo = jnp.where(lo_kv, 1.0, 0.0)
    ones_hi = 1.0 - ones_lo
    lo_c = lax.broadcasted_iota(jnp.int32, (ROW_CHUNK, LANES), 1) < HEAD_DIM
    lane_c = lax.broadcasted_iota(jnp.int32, (ROW_CHUNK, LANES), 1)
    row_c = lax.broadcasted_iota(jnp.int32, (ROW_CHUNK, LANES), 0)

    for blk in range(ATTN_BLOCKS):
        k_all = k_rows[blk * WINDOW:(blk + 2) * WINDOW]
        v_all = v_rows[blk * WINDOW:(blk + 2) * WINDOW]
        q_rows = pl.ds(blk * WINDOW, WINDOW)
        for h in range(N_KV_HEADS):
            buf = (blk * N_KV_HEADS + h) % ATTN_BUFFERS
            odd = (h % PAIR) == 1
            kv_col = slice((h // PAIR) * LANES, (h // PAIR + 1) * LANES)
            k_l, k_r = _own_and_swapped(k_all[:, kv_col], odd, lo_kv)
            v_l, v_r = _own_and_swapped(v_all[:, kv_col], odd, lo_kv)
            wk = jnp.concatenate([k_l, k_r], axis=0).astype(BF16)
            vext = jnp.concatenate(
                [jnp.concatenate([v_l, ones_lo], axis=1),
                 jnp.concatenate([v_r, ones_hi], axis=1)], axis=0).astype(BF16)
            qh = jnp.concatenate(
                [q_ref[0, q_rows, pl.ds((h * N_PAIRS + j) * LANES, LANES)] for j in range(N_PAIRS)],
                axis=0)
            qh = (qh * (SCALE * LOG2E)).astype(BF16)
            s_ref[buf] = lax.dot_general(qh, wk, NT_DIMS, preferred_element_type=F32)

            for j in range(N_PAIRS):
                sinks = [sinks_ref[h, PAIR * j] * LOG2E, sinks_ref[h, PAIR * j + 1] * LOG2E]
                for c in range(WINDOW // ROW_CHUNK):
                    rows = pl.ds(j * WINDOW + c * ROW_CHUNK, ROW_CHUNK)
                    mask = lane_c <= (row_c + c * ROW_CHUNK)
                    e_parts = []
                    for gi in range(PAIR):
                        base = gi * 2 * WINDOW
                        cur = s_ref[buf, rows, pl.ds(base + WINDOW, WINDOW)]
                        prev = s_ref[buf, rows, pl.ds(base, WINDOW)]
                        if blk == 0:
                            prev = prev + first_prev_bias
                        s = jnp.where(mask, cur, prev)
                        m = jnp.maximum(jnp.max(s, axis=-1, keepdims=True), sinks[gi])
                        p = jnp.exp2(s - m)
                        p_ref[buf, rows, pl.ds(base, WINDOW)] = jnp.where(mask, 0.0, p).astype(BF16)
                        p_ref[buf, rows, pl.ds(base + WINDOW, WINDOW)] = (
                            jnp.where(mask, p, 0.0).astype(BF16))
                        e_parts.append(jnp.broadcast_to(jnp.exp2(sinks[gi] - m), (ROW_CHUNK, LANES)))
                    e_ref[buf, rows, :] = jnp.where(lo_c, e_parts[0], e_parts[1])

            o_ext = jnp.dot(p_ref[buf], vext, preferred_element_type=F32)
            o = o_ext[:, :LANES] / (o_ext[:, LANES:] + e_ref[buf])
            for j in range(N_PAIRS):
                a_ref[0, q_rows, pl.ds((h * N_PAIRS + j) * LANES, LANES)] = o[j * WINDOW:(j + 1) * WINDOW]

    @pl.when(n == pl.num_programs(1) - 1)
    def _():
        kwin_ref[0] = kc_ref[0, pl.ds((ATTN_BLOCKS - 1) * WINDOW, WINDOW), :]
        vwin_ref[0] = vc_ref[0, pl.ds((ATTN_BLOCKS - 1) * WINDOW, WINDOW), :]


def _attn_prompt(z3, sinks):
    b, s, _ = z3.shape
    step_rows = ATTN_BLOCKS * WINDOW
    kcol, vcol = OFF_K // D_KV, OFF_V // D_KV
    prev = lambda col: (lambda bi, n: (bi, jnp.maximum(n * ATTN_BLOCKS - 1, 0), col))
    cur = lambda col: (lambda bi, n: (bi, n, col))
    win_spec = pl.BlockSpec((1, WINDOW, D_KV), lambda bi, n: (bi, 0, 0))
    rows = N_PAIRS * WINDOW
    n_buf = ATTN_BUFFERS
    return pl.pallas_call(
        _attn_prompt_kernel,
        grid=(b, s // step_rows),
        in_specs=[
            pl.BlockSpec(memory_space=pltpu.SMEM),
            pl.BlockSpec((1, step_rows, D_Q), lambda bi, n: (bi, n, 0)),
            pl.BlockSpec((1, WINDOW, D_KV), prev(kcol)),
            pl.BlockSpec((1, step_rows, D_KV), cur(kcol)),
            pl.BlockSpec((1, WINDOW, D_KV), prev(vcol)),
            pl.BlockSpec((1, step_rows, D_KV), cur(vcol)),
        ],
        out_specs=[pl.BlockSpec((1, step_rows, D_Q), lambda bi, n: (bi, n, 0)), win_spec, win_spec],
        out_shape=[jax.ShapeDtypeStruct((b, s, D_Q), F32),
                   jax.ShapeDtypeStruct((b, WINDOW, D_KV), F32),
                   jax.ShapeDtypeStruct((b, WINDOW, D_KV), F32)],
        scratch_shapes=[pltpu.VMEM((n_buf, rows, PAIR * 2 * WINDOW), F32),
                        pltpu.VMEM((n_buf, rows, PAIR * 2 * WINDOW), BF16),
                        pltpu.VMEM((n_buf, rows, LANES), F32)],
        compiler_params=pltpu.CompilerParams(
            dimension_semantics=("parallel", "arbitrary"), vmem_limit_bytes=VMEM_LIMIT),
        name="attn_prompt",
    )(sinks, z3, z3, z3, z3, z3)


def _attn_sample_kernel(sinkrow_ref, q_ref, kn_ref, vn_ref, ck_ref, cv_ref,
                        a_ref, kwin_ref, vwin_ref, *, bt, t):
    pad_rows = 2 * WINDOW - WINDOW - t
    pad = jnp.zeros((pad_rows, D_KV), F32)
    p_pad = jnp.zeros((pad_rows, N_Q_HEADS * t), F32)
    zeros_col = jnp.zeros((t, LANES), F32)
    lo = lax.broadcasted_iota(jnp.int32, (t, LANES), 1) < HEAD_DIM
    tok = lax.broadcasted_iota(jnp.int32, (t, N_Q_HEADS * t), 1) % t
    key = lax.broadcasted_iota(jnp.int32, (t, N_Q_HEADS * t), 0)
    new_mask = key <= tok
    sinkrow = sinkrow_ref[...]

    def place(piece_col, src_odd, dst_odd):
        own = jnp.where(lo != src_odd, piece_col, 0.0)
        return own if src_odd == dst_odd else pltpu.roll(own, HEAD_DIM, axis=1)

    def body(s, carry):
        kn, vn = kn_ref[s], vn_ref[s]
        ck, cv = ck_ref[s].T, cv_ref[s].T
        k_all = jnp.concatenate([ck, kn, pad], axis=0).astype(BF16)
        v_all = jnp.concatenate([cv, vn, pad], axis=0).astype(BF16)
        q = q_ref[s] * SCALE
        blocks = []
        for h in range(N_KV_HEADS):
            for g in range(GQA_GROUP):
                head = h * GQA_GROUP + g
                piece = place(q[:, (head // PAIR) * LANES:(head // PAIR + 1) * LANES],
                              head % PAIR == 1, h % PAIR == 1)
                cols = [zeros_col] * (D_KV // LANES)
                cols[h // PAIR] = piece
                blocks.append(jnp.concatenate(cols, axis=1))
        wq_t = jnp.concatenate(blocks, axis=0).astype(BF16)
        s_t = lax.dot_general(k_all, wq_t, NT_DIMS, preferred_element_type=F32)
        top = jnp.where(new_mask, s_t[WINDOW:WINDOW + t], s_t[0:t])
        s_m = jnp.concatenate([top, s_t[t:WINDOW]], axis=0)
        m = jnp.maximum(jnp.max(s_m, axis=0, keepdims=True), sinkrow)
        p = jnp.exp(s_m - m)
        denom = jnp.sum(p, axis=0, keepdims=True) + jnp.exp(sinkrow - m)
        probs = p * (1.0 / denom)
        p_all = jnp.concatenate(
            [jnp.where(new_mask, 0.0, probs[0:t]), probs[t:WINDOW],
             jnp.where(new_mask, probs[0:t], 0.0), p_pad], axis=0).astype(BF16)
        o_full = lax.dot_general(p_all, v_all, TN_DIMS, preferred_element_type=F32)
        out_cols = []
        for c in range(N_Q_HEADS // PAIR):
            h = (PAIR * c) // GQA_GROUP
            kv_col = slice((h // PAIR) * LANES, (h // PAIR + 1) * LANES)
            even = place(o_full[(PAIR * c) * t:(PAIR * c + 1) * t, kv_col], h % PAIR == 1, False)
            odd = place(o_full[(PAIR * c + 1) * t:(PAIR * c + 2) * t, kv_col], h % PAIR == 1, True)
            out_cols.append(even + odd)
        a_ref[s] = jnp.concatenate(out_cols, axis=1)
        kwin_ref[s, 0:WINDOW - t, :] = ck[t:, :]
        kwin_ref[s, WINDOW - t:WINDOW, :] = kn
        vwin_ref[s, 0:WINDOW - t, :] = cv[t:, :]
        vwin_ref[s, WINDOW - t:WINDOW, :] = vn
        return carry

    lax.fori_loop(0, bt, body, 0, unroll=Tiles.ATTN_SAMPLE_UNROLL)


def _attn_sample(z3, cache_k, cache_v, sinks, *, bt):
    b, t, _ = z3.shape
    kcol, vcol = OFF_K // D_KV, OFF_V // D_KV
    cache_spec = pl.BlockSpec((bt, WINDOW, D_KV), lambda i: (i, 0, 0))
    cache_t_spec = pl.BlockSpec((bt, D_KV, WINDOW), lambda i: (i, 0, 0))
    sinkrow = jnp.repeat(sinks.reshape(1, N_Q_HEADS), t, axis=1)
    return pl.pallas_call(
        functools.partial(_attn_sample_kernel, bt=bt, t=t),
        grid=(b // bt,),
        in_specs=[
            _const_spec((1, N_Q_HEADS * t)),
            pl.BlockSpec((bt, t, D_Q), lambda i: (i, 0, 0)),
            pl.BlockSpec((bt, t, D_KV), lambda i: (i, 0, kcol)),
            pl.BlockSpec((bt, t, D_KV), lambda i: (i, 0, vcol)),
            cache_t_spec, cache_t_spec,
        ],
        out_specs=[pl.BlockSpec((bt, t, D_Q), lambda i: (i, 0, 0)), cache_spec, cache_spec],
        out_shape=[jax.ShapeDtypeStruct((b, t, D_Q), F32),
                   jax.ShapeDtypeStruct((b, WINDOW, D_KV), F32),
                   jax.ShapeDtypeStruct((b, WINDOW, D_KV), F32)],
        compiler_params=pltpu.CompilerParams(
            dimension_semantics=("parallel",), vmem_limit_bytes=VMEM_LIMIT),
        name="attn_sample",
    )(sinkrow, z3, z3, z3, cache_k, cache_v)


def _mix_tail(pooled_parts, a, ga, gp, x, wpool_ref, pscale_ref, wout_ref, gpost_ref, gpre_ref,
              x1_ref, h2_ref):
    parts = [jnp.dot(pooled_parts[g].astype(BF16), wpool_ref[g], preferred_element_type=F32)
             for g in range(len(POOL_WINDOWS))]
    p = jnp.concatenate(parts, axis=-1) * pscale_ref[...]
    mixed = jax.nn.sigmoid(ga) * a + jax.nn.sigmoid(gp) * p
    y = jnp.dot(mixed.astype(BF16), wout_ref[...], preferred_element_type=F32)
    x1 = x + _rmsnorm(y, gpost_ref[...])
    x1_ref[...] = x1
    h2_ref[...] = _rmsnorm(x1, gpre_ref[...]).astype(BF16)


MIX_RING = 3


def _mix_prompt_kernel(a_hbm, z_hbm, x_hbm, halo_ref, wpool_ref, pscale_ref,
                       wout_ref, gpost_ref, gpre_ref, x1_ref, h2_ref, pool_ref,
                       ext_ref, sum2_ref, sum4_ref, sum8_ref,
                       a_buf, ga_buf, gp_buf, u_buf, x_buf, sem, *, bm, blocks_per_seq):
    step = pl.program_id(0)
    n_steps = pl.num_programs(0)

    def row_block_copies(block):
        slot = block % MIX_RING
        rows = pl.ds(block * bm, bm)
        z_cols = lambda off, width: z_hbm.at[rows, pl.ds(off, width)]
        pairs = [(a_hbm.at[rows, :], a_buf), (z_cols(OFF_GA, D_MODEL), ga_buf),
                 (z_cols(OFF_GP, D_MODEL), gp_buf), (z_cols(OFF_U, D_POOL), u_buf),
                 (x_hbm.at[rows, :], x_buf)]
        return [pltpu.make_async_copy(src, buf.at[slot], sem.at[slot, k])
                for k, (src, buf) in enumerate(pairs)]

    @pl.when(step == 0)
    def _():
        for block in range(MIX_RING - 1):
            for copy in row_block_copies(block):
                copy.start()

    @pl.when(step + MIX_RING - 1 < n_steps)
    def _():
        for copy in row_block_copies(step + MIX_RING - 1):
            copy.start()

    for copy in row_block_copies(step):
        copy.wait()
    slot = step % MIX_RING
    u_ref = u_buf.at[slot]

    assert POOL_WINDOWS == (2, 4, 8, 16)
    blk = step % blocks_per_seq
    pad = jnp.zeros((POOL_PAD, D_POOL), F32)
    ext_ref[0:POOL_PAD, :] = pad
    sum2_ref[0:POOL_PAD, :] = pad
    sum4_ref[0:POOL_PAD, :] = pad[:, POOL_GROUP:]
    ext_ref[POOL_PAD:POOL_PAD + HALO, :] = jnp.where(blk == 0, 0.0, halo_ref[...])
    ext_ref[POOL_PAD + HALO:, :] = u_ref[...]
    n = HALO + bm
    body = pl.ds(POOL_PAD, n)
    shifted = lambda s: pl.ds(POOL_PAD - s, n)
    g1 = pl.ds(POOL_GROUP, D_POOL - POOL_GROUP)
    sum2_ref[body, :] = ext_ref[body, :] + ext_ref[shifted(1), :]
    sum4_ref[body, :] = sum2_ref[body, g1] + sum2_ref[shifted(2), g1]
    sum8_ref[body, :] = sum4_ref[body, POOL_GROUP:] + sum4_ref[shifted(4), POOL_GROUP:]
    first = POOL_PAD + HALO
    rows = pl.ds(first, bm)
    sum16 = sum8_ref[rows, POOL_GROUP:] + sum8_ref[pl.ds(first - 8, bm), POOL_GROUP:]
    wsums = [sum2_ref[rows, 0:POOL_GROUP], sum4_ref[rows, 0:POOL_GROUP],
             sum8_ref[rows, 0:POOL_GROUP], sum16]
    pos = blk * bm + lax.broadcasted_iota(jnp.int32, (bm, 1), 0)
    pooled = []
    for g, w in enumerate(POOL_WINDOWS):
        inv_cnt = 1.0 / jnp.minimum(pos + 1, w).astype(F32)
        pooled.append(wsums[g] * inv_cnt - u_ref[:, pl.ds(g * POOL_GROUP, POOL_GROUP)])
    _mix_tail(pooled, a_buf[slot], ga_buf[slot], gp_buf[slot], x_buf[slot], wpool_ref, pscale_ref,
              wout_ref, gpost_ref, gpre_ref, x1_ref, h2_ref)

    @pl.when(blk == blocks_per_seq - 1)
    def _():
        pool_ref[0] = ext_ref[pl.ds(first + bm - POOL_HIST, POOL_HIST), :]


def _mix_prompt(a2d, z2d, x2d, wpool_bf, pscale, wout_bf, gpost, gpre, *, bm, seq):
    m = x2d.shape[0]
    blocks_per_seq = seq // bm
    row_spec = lambda col: pl.BlockSpec((bm, D_MODEL), lambda i: (i, col))
    halo_blocks = bm // HALO
    assert m // bm >= MIX_RING
    in_hbm = pl.BlockSpec(memory_space=pl.ANY)
    ring = lambda width: pltpu.VMEM((MIX_RING, bm, width), F32)
    return pl.pallas_call(
        functools.partial(_mix_prompt_kernel, bm=bm, blocks_per_seq=blocks_per_seq),
        grid=(m // bm,),
        in_specs=[
            in_hbm,
            in_hbm,
            in_hbm,
            pl.BlockSpec((HALO, D_POOL),
                         lambda i: (jnp.maximum(i * halo_blocks - 1, 0), OFF_U // D_POOL)),
            _const_spec((len(POOL_WINDOWS), POOL_GROUP, POOL_OUT_GROUP)),
            _const_spec((1, D_MODEL)),
            _const_spec((D_MODEL, D_MODEL)),
            _const_spec((1, D_MODEL)),
            _const_spec((1, D_MODEL)),
        ],
        out_specs=[row_spec(0), row_spec(0),
                   pl.BlockSpec((1, POOL_HIST, D_POOL), lambda i: (i // blocks_per_seq, 0, 0))],
        out_shape=[jax.ShapeDtypeStruct((m, D_MODEL), F32),
                   jax.ShapeDtypeStruct((m, D_MODEL), BF16),
                   jax.ShapeDtypeStruct((m // seq, POOL_HIST, D_POOL), F32)],
        scratch_shapes=[pltpu.VMEM((POOL_PAD + HALO + bm, D_POOL), F32),
                        pltpu.VMEM((POOL_PAD + HALO + bm, D_POOL), F32),
                        pltpu.VMEM((POOL_PAD + HALO + bm, D_POOL - POOL_GROUP), F32),
                        pltpu.VMEM((POOL_PAD + HALO + bm, D_POOL - 2 * POOL_GROUP), F32),
                        ring(D_MODEL), ring(D_MODEL), ring(D_MODEL), ring(D_POOL), ring(D_MODEL),
                        pltpu.SemaphoreType.DMA((MIX_RING, 5))],
        compiler_params=pltpu.CompilerParams(
            dimension_semantics=("arbitrary",), vmem_limit_bytes=VMEM_LIMIT),
        name="mix_prompt",
    )(a2d, z2d, x2d, z2d, wpool_bf, pscale, wout_bf, gpost, gpre)


def _mix_sample_kernel(a_ref, ga_ref, gp_ref, u_ref, hist_ref, x_ref, wpool_ref, pscale_ref,
                       wout_ref, gpost_ref, gpre_ref, x1_ref, h2_ref, pool_ref, *, bt, t, pos0):
    assert t == 8 and max(POOL_WINDOWS) >= t and pos0 + 1 >= max(POOL_WINDOWS)
    u3 = u_ref[...].reshape(bt, t, D_POOL)
    def shifted(x, k):
        tok = lax.broadcasted_iota(jnp.int32, x.shape, 1)
        return jnp.where(tok >= k, pltpu.roll(x, k, axis=1), 0.0)

    sum2 = u3 + shifted(u3, 1)
    sum4 = sum2[..., POOL_GROUP:] + shifted(sum2[..., POOL_GROUP:], 2)
    sum8 = sum4[..., POOL_GROUP:] + shifted(sum4[..., POOL_GROUP:], 4)
    new_sums = [sum2[..., :POOL_GROUP], sum4[..., :POOL_GROUP], sum8[..., :POOL_GROUP],
                sum8[..., POOL_GROUP:]]

    zero_plane = jnp.zeros((bt, POOL_GROUP), F32)
    pooled = []
    for g, w in enumerate(POOL_WINDOWS):
        lanes = pl.ds(g * POOL_GROUP, POOL_GROUP)
        planes = [zero_plane] * t
        suffix = None
        for r in range(POOL_HIST - 1, -1, -1):
            token = r - (POOL_HIST + 1 - w)
            if token < 0:
                break
            row = hist_ref[r, :, lanes]
            suffix = row if suffix is None else suffix + row
            if token < t:
                planes[token] = suffix
        hist_sums = jnp.swapaxes(jnp.stack(planes, axis=0), 0, 1)
        wsum = (new_sums[g] + hist_sums).reshape(bt * t, POOL_GROUP)
        pooled.append(wsum * (1.0 / w) - u_ref[:, lanes])
    _mix_tail(pooled, a_ref[...], ga_ref[...], gp_ref[...], x_ref[...], wpool_ref, pscale_ref,
              wout_ref, gpost_ref, gpre_ref, x1_ref, h2_ref)
    keep = POOL_HIST - t
    pool_ref[0:keep] = hist_ref[POOL_HIST - keep:POOL_HIST]
    pool_ref[keep:POOL_HIST] = jnp.swapaxes(u3, 0, 1)


def _mix_sample(a2d, z2d, x2d, hist, wpool_bf, pscale, wout_bf, gpost, gpre, *, bt, t):
    m = x2d.shape[0]
    bm = bt * t
    row_spec = lambda col: pl.BlockSpec((bm, D_MODEL), lambda i: (i, col))
    return pl.pallas_call(
        functools.partial(_mix_sample_kernel, bt=bt, t=t, pos0=PAST_LEN),
        grid=(m // bm,),
        in_specs=[
            row_spec(0),
            row_spec(OFF_GA // D_MODEL),
            row_spec(OFF_GP // D_MODEL),
            pl.BlockSpec((bm, D_POOL), lambda i: (i, OFF_U // D_POOL)),
            pl.BlockSpec((POOL_HIST, bt, D_POOL), lambda i: (0, i, 0)),
            row_spec(0),
            _const_spec((len(POOL_WINDOWS), POOL_GROUP, POOL_OUT_GROUP)),
            _const_spec((1, D_MODEL)),
            _const_spec((D_MODEL, D_MODEL)),
            _const_spec((1, D_MODEL)),
            _const_spec((1, D_MODEL)),
        ],
        out_specs=[row_spec(0), row_spec(0),
                   pl.BlockSpec((POOL_HIST, bt, D_POOL), lambda i: (0, i, 0))],
        out_shape=[jax.ShapeDtypeStruct((m, D_MODEL), F32),
                   jax.ShapeDtypeStruct((m, D_MODEL), BF16),
                   jax.ShapeDtypeStruct((POOL_HIST, m // t, D_POOL), F32)],
        compiler_params=pltpu.CompilerParams(
            dimension_semantics=("parallel",), vmem_limit_bytes=VMEM_LIMIT),
        name="mix_sample",
    )(a2d, z2d, z2d, z2d, hist, x2d, wpool_bf, pscale, wout_bf, gpost, gpre)


MLP_TAIL_ROWS = 16


def _mlp_kernel(h2_ref, x1_slice_ref, wup_ref, wdn_ref, g_ref, y_ref, x1_ref, *, bm, slice_rows):
    j = pl.program_id(1)

    x1_ref[pl.ds(pl.multiple_of(j * slice_rows, slice_rows), slice_rows), :] = x1_slice_ref[...]

    def chunk(first):
        hid = jnp.dot(h2_ref[...], wup_ref[...], preferred_element_type=F32)
        hid = jnp.square(jnp.maximum(hid, 0.0)).astype(BF16)
        part = jnp.dot(hid, wdn_ref[...], preferred_element_type=F32)
        if first:
            y_ref[...] = part
        else:
            y_ref[...] += part

    pl.when(j == 0)(functools.partial(chunk, True))
    pl.when(j > 0)(functools.partial(chunk, False))

    @pl.when(j == pl.num_programs(1) - 1)
    def _():
        for c in range(bm // MLP_TAIL_ROWS):
            rows = pl.ds(c * MLP_TAIL_ROWS, MLP_TAIL_ROWS)
            y_ref[rows, :] = x1_ref[rows, :] + _rmsnorm(y_ref[rows, :], g_ref[...])


def _mlp_cast_kernel(h2_ref, x1_ref, wup_ref, wdn_ref, g_ref, y_ref, wup_bf_ref, wdn_bf_ref):
    j = pl.program_id(0)

    @pl.when(j == 0)
    def _():
        y_ref[...] = jnp.zeros_like(y_ref)

    wup = wup_ref[...].astype(BF16)
    wdn = wdn_ref[...].astype(BF16)
    wup_bf_ref[...] = wup
    wdn_bf_ref[...] = wdn
    hid = jnp.dot(h2_ref[...], wup, preferred_element_type=F32)
    hid = jnp.square(jnp.maximum(hid, 0.0)).astype(BF16)
    y_ref[...] += jnp.dot(hid, wdn, preferred_element_type=F32)

    @pl.when(j == pl.num_programs(0) - 1)
    def _():
        y_ref[...] = x1_ref[...] + _rmsnorm(y_ref[...], g_ref[...])


def _mlp_cast(h2, x1, wup_f32, wdn_f32, g, *, fc):
    m = x1.shape[0]
    return pl.pallas_call(
        _mlp_cast_kernel,
        grid=(D_FF // fc,),
        in_specs=[
            _const_spec((m, D_MODEL)),
            _const_spec((m, D_MODEL)),
            pl.BlockSpec((D_MODEL, fc), lambda j: (0, j)),
            pl.BlockSpec((fc, D_MODEL), lambda j: (j, 0)),
            _const_spec((1, D_MODEL)),
        ],
        out_specs=[pl.BlockSpec((m, D_MODEL), lambda j: (0, 0)),
                   pl.BlockSpec((D_MODEL, fc), lambda j: (0, j)),
                   pl.BlockSpec((fc, D_MODEL), lambda j: (j, 0))],
        out_shape=[jax.ShapeDtypeStruct((m, D_MODEL), F32),
                   jax.ShapeDtypeStruct((D_MODEL, D_FF), BF16),
                   jax.ShapeDtypeStruct((D_FF, D_MODEL), BF16)],
        compiler_params=pltpu.CompilerParams(
            dimension_semantics=("arbitrary",), vmem_limit_bytes=VMEM_LIMIT),
        name="mlp_cast",
    )(h2, x1, wup_f32, wdn_f32, g)


def _mlp(h2, x1, wup_bf, wdn_bf, g, *, bm, fc):
    m = x1.shape[0]
    n_ff = D_FF // fc
    slice_rows = bm // n_ff
    return pl.pallas_call(
        functools.partial(_mlp_kernel, bm=bm, slice_rows=slice_rows),
        grid=(m // bm, n_ff),
        in_specs=[
            pl.BlockSpec((bm, D_MODEL), lambda i, j: (i, 0)),
            pl.BlockSpec((slice_rows, D_MODEL), lambda i, j: (i * n_ff + j, 0)),
            pl.BlockSpec((D_MODEL, fc), lambda i, j: (0, j)),
            pl.BlockSpec((fc, D_MODEL), lambda i, j: (j, 0)),
            _const_spec((1, D_MODEL)),
        ],
        out_specs=pl.BlockSpec((bm, D_MODEL), lambda i, j: (i, 0)),
        out_shape=jax.ShapeDtypeStruct((m, D_MODEL), F32),
        scratch_shapes=[pltpu.VMEM((bm, D_MODEL), F32)],
        compiler_params=pltpu.CompilerParams(
            dimension_semantics=("parallel", "arbitrary"), vmem_limit_bytes=VMEM_LIMIT),
        name="mlp",
    )(h2, x1, wup_bf, wdn_bf, g)


def kernel(x_prompt, x_sample, cache_k_win, cache_v_win, state_pool, norm_attn_pre, norm_attn_post,
           w_in, attn_sinks, w_pool, pool_scale, w_out, norm_mlp_pre, norm_mlp_post, w_up, w_down):
    depth = w_in.shape[0]
    assert depth == 1
    b, s, _ = x_prompt.shape
    bs, t, _ = x_sample.shape

    l = 0
    w_pool_bf = w_pool[l].astype(BF16)
    w_out_bf = w_out[l].astype(BF16)
    row = lambda v: v[l].reshape(1, D_MODEL)
    g_attn_pre, g_attn_post = row(norm_attn_pre), row(norm_attn_post)
    g_mlp_pre, g_mlp_post = row(norm_mlp_pre), row(norm_mlp_post)
    pscale = row(pool_scale)
    sinks = attn_sinks[l]

    xs = x_sample.reshape(bs * t, D_MODEL)
    zs, w_in_bf = _in_proj_cast(xs, g_attn_pre, w_in[l])
    stored_order = lambda c: jnp.transpose(c, (0, 2, 3, 1)).reshape(bs, D_KV, WINDOW)
    as_, ks, vs = _attn_sample(zs.reshape(bs, t, D_IN), stored_order(cache_k_win[l]),
                               stored_order(cache_v_win[l]), sinks, bt=Tiles.ATTN_SAMPLE_SEQS)
    hist_rows = jnp.transpose(state_pool[l], (1, 0, 2))
    x1s, h2s, pools = _mix_sample(as_.reshape(bs * t, D_Q), zs, xs, hist_rows, w_pool_bf, pscale,
                                  w_out_bf, g_attn_post, g_mlp_pre, bt=Tiles.MIX_SAMPLE_SEQS, t=t)
    ys, w_up_bf, w_down_bf = _mlp_cast(h2s, x1s, w_up[l], w_down[l], g_mlp_post,
                                       fc=Tiles.MLP_CAST_FF_COLS)

    xp = x_prompt.reshape(b * s, D_MODEL)
    zp = _in_proj(xp, g_attn_pre, w_in_bf, bm=Tiles.IN_PROJ_ROWS, bn=Tiles.IN_PROJ_COLS)
    ap, kp, vp = _attn_prompt(zp.reshape(b, s, D_IN), sinks)
    x1p, h2p, poolp = _mix_prompt(ap.reshape(b * s, D_Q), zp, xp, w_pool_bf, pscale, w_out_bf,
                                  g_attn_post, g_mlp_pre, bm=Tiles.MIX_ROWS, seq=s)
    yp = _mlp(h2p, x1p, w_up_bf, w_down_bf, g_mlp_post, bm=Tiles.MLP_ROWS, fc=Tiles.MLP_FF_COLS)

    kv_shape = lambda nb: (1, nb, WINDOW, N_KV_HEADS, HEAD_DIM)
    return (yp.reshape(b, s, D_MODEL), ys.reshape(bs, t, D_MODEL),
            kp.reshape(kv_shape(b)), vp.reshape(kv_shape(b)), poolp[None],
            ks.reshape(kv_shape(bs)), vs.reshape(kv_shape(bs)),
            jnp.transpose(pools, (1, 0, 2))[None])
```

```python
import functools

import jax
import jax.numpy as jnp
from jax import lax
from jax.experimental import pallas as pl
from jax.experimental.pallas import tpu as pltpu

F32 = jnp.float32
BF16 = jnp.bfloat16

D_MODEL = 2048
HEAD_DIM = 64
N_KV_HEADS = 4
GQA_GROUP = 8
N_Q_HEADS = N_KV_HEADS * GQA_GROUP
WINDOW = 128
D_Q = 2048
D_KV = 256
POOL_WINDOWS = (2, 4, 8, 16)
D_POOL = 1024
POOL_GROUP = 256
POOL_OUT_GROUP = 512
POOL_HIST = 15
D_FF = 8192
D_IN = D_Q + 2 * D_KV + D_POOL + 2 * D_MODEL
PAST_LEN = 8192
EPS = 1e-6
NEG_INF = -1e30
SCALE = HEAD_DIM ** -0.5
LOG2E = 1.4426950408889634

LANES = 128
PAIR = 2
N_PAIRS = GQA_GROUP // PAIR
ROW_CHUNK = 16

OFF_Q, OFF_GA, OFF_GP, OFF_U, OFF_K, OFF_V = 0, 2048, 4096, 6144, 7168, 7424
HALO = 16
POOL_PAD = 8

VMEM_LIMIT = 60 * 1024 * 1024
MXU_TILE = 256


class Tiles:
    IN_PROJ_ROWS, IN_PROJ_COLS = 1024, 6 * MXU_TILE
    IN_PROJ_CAST_COLS = 2 * MXU_TILE
    MIX_ROWS = 256
    MLP_ROWS, MLP_FF_COLS = 1024, 4 * MXU_TILE
    MLP_CAST_FF_COLS = 2 * MXU_TILE
    ATTN_SAMPLE_SEQS = 16
    ATTN_SAMPLE_UNROLL = 8
    MIX_SAMPLE_SEQS = 32

NT_DIMS = (((1,), (1,)), ((), ()))
TN_DIMS = (((0,), (0,)), ((), ()))


def _rmsnorm(x, g):
    r = lax.rsqrt(jnp.mean(x * x, axis=-1, keepdims=True) + EPS)
    return x * r * g


def _const_spec(shape):
    return pl.BlockSpec(shape, lambda *_: (0,) * len(shape), pipeline_mode=pl.Buffered(1))


def _in_proj_kernel(x_ref, g_ref, w_ref, z_ref, h_ref):
    @pl.when(pl.program_id(1) == 0)
    def _():
        h_ref[...] = _rmsnorm(x_ref[...], g_ref[...]).astype(BF16)

    z_ref[...] = jnp.dot(h_ref[...], w_ref[...], preferred_element_type=F32)


def _in_proj_cast_kernel(x_ref, g_ref, w_ref, z_ref, wbf_ref, h_ref):
    @pl.when(pl.program_id(0) == 0)
    def _():
        h_ref[...] = _rmsnorm(x_ref[...], g_ref[...]).astype(BF16)

    w = w_ref[...].astype(BF16)
    wbf_ref[...] = w
    z_ref[...] = jnp.dot(h_ref[...], w, preferred_element_type=F32)


def _in_proj_cast(x2d, g, w_f32):
    m = x2d.shape[0]
    bn = Tiles.IN_PROJ_CAST_COLS
    src_bounds = (D_Q, D_Q + 2 * D_KV, D_Q + 2 * D_KV + D_POOL, D_Q + 2 * D_KV + D_POOL + D_MODEL)
    assert all(off % bn == 0 for off in src_bounds + (OFF_GA, OFF_GP, OFF_U, OFF_K))

    def reordered(j):
        q_end, kv_end, u_end, ga_end = (off // bn for off in src_bounds)
        return jnp.where(
            j < q_end, j, jnp.where(
                j < kv_end, OFF_K // bn + (j - q_end), jnp.where(
                    j < u_end, OFF_U // bn + (j - kv_end), jnp.where(
                        j < ga_end, OFF_GA // bn + (j - u_end), OFF_GP // bn + (j - ga_end)))))

    return pl.pallas_call(
        _in_proj_cast_kernel,
        grid=(D_IN // bn,),
        in_specs=[
            _const_spec((m, D_MODEL)),
            _const_spec((1, D_MODEL)),
            pl.BlockSpec((D_MODEL, bn), lambda j: (0, j)),
        ],
        out_specs=[pl.BlockSpec((m, bn), lambda j: (0, reordered(j))),
                   pl.BlockSpec((D_MODEL, bn), lambda j: (0, reordered(j)))],
        out_shape=[jax.ShapeDtypeStruct((m, D_IN), F32),
                   jax.ShapeDtypeStruct((D_MODEL, D_IN), BF16)],
        scratch_shapes=[pltpu.VMEM((m, D_MODEL), BF16)],
        compiler_params=pltpu.CompilerParams(
            dimension_semantics=("arbitrary",), vmem_limit_bytes=VMEM_LIMIT),
        name="in_proj_cast",
    )(x2d, g, w_f32)


def _in_proj(x2d, g, w_bf, *, bm, bn):
    m = x2d.shape[0]
    return pl.pallas_call(
        _in_proj_kernel,
        grid=(m // bm, D_IN // bn),
        in_specs=[
            pl.BlockSpec((bm, D_MODEL), lambda i, j: (i, 0)),
            _const_spec((1, D_MODEL)),
            pl.BlockSpec((D_MODEL, bn), lambda i, j: (0, j)),
        ],
        out_specs=pl.BlockSpec((bm, bn), lambda i, j: (i, j)),
        out_shape=jax.ShapeDtypeStruct((m, D_IN), F32),
        scratch_shapes=[pltpu.VMEM((bm, D_MODEL), BF16)],
        compiler_params=pltpu.CompilerParams(
            dimension_semantics=("parallel", "arbitrary"), vmem_limit_bytes=VMEM_LIMIT),
        name="in_proj",
    )(x2d, g, w_bf)


def _own_and_swapped(slab, odd, lo):
    own = jnp.where(lo != odd, slab, 0.0)
    swapped = pltpu.roll(own, HEAD_DIM, axis=1)
    return (swapped, own) if odd else (own, swapped)


ATTN_BLOCKS = 8
ATTN_BUFFERS = 8


def _attn_prompt_kernel(sinks_ref, q_ref, kp_ref, kc_ref, vp_ref, vc_ref,
                        a_ref, kwin_ref, vwin_ref, s_ref, p_ref, e_ref):
    n = pl.program_id(1)
    k_rows = jnp.concatenate([kp_ref[0], kc_ref[0]], axis=0)
    v_rows = jnp.concatenate([vp_ref[0], vc_ref[0]], axis=0)
    first_prev_bias = jnp.where(n > 0, 0.0, NEG_INF).astype(F32)
    lo_kv = lax.broadcasted_iota(jnp.int32, (2 * WINDOW, LANES), 1) < HEAD_DIM
    ones_lo = jnp.where(lo_kv, 1.0, 0.0)
    ones_hi = 1.0 - ones_lo
    lo_c = lax.broadcasted_iota(jnp.int32, (ROW_CHUNK, LANES), 1) < HEAD_DIM
    lane_c = lax.broadcasted_iota(jnp.int32, (ROW_CHUNK, LANES), 1)
    row_c = lax.broadcasted_iota(jnp.int32, (ROW_CHUNK, LANES), 0)

    for blk in range(ATTN_BLOCKS):
        k_all = k_rows[blk * WINDOW:(blk + 2) * WINDOW]
        v_all = v_rows[blk * WINDOW:(blk + 2) * WINDOW]
        q_rows = pl.ds(blk * WINDOW, WINDOW)
        for h in range(N_KV_HEADS):
            buf = (blk * N_KV_HEADS + h) % ATTN_BUFFERS
            odd = (h % PAIR) == 1
            kv_col = slice((h // PAIR) * LANES, (h // PAIR + 1) * LANES)
            k_l, k_r = _own_and_swapped(k_all[:, kv_col], odd, lo_kv)
            v_l, v_r = _own_and_swapped(v_all[:, kv_col], odd, lo_kv)
            wk = jnp.concatenate([k_l, k_r], axis=0).astype(BF16)
            vext = jnp.concatenate(
                [jnp.concatenate([v_l, ones_lo], axis=1),
                 jnp.concatenate([v_r, ones_hi], axis=1)], axis=0).astype(BF16)
            qh = jnp.concatenate(
                [q_ref[0, q_rows, pl.ds((h * N_PAIRS + j) * LANES, LANES)] for j in range(N_PAIRS)],
                axis=0)
            qh = (qh * (SCALE * LOG2E)).astype(BF16)
            s_ref[buf] = lax.dot_general(qh, wk, NT_DIMS, preferred_element_type=F32)

            for j in range(N_PAIRS):
                sinks = [sinks_ref[h, PAIR * j] * LOG2E, sinks_ref[h, PAIR * j + 1] * LOG2E]
                for c in range(WINDOW // ROW_CHUNK):
                    rows = pl.ds(j * WINDOW + c * ROW_CHUNK, ROW_CHUNK)
                    mask = lane_c <= (row_c + c * ROW_CHUNK)
                    e_parts = []
                    for gi in range(PAIR):
                        base = gi * 2 * WINDOW
                        cur = s_ref[buf, rows, pl.ds(base + WINDOW, WINDOW)]
                        prev = s_ref[buf, rows, pl.ds(base, WINDOW)]
                        if blk == 0:
                            prev = prev + first_prev_bias
                        s = jnp.where(mask, cur, prev)
                        m = jnp.maximum(jnp.max(s, axis=-1, keepdims=True), sinks[gi])
                        p = jnp.exp2(s - m)
                        p_ref[buf, rows, pl.ds(base, WINDOW)] = jnp.where(mask, 0.0, p).astype(BF16)
                        p_ref[buf, rows, pl.ds(base + WINDOW, WINDOW)] = (
                            jnp.where(mask, p, 0.0).astype(BF16))
                        e_parts.append(jnp.broadcast_to(jnp.exp2(sinks[gi] - m), (ROW_CHUNK, LANES)))
                    e_ref[buf, rows, :] = jnp.where(lo_c, e_parts[0], e_parts[1])

            o_ext = jnp.dot(p_ref[buf], vext, preferred_element_type=F32)
            o = o_ext[:, :LANES] / (o_ext[:, LANES:] + e_ref[buf])
            for j in range(N_PAIRS):
                a_ref[0, q_rows, pl.ds((h * N_PAIRS + j) * LANES, LANES)] = o[j * WINDOW:(j + 1) * WINDOW]

    @pl.when(n == pl.num_programs(1) - 1)
    def _():
        kwin_ref[0] = kc_ref[0, pl.ds((ATTN_BLOCKS - 1) * WINDOW, WINDOW), :]
        vwin_ref[0] = vc_ref[0, pl.ds((ATTN_BLOCKS - 1) * WINDOW, WINDOW), :]


def _attn_prompt(z3, sinks):
    b, s, _ = z3.shape
    step_rows = ATTN_BLOCKS * WINDOW
    kcol, vcol = OFF_K // D_KV, OFF_V // D_KV
    prev = lambda col: (lambda bi, n: (bi, jnp.maximum(n * ATTN_BLOCKS - 1, 0), col))
    cur = lambda col: (lambda bi, n: (bi, n, col))
    win_spec = pl.BlockSpec((1, WINDOW, D_KV), lambda bi, n: (bi, 0, 0))
    rows = N_PAIRS * WINDOW
    n_buf = ATTN_BUFFERS
    return pl.pallas_call(
        _attn_prompt_kernel,
        grid=(b, s // step_rows),
        in_specs=[
            pl.BlockSpec(memory_space=pltpu.SMEM),
            pl.BlockSpec((1, step_rows, D_Q), lambda bi, n: (bi, n, 0)),
            pl.BlockSpec((1, WINDOW, D_KV), prev(kcol)),
            pl.BlockSpec((1, step_rows, D_KV), cur(kcol)),
            pl.BlockSpec((1, WINDOW, D_KV), prev(vcol)),
            pl.BlockSpec((1, step_rows, D_KV), cur(vcol)),
        ],
        out_specs=[pl.BlockSpec((1, step_rows, D_Q), lambda bi, n: (bi, n, 0)), win_spec, win_spec],
        out_shape=[jax.ShapeDtypeStruct((b, s, D_Q), F32),
                   jax.ShapeDtypeStruct((b, WINDOW, D_KV), F32),
                   jax.ShapeDtypeStruct((b, WINDOW, D_KV), F32)],
        scratch_shapes=[pltpu.VMEM((n_buf, rows, PAIR * 2 * WINDOW), F32),
                        pltpu.VMEM((n_buf, rows, PAIR * 2 * WINDOW), BF16),
                        pltpu.VMEM((n_buf, rows, LANES), F32)],
        compiler_params=pltpu.CompilerParams(
            dimension_semantics=("parallel", "arbitrary"), vmem_limit_bytes=VMEM_LIMIT),
        name="attn_prompt",
    )(sinks, z3, z3, z3, z3, z3)


def _attn_sample_kernel(sinkrow_ref, q_ref, kn_ref, vn_ref, ck_ref, cv_ref,
                        a_ref, kwin_ref, vwin_ref, *, bt, t):
    pad_rows = 2 * WINDOW - WINDOW - t
    pad = jnp.zeros((pad_rows, D_KV), F32)
    p_pad = jnp.zeros((pad_rows, N_Q_HEADS * t), F32)
    zeros_col = jnp.zeros((t, LANES), F32)
    lo = lax.broadcasted_iota(jnp.int32, (t, LANES), 1) < HEAD_DIM
    tok = lax.broadcasted_iota(jnp.int32, (t, N_Q_HEADS * t), 1) % t
    key = lax.broadcasted_iota(jnp.int32, (t, N_Q_HEADS * t), 0)
    new_mask = key <= tok
    sinkrow = sinkrow_ref[...]

    def place(piece_col, src_odd, dst_odd):
        own = jnp.where(lo != src_odd, piece_col, 0.0)
        return own if src_odd == dst_odd else pltpu.roll(own, HEAD_DIM, axis=1)

    def body(s, carry):
        kn, vn = kn_ref[s], vn_ref[s]
        ck, cv = ck_ref[s].T, cv_ref[s].T
        k_all = jnp.concatenate([ck, kn, pad], axis=0).astype(BF16)
        v_all = jnp.concatenate([cv, vn, pad], axis=0).astype(BF16)
        q = q_ref[s] * SCALE
        blocks = []
        for h in range(N_KV_HEADS):
            for g in range(GQA_GROUP):
                head = h * GQA_GROUP + g
                piece = place(q[:, (head // PAIR) * LANES:(head // PAIR + 1) * LANES],
                              head % PAIR == 1, h % PAIR == 1)
                cols = [zeros_col] * (D_KV // LANES)
                cols[h // PAIR] = piece
                blocks.append(jnp.concatenate(cols, axis=1))
        wq_t = jnp.concatenate(blocks, axis=0).astype(BF16)
        s_t = lax.dot_general(k_all, wq_t, NT_DIMS, preferred_element_type=F32)
        top = jnp.where(new_mask, s_t[WINDOW:WINDOW + t], s_t[0:t])
        s_m = jnp.concatenate([top, s_t[t:WINDOW]], axis=0)
        m = jnp.maximum(jnp.max(s_m, axis=0, keepdims=True), sinkrow)
        p = jnp.exp(s_m - m)
        denom = jnp.sum(p, axis=0, keepdims=True) + jnp.exp(sinkrow - m)
        probs = p * (1.0 / denom)
        p_all = jnp.concatenate(
            [jnp.where(new_mask, 0.0, probs[0:t]), probs[t:WINDOW],
             jnp.where(new_mask, probs[0:t], 0.0), p_pad], axis=0).astype(BF16)
        o_full = lax.dot_general(p_all, v_all, TN_DIMS, preferred_element_type=F32)
        out_cols = []
        for c in range(N_Q_HEADS // PAIR):
            h = (PAIR * c) // GQA_GROUP
            kv_col = slice((h // PAIR) * LANES, (h // PAIR + 1) * LANES)
            even = place(o_full[(PAIR * c) * t:(PAIR * c + 1) * t, kv_col], h % PAIR == 1, False)
            odd = place(o_full[(PAIR * c + 1) * t:(PAIR * c + 2) * t, kv_col], h % PAIR == 1, True)
            out_cols.append(even + odd)
        a_ref[s] = jnp.concatenate(out_cols, axis=1)
        kwin_ref[s, 0:WINDOW - t, :] = ck[t:, :]
        kwin_ref[s, WINDOW - t:WINDOW, :] = kn
        vwin_ref[s, 0:WINDOW - t, :] = cv[t:, :]
        vwin_ref[s, WINDOW - t:WINDOW, :] = vn
        return carry

    lax.fori_loop(0, bt, body, 0, unroll=Tiles.ATTN_SAMPLE_UNROLL)


def _attn_sample(z3, cache_k, cache_v, sinks, *, bt):
    b, t, _ = z3.shape
    kcol, vcol = OFF_K // D_KV, OFF_V // D_KV
    cache_spec = pl.BlockSpec((bt, WINDOW, D_KV), lambda i: (i, 0, 0))
    cache_t_spec = pl.BlockSpec((bt, D_KV, WINDOW), lambda i: (i, 0, 0))
    sinkrow = jnp.repeat(sinks.reshape(1, N_Q_HEADS), t, axis=1)
    return pl.pallas_call(
        functools.partial(_attn_sample_kernel, bt=bt, t=t),
        grid=(b // bt,),
        in_specs=[
            _const_spec((1, N_Q_HEADS * t)),
            pl.BlockSpec((bt, t, D_Q), lambda i: (i, 0, 0)),
            pl.BlockSpec((bt, t, D_KV), lambda i: (i, 0, kcol)),
            pl.BlockSpec((bt, t, D_KV), lambda i: (i, 0, vcol)),
            cache_t_spec, cache_t_spec,
        ],
        out_specs=[pl.BlockSpec((bt, t, D_Q), lambda i: (i, 0, 0)), cache_spec, cache_spec],
        out_shape=[jax.ShapeDtypeStruct((b, t, D_Q), F32),
                   jax.ShapeDtypeStruct((b, WINDOW, D_KV), F32),
                   jax.ShapeDtypeStruct((b, WINDOW, D_KV), F32)],
        compiler_params=pltpu.CompilerParams(
            dimension_semantics=("parallel",), vmem_limit_bytes=VMEM_LIMIT),
        name="attn_sample",
    )(sinkrow, z3, z3, z3, cache_k, cache_v)


def _mix_tail(pooled_parts, a, ga, gp, x, wpool_ref, pscale_ref, wout_ref, gpost_ref, gpre_ref,
              x1_ref, h2_ref):
    parts = [jnp.dot(pooled_parts[g].astype(BF16), wpool_ref[g], preferred_element_type=F32)
             for g in range(len(POOL_WINDOWS))]
    p = jnp.concatenate(parts, axis=-1) * pscale_ref[...]
    mixed = jax.nn.sigmoid(ga) * a + jax.nn.sigmoid(gp) * p
    y = jnp.dot(mixed.astype(BF16), wout_ref[...], preferred_element_type=F32)
    x1 = x + _rmsnorm(y, gpost_ref[...])
    x1_ref[...] = x1
    h2_ref[...] = _rmsnorm(x1, gpre_ref[...]).astype(BF16)


MIX_RING = 3


def _mix_prompt_kernel(a_hbm, z_hbm, x_hbm, halo_ref, wpool_ref, pscale_ref,
                       wout_ref, gpost_ref, gpre_ref, x1_ref, h2_ref, pool_ref,
                       ext_ref, sum2_ref, sum4_ref, sum8_ref,
                       a_buf, ga_buf, gp_buf, u_buf, x_buf, sem, *, bm, blocks_per_seq):
    step = pl.program_id(0)
    n_steps = pl.num_programs(0)

    def row_block_copies(block):
        slot = block % MIX_RING
        rows = pl.ds(block * bm, bm)
        z_cols = lambda off, width: z_hbm.at[rows, pl.ds(off, width)]
        pairs = [(a_hbm.at[rows, :], a_buf), (z_cols(OFF_GA, D_MODEL), ga_buf),
                 (z_cols(OFF_GP, D_MODEL), gp_buf), (z_cols(OFF_U, D_POOL), u_buf),
                 (x_hbm.at[rows, :], x_buf)]
        return [pltpu.make_async_copy(src, buf.at[slot], sem.at[slot, k])
                for k, (src, buf) in enumerate(pairs)]

    @pl.when(step == 0)
    def _():
        for block in range(MIX_RING - 1):
            for copy in row_block_copies(block):
                copy.start()

    @pl.when(step + MIX_RING - 1 < n_steps)
    def _():
        for copy in row_block_copies(step + MIX_RING - 1):
            copy.start()

    for copy in row_block_copies(step):
        copy.wait()
    slot = step % MIX_RING
    u_ref = u_buf.at[slot]

    assert POOL_WINDOWS == (2, 4, 8, 16)
    blk = step % blocks_per_seq
    pad = jnp.zeros((POOL_PAD, D_POOL), F32)
    ext_ref[0:POOL_PAD, :] = pad
    sum2_ref[0:POOL_PAD, :] = pad
    sum4_ref[0:POOL_PAD, :] = pad[:, POOL_GROUP:]
    ext_ref[POOL_PAD:POOL_PAD + HALO, :] = jnp.where(blk == 0, 0.0, halo_ref[...])
    ext_ref[POOL_PAD + HALO:, :] = u_ref[...]
    n = HALO + bm
    body = pl.ds(POOL_PAD, n)
    shifted = lambda s: pl.ds(POOL_PAD - s, n)
    g1 = pl.ds(POOL_GROUP, D_POOL - POOL_GROUP)
    sum2_ref[body, :] = ext_ref[body, :] + ext_ref[shifted(1), :]
    sum4_ref[body, :] = sum2_ref[body, g1] + sum2_ref[shifted(2), g1]
    sum8_ref[body, :] = sum4_ref[body, POOL_GROUP:] + sum4_ref[shifted(4), POOL_GROUP:]
    first = POOL_PAD + HALO
    rows = pl.ds(first, bm)
    sum16 = sum8_ref[rows, POOL_GROUP:] + sum8_ref[pl.ds(first - 8, bm), POOL_GROUP:]
    wsums = [sum2_ref[rows, 0:POOL_GROUP], sum4_ref[rows, 0:POOL_GROUP],
             sum8_ref[rows, 0:POOL_GROUP], sum16]
    pos = blk * bm + lax.broadcasted_iota(jnp.int32, (bm, 1), 0)
    pooled = []
    for g, w in enumerate(POOL_WINDOWS):
        inv_cnt = 1.0 / jnp.minimum(pos + 1, w).astype(F32)
        pooled.append(wsums[g] * inv_cnt - u_ref[:, pl.ds(g * POOL_GROUP, POOL_GROUP)])
    _mix_tail(pooled, a_buf[slot], ga_buf[slot], gp_buf[slot], x_buf[slot], wpool_ref, pscale_ref,
              wout_ref, gpost_ref, gpre_ref, x1_ref, h2_ref)

    @pl.when(blk == blocks_per_seq - 1)
    def _():
        pool_ref[0] = ext_ref[pl.ds(first + bm - POOL_HIST, POOL_HIST), :]


def _mix_prompt(a2d, z2d, x2d, wpool_bf, pscale, wout_bf, gpost, gpre, *, bm, seq):
    m = x2d.shape[0]
    blocks_per_seq = seq // bm
    row_spec = lambda col: pl.BlockSpec((bm, D_MODEL), lambda i: (i, col))
    halo_blocks = bm // HALO
    assert m // bm >= MIX_RING
    in_hbm = pl.BlockSpec(memory_space=pl.ANY)
    ring = lambda width: pltpu.VMEM((MIX_RING, bm, width), F32)
    return pl.pallas_call(
        functools.partial(_mix_prompt_kernel, bm=bm, blocks_per_seq=blocks_per_seq),
        grid=(m // bm,),
        in_specs=[
            in_hbm,
            in_hbm,
            in_hbm,
            pl.BlockSpec((HALO, D_POOL),
                         lambda i: (jnp.maximum(i * halo_blocks - 1, 0), OFF_U // D_POOL)),
            _const_spec((len(POOL_WINDOWS), POOL_GROUP, POOL_OUT_GROUP)),
            _const_spec((1, D_MODEL)),
            _const_spec((D_MODEL, D_MODEL)),
            _const_spec((1, D_MODEL)),
            _const_spec((1, D_MODEL)),
        ],
        out_specs=[row_spec(0), row_spec(0),
                   pl.BlockSpec((1, POOL_HIST, D_POOL), lambda i: (i // blocks_per_seq, 0, 0))],
        out_shape=[jax.ShapeDtypeStruct((m, D_MODEL), F32),
                   jax.ShapeDtypeStruct((m, D_MODEL), BF16),
                   jax.ShapeDtypeStruct((m // seq, POOL_HIST, D_POOL), F32)],
        scratch_shapes=[pltpu.VMEM((POOL_PAD + HALO + bm, D_POOL), F32),
                        pltpu.VMEM((POOL_PAD + HALO + bm, D_POOL), F32),
                        pltpu.VMEM((POOL_PAD + HALO + bm, D_POOL - POOL_GROUP), F32),
                        pltpu.VMEM((POOL_PAD + HALO + bm, D_POOL - 2 * POOL_GROUP), F32),
                        ring(D_MODEL), ring(D_MODEL), ring(D_MODEL), ring(D_POOL), ring(D_MODEL),
                        pltpu.SemaphoreType.DMA((MIX_RING, 5))],
        compiler_params=pltpu.CompilerParams(
            dimension_semantics=("arbitrary",), vmem_limit_bytes=VMEM_LIMIT),
        name="mix_prompt",
    )(a2d, z2d, x2d, z2d, wpool_bf, pscale, wout_bf, gpost, gpre)


def _mix_sample_kernel(a_ref, ga_ref, gp_ref, u_ref, hist_ref, x_ref, wpool_ref, pscale_ref,
                       wout_ref, gpost_ref, gpre_ref, x1_ref, h2_ref, pool_ref, *, bt, t, pos0):
    assert t == 8 and max(POOL_WINDOWS) >= t and pos0 + 1 >= max(POOL_WINDOWS)
    u3 = u_ref[...].reshape(bt, t, D_POOL)
    def shifted(x, k):
        tok = lax.broadcasted_iota(jnp.int32, x.shape, 1)
        return jnp.where(tok >= k, pltpu.roll(x, k, axis=1), 0.0)

    sum2 = u3 + shifted(u3, 1)
    sum4 = sum2[..., POOL_GROUP:] + shifted(sum2[..., POOL_GROUP:], 2)
    sum8 = sum4[..., POOL_GROUP:] + shifted(sum4[..., POOL_GROUP:], 4)
    new_sums = [sum2[..., :POOL_GROUP], sum4[..., :POOL_GROUP], sum8[..., :POOL_GROUP],
                sum8[..., POOL_GROUP:]]

    zero_plane = jnp.zeros((bt, POOL_GROUP), F32)
    pooled = []
    for g, w in enumerate(POOL_WINDOWS):
        lanes = pl.ds(g * POOL_GROUP, POOL_GROUP)
        planes = [zero_plane] * t
        suffix = None
        for r in range(POOL_HIST - 1, -1, -1):
            token = r - (POOL_HIST + 1 - w)
            if token < 0:
                break
            row = hist_ref[r, :, lanes]
            suffix = row if suffix is None else suffix + row
            if token < t:
                planes[token] = suffix
        hist_sums = jnp.swapaxes(jnp.stack(planes, axis=0), 0, 1)
        wsum = (new_sums[g] + hist_sums).reshape(bt * t, POOL_GROUP)
        pooled.append(wsum * (1.0 / w) - u_ref[:, lanes])
    _mix_tail(pooled, a_ref[...], ga_ref[...], gp_ref[...], x_ref[...], wpool_ref, pscale_ref,
              wout_ref, gpost_ref, gpre_ref, x1_ref, h2_ref)
    keep = POOL_HIST - t
    pool_ref[0:keep] = hist_ref[POOL_HIST - keep:POOL_HIST]
    pool_ref[keep:POOL_HIST] = jnp.swapaxes(u3, 0, 1)


def _mix_sample(a2d, z2d, x2d, hist, wpool_bf, pscale, wout_bf, gpost, gpre, *, bt, t):
    m = x2d.shape[0]
    bm = bt * t
    row_spec = lambda col: pl.BlockSpec((bm, D_MODEL), lambda i: (i, col))
    return pl.pallas_call(
        functools.partial(_mix_sample_kernel, bt=bt, t=t, pos0=PAST_LEN),
        grid=(m // bm,),
        in_specs=[
            row_spec(0),
            row_spec(OFF_GA // D_MODEL),
            row_spec(OFF_GP // D_MODEL),
            pl.BlockSpec((bm, D_POOL), lambda i: (i, OFF_U // D_POOL)),
            pl.BlockSpec((POOL_HIST, bt, D_POOL), lambda i: (0, i, 0)),
            row_spec(0),
            _const_spec((len(POOL_WINDOWS), POOL_GROUP, POOL_OUT_GROUP)),
            _const_spec((1, D_MODEL)),
            _const_spec((D_MODEL, D_MODEL)),
            _const_spec((1, D_MODEL)),
            _const_spec((1, D_MODEL)),
        ],
        out_specs=[row_spec(0), row_spec(0),
                   pl.BlockSpec((POOL_HIST, bt, D_POOL), lambda i: (0, i, 0))],
        out_shape=[jax.ShapeDtypeStruct((m, D_MODEL), F32),
                   jax.ShapeDtypeStruct((m, D_MODEL), BF16),
                   jax.ShapeDtypeStruct((POOL_HIST, m // t, D_POOL), F32)],
        compiler_params=pltpu.CompilerParams(
            dimension_semantics=("parallel",), vmem_limit_bytes=VMEM_LIMIT),
        name="mix_sample",
    )(a2d, z2d, z2d, z2d, hist, x2d, wpool_bf, pscale, wout_bf, gpost, gpre)


MLP_TAIL_ROWS = 16


def _mlp_kernel(h2_ref, x1_slice_ref, wup_ref, wdn_ref, g_ref, y_ref, x1_ref, *, bm, slice_rows):
    j = pl.program_id(1)

    x1_ref[pl.ds(pl.multiple_of(j * slice_rows, slice_rows), slice_rows), :] = x1_slice_ref[...]

    def chunk(first):
        hid = jnp.dot(h2_ref[...], wup_ref[...], preferred_element_type=F32)
        hid = jnp.square(jnp.maximum(hid, 0.0)).astype(BF16)
        part = jnp.dot(hid, wdn_ref[...], preferred_element_type=F32)
        if first:
            y_ref[...] = part
        else:
            y_ref[...] += part

    pl.when(j == 0)(functools.partial(chunk, True))
    pl.when(j > 0)(functools.partial(chunk, False))

    @pl.when(j == pl.num_programs(1) - 1)
    def _():
        for c in range(bm // MLP_TAIL_ROWS):
            rows = pl.ds(c * MLP_TAIL_ROWS, MLP_TAIL_ROWS)
            y_ref[rows, :] = x1_ref[rows, :] + _rmsnorm(y_ref[rows, :], g_ref[...])


def _mlp_cast_kernel(h2_ref, x1_ref, wup_ref, wdn_ref, g_ref, y_ref, wup_bf_ref, wdn_bf_ref):
    j = pl.program_id(0)

    @pl.when(j == 0)
    def _():
        y_ref[...] = jnp.zeros_like(y_ref)

    wup = wup_ref[...].astype(BF16)
    wdn = wdn_ref[...].astype(BF16)
    wup_bf_ref[...] = wup
    wdn_bf_ref[...] = wdn
    hid = jnp.dot(h2_ref[...], wup, preferred_element_type=F32)
    hid = jnp.square(jnp.maximum(hid, 0.0)).astype(BF16)
    y_ref[...] += jnp.dot(hid, wdn, preferred_element_type=F32)

    @pl.when(j == pl.num_programs(0) - 1)
    def _():
        for c in range(y_ref.shape[0] // MLP_TAIL_ROWS):
            rows = pl.ds(c * MLP_TAIL_ROWS, MLP_TAIL_ROWS)
            y_ref[rows, :] = x1_ref[rows, :] + _rmsnorm(y_ref[rows, :], g_ref[...])


def _mlp_cast(h2, x1, wup_f32, wdn_f32, g, *, fc):
    m = x1.shape[0]
    return pl.pallas_call(
        _mlp_cast_kernel,
        grid=(D_FF // fc,),
        in_specs=[
            _const_spec((m, D_MODEL)),
            _const_spec((m, D_MODEL)),
            pl.BlockSpec((D_MODEL, fc), lambda j: (0, j)),
            pl.BlockSpec((fc, D_MODEL), lambda j: (j, 0)),
            _const_spec((1, D_MODEL)),
        ],
        out_specs=[pl.BlockSpec((m, D_MODEL), lambda j: (0, 0)),
                   pl.BlockSpec((D_MODEL, fc), lambda j: (0, j)),
                   pl.BlockSpec((fc, D_MODEL), lambda j: (j, 0))],
        out_shape=[jax.ShapeDtypeStruct((m, D_MODEL), F32),
                   jax.ShapeDtypeStruct((D_MODEL, D_FF), BF16),
                   jax.ShapeDtypeStruct((D_FF, D_MODEL), BF16)],
        compiler_params=pltpu.CompilerParams(
            dimension_semantics=("arbitrary",), vmem_limit_bytes=VMEM_LIMIT),
        name="mlp_cast",
    )(h2, x1, wup_f32, wdn_f32, g)


def _mlp(h2, x1, wup_bf, wdn_bf, g, *, bm, fc):
    m = x1.shape[0]
    n_ff = D_FF // fc
    slice_rows = bm // n_ff
    return pl.pallas_call(
        functools.partial(_mlp_kernel, bm=bm, slice_rows=slice_rows),
        grid=(m // bm, n_ff),
        in_specs=[
            pl.BlockSpec((bm, D_MODEL), lambda i, j: (i, 0)),
            pl.BlockSpec((slice_rows, D_MODEL), lambda i, j: (i * n_ff + j, 0)),
            pl.BlockSpec((D_MODEL, fc), lambda i, j: (0, j)),
            pl.BlockSpec((fc, D_MODEL), lambda i, j: (j, 0)),
            _const_spec((1, D_MODEL)),
        ],
        out_specs=pl.BlockSpec((bm, D_MODEL), lambda i, j: (i, 0)),
        out_shape=jax.ShapeDtypeStruct((m, D_MODEL), F32),
        scratch_shapes=[pltpu.VMEM((bm, D_MODEL), F32)],
        compiler_params=pltpu.CompilerParams(
            dimension_semantics=("parallel", "arbitrary"), vmem_limit_bytes=VMEM_LIMIT),
        name="mlp",
    )(h2, x1, wup_bf, wdn_bf, g)


def kernel(x_prompt, x_sample, cache_k_win, cache_v_win, state_pool, norm_attn_pre, norm_attn_post,
           w_in, attn_sinks, w_pool, pool_scale, w_out, norm_mlp_pre, norm_mlp_post, w_up, w_down):
    depth = w_in.shape[0]
    assert depth == 1
    b, s, _ = x_prompt.shape
    bs, t, _ = x_sample.shape

    l = 0
    w_pool_bf = w_pool[l].astype(BF16)
    w_out_bf = w_out[l].astype(BF16)
    row = lambda v: v[l].reshape(1, D_MODEL)
    g_attn_pre, g_attn_post = row(norm_attn_pre), row(norm_attn_post)
    g_mlp_pre, g_mlp_post = row(norm_mlp_pre), row(norm_mlp_post)
    pscale = row(pool_scale)
    sinks = attn_sinks[l]

    xs = x_sample.reshape(bs * t, D_MODEL)
    zs, w_in_bf = _in_proj_cast(xs, g_attn_pre, w_in[l])
    stored_order = lambda c: jnp.transpose(c, (0, 2, 3, 1)).reshape(bs, D_KV, WINDOW)
    as_, ks, vs = _attn_sample(zs.reshape(bs, t, D_IN), stored_order(cache_k_win[l]),
                               stored_order(cache_v_win[l]), sinks, bt=Tiles.ATTN_SAMPLE_SEQS)
    hist_rows = jnp.transpose(state_pool[l], (1, 0, 2))
    x1s, h2s, pools = _mix_sample(as_.reshape(bs * t, D_Q), zs, xs, hist_rows, w_pool_bf, pscale,
                                  w_out_bf, g_attn_post, g_mlp_pre, bt=Tiles.MIX_SAMPLE_SEQS, t=t)
    ys, w_up_bf, w_down_bf = _mlp_cast(h2s, x1s, w_up[l], w_down[l], g_mlp_post,
                                       fc=Tiles.MLP_CAST_FF_COLS)

    xp = x_prompt.reshape(b * s, D_MODEL)
    zp = _in_proj(xp, g_attn_pre, w_in_bf, bm=Tiles.IN_PROJ_ROWS, bn=Tiles.IN_PROJ_COLS)
    ap, kp, vp = _attn_prompt(zp.reshape(b, s, D_IN), sinks)
    x1p, h2p, poolp = _mix_prompt(ap.reshape(b * s, D_Q), zp, xp, w_pool_bf, pscale, w_out_bf,
                                  g_attn_post, g_mlp_pre, bm=Tiles.MIX_ROWS, seq=s)
    yp = _mlp(h2p, x1p, w_up_bf, w_down_bf, g_mlp_post, bm=Tiles.MLP_ROWS, fc=Tiles.MLP_FF_COLS)

    kv_shape = lambda nb: (1, nb, WINDOW, N_KV_HEADS, HEAD_DIM)
    return (yp.reshape(b, s, D_MODEL), ys.reshape(bs, t, D_MODEL),
            kp.reshape(kv_shape(b)), vp.reshape(kv_shape(b)), poolp[None],
            ks.reshape(kv_shape(bs)), vs.reshape(kv_shape(bs)),
            jnp.transpose(pools, (1, 0, 2))[None])
```

```python
import functools

import jax
import jax.numpy as jnp
from jax import lax
from jax.experimental import pallas as pl
from jax.experimental.pallas import tpu as pltpu

F32 = jnp.float32
BF16 = jnp.bfloat16

D_MODEL = 2048
HEAD_DIM = 64
N_KV_HEADS = 4
GQA_GROUP = 8
N_Q_HEADS = N_KV_HEADS * GQA_GROUP
WINDOW = 128
D_Q = 2048
D_KV = 256
POOL_WINDOWS = (2, 4, 8, 16)
D_POOL = 1024
POOL_GROUP = 256
POOL_OUT_GROUP = 512
POOL_HIST = 15
D_FF = 8192
D_IN = D_Q + 2 * D_KV + D_POOL + 2 * D_MODEL
PAST_LEN = 8192
EPS = 1e-6
NEG_INF = -1e30
SCALE = HEAD_DIM ** -0.5
LOG2E = 1.4426950408889634

LANES = 128
PAIR = 2
N_PAIRS = GQA_GROUP // PAIR
ROW_CHUNK = 16

OFF_Q, OFF_GA, OFF_GP, OFF_U, OFF_K, OFF_V = 0, 2048, 4096, 6144, 7168, 7424
HALO = 16
POOL_PAD = 8

VMEM_LIMIT = 60 * 1024 * 1024
MXU_TILE = 256


class Tiles:
    IN_PROJ_ROWS, IN_PROJ_COLS = 1024, 6 * MXU_TILE
    IN_PROJ_CAST_COLS = 2 * MXU_TILE
    MIX_ROWS = 256
    MLP_ROWS, MLP_FF_COLS = 1024, 4 * MXU_TILE
    MLP_CAST_FF_COLS = 2 * MXU_TILE
    ATTN_SAMPLE_SEQS = 16
    ATTN_SAMPLE_UNROLL = 8
    MIX_SAMPLE_SEQS = 32

NT_DIMS = (((1,), (1,)), ((), ()))
TN_DIMS = (((0,), (0,)), ((), ()))


def _rmsnorm(x, g):
    r = lax.rsqrt(jnp.mean(x * x, axis=-1, keepdims=True) + EPS)
    return x * r * g


def _const_spec(shape):
    return pl.BlockSpec(shape, lambda *_: (0,) * len(shape), pipeline_mode=pl.Buffered(1))


def _in_proj_kernel(x_ref, g_ref, w_ref, z_ref, h_ref):
    @pl.when(pl.program_id(1) == 0)
    def _():
        h_ref[...] = _rmsnorm(x_ref[...], g_ref[...]).astype(BF16)

    z_ref[...] = jnp.dot(h_ref[...], w_ref[...], preferred_element_type=F32)


def _in_proj_cast_kernel(x_ref, g_ref, w_ref, z_ref, wbf_ref, h_ref):
    @pl.when(pl.program_id(0) == 0)
    def _():
        h_ref[...] = _rmsnorm(x_ref[...], g_ref[...]).astype(BF16)

    w = w_ref[...].astype(BF16)
    wbf_ref[...] = w
    z_ref[...] = jnp.dot(h_ref[...], w, preferred_element_type=F32)


def _in_proj_cast(x2d, g, w_f32):
    m = x2d.shape[0]
    bn = Tiles.IN_PROJ_CAST_COLS
    src_bounds = (D_Q, D_Q + 2 * D_KV, D_Q + 2 * D_KV + D_POOL, D_Q + 2 * D_KV + D_POOL + D_MODEL)
    assert all(off % bn == 0 for off in src_bounds + (OFF_GA, OFF_GP, OFF_U, OFF_K))

    def reordered(j):
        q_end, kv_end, u_end, ga_end = (off // bn for off in src_bounds)
        return jnp.where(
            j < q_end, j, jnp.where(
                j < kv_end, OFF_K // bn + (j - q_end), jnp.where(
                    j < u_end, OFF_U // bn + (j - kv_end), jnp.where(
                        j < ga_end, OFF_GA // bn + (j - u_end), OFF_GP // bn + (j - ga_end)))))

    return pl.pallas_call(
        _in_proj_cast_kernel,
        grid=(D_IN // bn,),
        in_specs=[
            _const_spec((m, D_MODEL)),
            _const_spec((1, D_MODEL)),
            pl.BlockSpec((D_MODEL, bn), lambda j: (0, j)),
        ],
        out_specs=[pl.BlockSpec((m, bn), lambda j: (0, reordered(j))),
                   pl.BlockSpec((D_MODEL, bn), lambda j: (0, reordered(j)))],
        out_shape=[jax.ShapeDtypeStruct((m, D_IN), F32),
                   jax.ShapeDtypeStruct((D_MODEL, D_IN), BF16)],
        scratch_shapes=[pltpu.VMEM((m, D_MODEL), BF16)],
        compiler_params=pltpu.CompilerParams(
            dimension_semantics=("arbitrary",), vmem_limit_bytes=VMEM_LIMIT),
        name="in_proj_cast",
    )(x2d, g, w_f32)


def _in_proj(x2d, g, w_bf, *, bm, bn):
    m = x2d.shape[0]
    return pl.pallas_call(
        _in_proj_kernel,
        grid=(m // bm, D_IN // bn),
        in_specs=[
            pl.BlockSpec((bm, D_MODEL), lambda i, j: (i, 0)),
            _const_spec((1, D_MODEL)),
            pl.BlockSpec((D_MODEL, bn), lambda i, j: (0, j)),
        ],
        out_specs=pl.BlockSpec((bm, bn), lambda i, j: (i, j)),
        out_shape=jax.ShapeDtypeStruct((m, D_IN), F32),
        scratch_shapes=[pltpu.VMEM((bm, D_MODEL), BF16)],
        compiler_params=pltpu.CompilerParams(
            dimension_semantics=("parallel", "arbitrary"), vmem_limit_bytes=VMEM_LIMIT),
        name="in_proj",
    )(x2d, g, w_bf)


def _own_and_swapped(slab, odd, lo):
    own = jnp.where(lo != odd, slab, 0.0)
    swapped = pltpu.roll(own, HEAD_DIM, axis=1)
    return (swapped, own) if odd else (own, swapped)


ATTN_BLOCKS = 8
ATTN_BUFFERS = 8


def _attn_prompt_kernel(sinks_ref, q_ref, kp_ref, kc_ref, vp_ref, vc_ref,
                        a_ref, kwin_ref, vwin_ref, s_ref, p_ref, e_ref):
    n = pl.program_id(1)
    k_rows = jnp.concatenate([kp_ref[0], kc_ref[0]], axis=0)
    v_rows = jnp.concatenate([vp_ref[0], vc_ref[0]], axis=0)
    first_prev_bias = jnp.where(n > 0, 0.0, NEG_INF).astype(F32)
    lo_kv = lax.broadcasted_iota(jnp.int32, (2 * WINDOW, LANES), 1) < HEAD_DIM
    ones_lo = jnp.where(lo_kv, 1.0, 0.0)
    ones_hi = 1.0 - ones_lo
    lo_c = lax.broadcasted_iota(jnp.int32, (ROW_CHUNK, LANES), 1) < HEAD_DIM
    lane_c = lax.broadcasted_iota(jnp.int32, (ROW_CHUNK, LANES), 1)
    row_c = lax.broadcasted_iota(jnp.int32, (ROW_CHUNK, LANES), 0)

    for blk in range(ATTN_BLOCKS):
        k_all = k_rows[blk * WINDOW:(blk + 2) * WINDOW]
        v_all = v_rows[blk * WINDOW:(blk + 2) * WINDOW]
        q_rows = pl.ds(blk * WINDOW, WINDOW)
        for h in range(N_KV_HEADS):
            buf = (blk * N_KV_HEADS + h) % ATTN_BUFFERS
            odd = (h % PAIR) == 1
            kv_col = slice((h // PAIR) * LANES, (h // PAIR + 1) * LANES)
            k_l, k_r = _own_and_swapped(k_all[:, kv_col], odd, lo_kv)
            v_l, v_r = _own_and_swapped(v_all[:, kv_col], odd, lo_kv)
            wk = jnp.concatenate([k_l, k_r], axis=0).astype(BF16)
            vext = jnp.concatenate(
                [jnp.concatenate([v_l, ones_lo], axis=1),
                 jnp.concatenate([v_r, ones_hi], axis=1)], axis=0).astype(BF16)
            qh = jnp.concatenate(
                [q_ref[0, q_rows, pl.ds((h * N_PAIRS + j) * LANES, LANES)] for j in range(N_PAIRS)],
                axis=0)
            qh = (qh * (SCALE * LOG2E)).astype(BF16)
            s_ref[buf] = lax.dot_general(qh, wk, NT_DIMS, preferred_element_type=F32)

            for j in range(N_PAIRS):
                sinks = [sinks_ref[h, PAIR * j] * LOG2E, sinks_ref[h, PAIR * j + 1] * LOG2E]
                for c in range(WINDOW // ROW_CHUNK):
                    rows = pl.ds(j * WINDOW + c * ROW_CHUNK, ROW_CHUNK)
                    mask = lane_c <= (row_c + c * ROW_CHUNK)
                    e_parts = []
                    for gi in range(PAIR):
                        base = gi * 2 * WINDOW
                        cur = s_ref[buf, rows, pl.ds(base + WINDOW, WINDOW)]
                        prev = s_ref[buf, rows, pl.ds(base, WINDOW)]
                        if blk == 0:
                            prev = prev + first_prev_bias
                        s = jnp.where(mask, cur, prev)
                        m = jnp.maximum(jnp.max(s, axis=-1, keepdims=True), sinks[gi])
                        p = jnp.exp2(s - m)
                        p_ref[buf, rows, pl.ds(base, WINDOW)] = jnp.where(mask, 0.0, p).astype(BF16)
                        p_ref[buf, rows, pl.ds(base + WINDOW, WINDOW)] = (
                            jnp.where(mask, p, 0.0).astype(BF16))
                        e_parts.append(jnp.broadcast_to(jnp.exp2(sinks[gi] - m), (ROW_CHUNK, LANES)))
                    e_ref[buf, rows, :] = jnp.where(lo_c, e_parts[0], e_parts[1])

            o_ext = jnp.dot(p_ref[buf], vext, preferred_element_type=F32)
            o = o_ext[:, :LANES] / (o_ext[:, LANES:] + e_ref[buf])
            for j in range(N_PAIRS):
                a_ref[0, q_rows, pl.ds((h * N_PAIRS + j) * LANES, LANES)] = o[j * WINDOW:(j + 1) * WINDOW]

    @pl.when(n == pl.num_programs(1) - 1)
    def _():
        kwin_ref[0] = kc_ref[0, pl.ds((ATTN_BLOCKS - 1) * WINDOW, WINDOW), :]
        vwin_ref[0] = vc_ref[0, pl.ds((ATTN_BLOCKS - 1) * WINDOW, WINDOW), :]


def _attn_prompt(z3, sinks):
    b, s, _ = z3.shape
    step_rows = ATTN_BLOCKS * WINDOW
    kcol, vcol = OFF_K // D_KV, OFF_V // D_KV
    prev = lambda col: (lambda bi, n: (bi, jnp.maximum(n * ATTN_BLOCKS - 1, 0), col))
    cur = lambda col: (lambda bi, n: (bi, n, col))
    win_spec = pl.BlockSpec((1, WINDOW, D_KV), lambda bi, n: (bi, 0, 0))
    rows = N_PAIRS * WINDOW
    n_buf = ATTN_BUFFERS
    return pl.pallas_call(
        _attn_prompt_kernel,
        grid=(b, s // step_rows),
        in_specs=[
            pl.BlockSpec(memory_space=pltpu.SMEM),
            pl.BlockSpec((1, step_rows, D_Q), lambda bi, n: (bi, n, 0)),
            pl.BlockSpec((1, WINDOW, D_KV), prev(kcol)),
            pl.BlockSpec((1, step_rows, D_KV), cur(kcol)),
            pl.BlockSpec((1, WINDOW, D_KV), prev(vcol)),
            pl.BlockSpec((1, step_rows, D_KV), cur(vcol)),
        ],
        out_specs=[pl.BlockSpec((1, step_rows, D_Q), lambda bi, n: (bi, n, 0)), win_spec, win_spec],
        out_shape=[jax.ShapeDtypeStruct((b, s, D_Q), F32),
                   jax.ShapeDtypeStruct((b, WINDOW, D_KV), F32),
                   jax.ShapeDtypeStruct((b, WINDOW, D_KV), F32)],
        scratch_shapes=[pltpu.VMEM((n_buf, rows, PAIR * 2 * WINDOW), F32),
                        pltpu.VMEM((n_buf, rows, PAIR * 2 * WINDOW), BF16),
                        pltpu.VMEM((n_buf, rows, LANES), F32)],
        compiler_params=pltpu.CompilerParams(
            dimension_semantics=("parallel", "arbitrary"), vmem_limit_bytes=VMEM_LIMIT),
        name="attn_prompt",
    )(sinks, z3, z3, z3, z3, z3)


def _attn_sample_kernel(sinkrow_ref, q_ref, kn_ref, vn_ref, ck_ref, cv_ref,
                        a_ref, kwin_ref, vwin_ref, *, bt, t):
    pad_rows = 2 * WINDOW - WINDOW - t
    pad = jnp.zeros((pad_rows, D_KV), F32)
    p_pad = jnp.zeros((pad_rows, N_Q_HEADS * t), F32)
    zeros_col = jnp.zeros((t, LANES), F32)
    lo = lax.broadcasted_iota(jnp.int32, (t, LANES), 1) < HEAD_DIM
    tok = lax.broadcasted_iota(jnp.int32, (t, N_Q_HEADS * t), 1) % t
    key = lax.broadcasted_iota(jnp.int32, (t, N_Q_HEADS * t), 0)
    new_mask = key <= tok
    sinkrow = sinkrow_ref[...]

    def place(piece_col, src_odd, dst_odd):
        own = jnp.where(lo != src_odd, piece_col, 0.0)
        return own if src_odd == dst_odd else pltpu.roll(own, HEAD_DIM, axis=1)

    def body(s, carry):
        kn, vn = kn_ref[s], vn_ref[s]
        ck, cv = ck_ref[s].T, cv_ref[s].T
        k_all = jnp.concatenate([ck, kn, pad], axis=0).astype(BF16)
        v_all = jnp.concatenate([cv, vn, pad], axis=0).astype(BF16)
        q = q_ref[s] * SCALE
        blocks = []
        for h in range(N_KV_HEADS):
            for g in range(GQA_GROUP):
                head = h * GQA_GROUP + g
                piece = place(q[:, (head // PAIR) * LANES:(head // PAIR + 1) * LANES],
                              head % PAIR == 1, h % PAIR == 1)
                cols = [zeros_col] * (D_KV // LANES)
                cols[h // PAIR] = piece
                blocks.append(jnp.concatenate(cols, axis=1))
        wq_t = jnp.concatenate(blocks, axis=0).astype(BF16)
        s_t = lax.dot_general(k_all, wq_t, NT_DIMS, preferred_element_type=F32)
        top = jnp.where(new_mask, s_t[WINDOW:WINDOW + t], s_t[0:t])
        s_m = jnp.concatenate([top, s_t[t:WINDOW]], axis=0)
        m = jnp.maximum(jnp.max(s_m, axis=0, keepdims=True), sinkrow)
        p = jnp.exp(s_m - m)
        denom = jnp.sum(p, axis=0, keepdims=True) + jnp.exp(sinkrow - m)
        probs = p * (1.0 / denom)
        p_all = jnp.concatenate(
            [jnp.where(new_mask, 0.0, probs[0:t]), probs[t:WINDOW],
             jnp.where(new_mask, probs[0:t], 0.0), p_pad], axis=0).astype(BF16)
        o_full = lax.dot_general(p_all, v_all, TN_DIMS, preferred_element_type=F32)
        out_cols = []
        for c in range(N_Q_HEADS // PAIR):
            h = (PAIR * c) // GQA_GROUP
            kv_col = slice((h // PAIR) * LANES, (h // PAIR + 1) * LANES)
            even = place(o_full[(PAIR * c) * t:(PAIR * c + 1) * t, kv_col], h % PAIR == 1, False)
            odd = place(o_full[(PAIR * c + 1) * t:(PAIR * c + 2) * t, kv_col], h % PAIR == 1, True)
            out_cols.append(even + odd)
        a_ref[s] = jnp.concatenate(out_cols, axis=1)
        kwin_ref[s, 0:WINDOW - t, :] = ck[t:, :]
        kwin_ref[s, WINDOW - t:WINDOW, :] = kn
        vwin_ref[s, 0:WINDOW - t, :] = cv[t:, :]
        vwin_ref[s, WINDOW - t:WINDOW, :] = vn
        return carry

    lax.fori_loop(0, bt, body, 0, unroll=Tiles.ATTN_SAMPLE_UNROLL)


def _attn_sample(z3, cache_k, cache_v, sinks, *, bt):
    b, t, _ = z3.shape
    kcol, vcol = OFF_K // D_KV, OFF_V // D_KV
    cache_spec = pl.BlockSpec((bt, WINDOW, D_KV), lambda i: (i, 0, 0))
    cache_t_spec = pl.BlockSpec((bt, D_KV, WINDOW), lambda i: (i, 0, 0))
    sinkrow = jnp.repeat(sinks.reshape(1, N_Q_HEADS), t, axis=1)
    return pl.pallas_call(
        functools.partial(_attn_sample_kernel, bt=bt, t=t),
        grid=(b // bt,),
        in_specs=[
            _const_spec((1, N_Q_HEADS * t)),
            pl.BlockSpec((bt, t, D_Q), lambda i: (i, 0, 0)),
            pl.BlockSpec((bt, t, D_KV), lambda i: (i, 0, kcol)),
            pl.BlockSpec((bt, t, D_KV), lambda i: (i, 0, vcol)),
            cache_t_spec, cache_t_spec,
        ],
        out_specs=[pl.BlockSpec((bt, t, D_Q), lambda i: (i, 0, 0)), cache_spec, cache_spec],
        out_shape=[jax.ShapeDtypeStruct((b, t, D_Q), F32),
                   jax.ShapeDtypeStruct((b, WINDOW, D_KV), F32),
                   jax.ShapeDtypeStruct((b, WINDOW, D_KV), F32)],
        compiler_params=pltpu.CompilerParams(
            dimension_semantics=("parallel",), vmem_limit_bytes=VMEM_LIMIT),
        name="attn_sample",
    )(sinkrow, z3, z3, z3, cache_k, cache_v)


def _mix_tail(pooled_parts, a, ga, gp, x, wpool_ref, pscale_ref, wout_ref, gpost_ref, gpre_ref,
              x1_ref, h2_ref):
    parts = [jnp.dot(pooled_parts[g].astype(BF16), wpool_ref[g], preferred_element_type=F32)
             for g in range(len(POOL_WINDOWS))]
    p = jnp.concatenate(parts, axis=-1) * pscale_ref[...]
    mixed = jax.nn.sigmoid(ga) * a + jax.nn.sigmoid(gp) * p
    y = jnp.dot(mixed.astype(BF16), wout_ref[...], preferred_element_type=F32)
    x1 = x + _rmsnorm(y, gpost_ref[...])
    x1_ref[...] = x1
    h2_ref[...] = _rmsnorm(x1, gpre_ref[...]).astype(BF16)


MIX_RING = 3


def _mix_prompt_kernel(a_hbm, z_hbm, x_hbm, halo_ref, wpool_ref, pscale_ref,
                       wout_ref, gpost_ref, gpre_ref, x1_ref, h2_ref, pool_ref,
                       ext_ref, sum2_ref, sum4_ref, sum8_ref,
                       a_buf, ga_buf, gp_buf, u_buf, x_buf, sem, *, bm, blocks_per_seq):
    step = pl.program_id(0)
    n_steps = pl.num_programs(0)

    def row_block_copies(block):
        slot = block % MIX_RING
        rows = pl.ds(block * bm, bm)
        z_cols = lambda off, width: z_hbm.at[rows, pl.ds(off, width)]
        pairs = [(a_hbm.at[rows, :], a_buf), (z_cols(OFF_GA, D_MODEL), ga_buf),
                 (z_cols(OFF_GP, D_MODEL), gp_buf), (z_cols(OFF_U, D_POOL), u_buf),
                 (x_hbm.at[rows, :], x_buf)]
        return [pltpu.make_async_copy(src, buf.at[slot], sem.at[slot, k])
                for k, (src, buf) in enumerate(pairs)]

    @pl.when(step == 0)
    def _():
        for block in range(MIX_RING - 1):
            for k, copy in enumerate(row_block_copies(block)):
                copy.start(priority=k % 2)

    @pl.when(step + MIX_RING - 1 < n_steps)
    def _():
        for k, copy in enumerate(row_block_copies(step + MIX_RING - 1)):
            copy.start(priority=k % 2)

    for copy in row_block_copies(step):
        copy.wait()
    slot = step % MIX_RING
    u_ref = u_buf.at[slot]

    assert POOL_WINDOWS == (2, 4, 8, 16)
    blk = step % blocks_per_seq
    pad = jnp.zeros((POOL_PAD, D_POOL), F32)
    ext_ref[0:POOL_PAD, :] = pad
    sum2_ref[0:POOL_PAD, :] = pad
    sum4_ref[0:POOL_PAD, :] = pad[:, POOL_GROUP:]
    ext_ref[POOL_PAD:POOL_PAD + HALO, :] = jnp.where(blk == 0, 0.0, halo_ref[...])
    ext_ref[POOL_PAD + HALO:, :] = u_ref[...]
    n = HALO + bm
    body = pl.ds(POOL_PAD, n)
    shifted = lambda s: pl.ds(POOL_PAD - s, n)
    g1 = pl.ds(POOL_GROUP, D_POOL - POOL_GROUP)
    sum2_ref[body, :] = ext_ref[body, :] + ext_ref[shifted(1), :]
    sum4_ref[body, :] = sum2_ref[body, g1] + sum2_ref[shifted(2), g1]
    sum8_ref[body, :] = sum4_ref[body, POOL_GROUP:] + sum4_ref[shifted(4), POOL_GROUP:]
    first = POOL_PAD + HALO
    rows = pl.ds(first, bm)
    sum16 = sum8_ref[rows, POOL_GROUP:] + sum8_ref[pl.ds(first - 8, bm), POOL_GROUP:]
    wsums = [sum2_ref[rows, 0:POOL_GROUP], sum4_ref[rows, 0:POOL_GROUP],
             sum8_ref[rows, 0:POOL_GROUP], sum16]
    pos = blk * bm + lax.broadcasted_iota(jnp.int32, (bm, 1), 0)
    pooled = []
    for g, w in enumerate(POOL_WINDOWS):
        inv_cnt = 1.0 / jnp.minimum(pos + 1, w).astype(F32)
        pooled.append(wsums[g] * inv_cnt - u_ref[:, pl.ds(g * POOL_GROUP, POOL_GROUP)])
    _mix_tail(pooled, a_buf[slot], ga_buf[slot], gp_buf[slot], x_buf[slot], wpool_ref, pscale_ref,
              wout_ref, gpost_ref, gpre_ref, x1_ref, h2_ref)

    @pl.when(blk == blocks_per_seq - 1)
    def _():
        pool_ref[0] = ext_ref[pl.ds(first + bm - POOL_HIST, POOL_HIST), :]


def _mix_prompt(a2d, z2d, x2d, wpool_bf, pscale, wout_bf, gpost, gpre, *, bm, seq):
    m = x2d.shape[0]
    blocks_per_seq = seq // bm
    row_spec = lambda col: pl.BlockSpec((bm, D_MODEL), lambda i: (i, col))
    halo_blocks = bm // HALO
    assert m // bm >= MIX_RING
    in_hbm = pl.BlockSpec(memory_space=pl.ANY)
    ring = lambda width: pltpu.VMEM((MIX_RING, bm, width), F32)
    return pl.pallas_call(
        functools.partial(_mix_prompt_kernel, bm=bm, blocks_per_seq=blocks_per_seq),
        grid=(m // bm,),
        in_specs=[
            in_hbm,
            in_hbm,
            in_hbm,
            pl.BlockSpec((HALO, D_POOL),
                         lambda i: (jnp.maximum(i * halo_blocks - 1, 0), OFF_U // D_POOL)),
            _const_spec((len(POOL_WINDOWS), POOL_GROUP, POOL_OUT_GROUP)),
            _const_spec((1, D_MODEL)),
            _const_spec((D_MODEL, D_MODEL)),
            _const_spec((1, D_MODEL)),
            _const_spec((1, D_MODEL)),
        ],
        out_specs=[row_spec(0), row_spec(0),
                   pl.BlockSpec((1, POOL_HIST, D_POOL), lambda i: (i // blocks_per_seq, 0, 0))],
        out_shape=[jax.ShapeDtypeStruct((m, D_MODEL), F32),
                   jax.ShapeDtypeStruct((m, D_MODEL), BF16),
                   jax.ShapeDtypeStruct((m // seq, POOL_HIST, D_POOL), F32)],
        scratch_shapes=[pltpu.VMEM((POOL_PAD + HALO + bm, D_POOL), F32),
                        pltpu.VMEM((POOL_PAD + HALO + bm, D_POOL), F32),
                        pltpu.VMEM((POOL_PAD + HALO + bm, D_POOL - POOL_GROUP), F32),
                        pltpu.VMEM((POOL_PAD + HALO + bm, D_POOL - 2 * POOL_GROUP), F32),
                        ring(D_MODEL), ring(D_MODEL), ring(D_MODEL), ring(D_POOL), ring(D_MODEL),
                        pltpu.SemaphoreType.DMA((MIX_RING, 5))],
        compiler_params=pltpu.CompilerParams(
            dimension_semantics=("arbitrary",), vmem_limit_bytes=VMEM_LIMIT),
        name="mix_prompt",
    )(a2d, z2d, x2d, z2d, wpool_bf, pscale, wout_bf, gpost, gpre)


def _mix_sample_kernel(a_ref, ga_ref, gp_ref, u_ref, hist_ref, x_ref, wpool_ref, pscale_ref,
                       wout_ref, gpost_ref, gpre_ref, x1_ref, h2_ref, pool_ref, *, bt, t, pos0):
    assert t == 8 and max(POOL_WINDOWS) >= t and pos0 + 1 >= max(POOL_WINDOWS)
    u3 = u_ref[...].reshape(bt, t, D_POOL)
    def shifted(x, k):
        tok = lax.broadcasted_iota(jnp.int32, x.shape, 1)
        return jnp.where(tok >= k, pltpu.roll(x, k, axis=1), 0.0)

    sum2 = u3 + shifted(u3, 1)
    sum4 = sum2[..., POOL_GROUP:] + shifted(sum2[..., POOL_GROUP:], 2)
    sum8 = sum4[..., POOL_GROUP:] + shifted(sum4[..., POOL_GROUP:], 4)
    new_sums = [sum2[..., :POOL_GROUP], sum4[..., :POOL_GROUP], sum8[..., :POOL_GROUP],
                sum8[..., POOL_GROUP:]]

    zero_plane = jnp.zeros((bt, POOL_GROUP), F32)
    pooled = []
    for g, w in enumerate(POOL_WINDOWS):
        lanes = pl.ds(g * POOL_GROUP, POOL_GROUP)
        planes = [zero_plane] * t
        suffix = None
        for r in range(POOL_HIST - 1, -1, -1):
            token = r - (POOL_HIST + 1 - w)
            if token < 0:
                break
            row = hist_ref[r, :, lanes]
            suffix = row if suffix is None else suffix + row
            if token < t:
                planes[token] = suffix
        hist_sums = jnp.swapaxes(jnp.stack(planes, axis=0), 0, 1)
        wsum = (new_sums[g] + hist_sums).reshape(bt * t, POOL_GROUP)
        pooled.append(wsum * (1.0 / w) - u_ref[:, lanes])
    _mix_tail(pooled, a_ref[...], ga_ref[...], gp_ref[...], x_ref[...], wpool_ref, pscale_ref,
              wout_ref, gpost_ref, gpre_ref, x1_ref, h2_ref)
    keep = POOL_HIST - t
    pool_ref[0:keep] = hist_ref[POOL_HIST - keep:POOL_HIST]
    pool_ref[keep:POOL_HIST] = jnp.swapaxes(u3, 0, 1)


def _mix_sample(a2d, z2d, x2d, hist, wpool_bf, pscale, wout_bf, gpost, gpre, *, bt, t):
    m = x2d.shape[0]
    bm = bt * t
    row_spec = lambda col: pl.BlockSpec((bm, D_MODEL), lambda i: (i, col))
    return pl.pallas_call(
        functools.partial(_mix_sample_kernel, bt=bt, t=t, pos0=PAST_LEN),
        grid=(m // bm,),
        in_specs=[
            row_spec(0),
            row_spec(OFF_GA // D_MODEL),
            row_spec(OFF_GP // D_MODEL),
            pl.BlockSpec((bm, D_POOL), lambda i: (i, OFF_U // D_POOL)),
            pl.BlockSpec((POOL_HIST, bt, D_POOL), lambda i: (0, i, 0)),
            row_spec(0),
            _const_spec((len(POOL_WINDOWS), POOL_GROUP, POOL_OUT_GROUP)),
            _const_spec((1, D_MODEL)),
            _const_spec((D_MODEL, D_MODEL)),
            _const_spec((1, D_MODEL)),
            _const_spec((1, D_MODEL)),
        ],
        out_specs=[row_spec(0), row_spec(0),
                   pl.BlockSpec((POOL_HIST, bt, D_POOL), lambda i: (0, i, 0))],
        out_shape=[jax.ShapeDtypeStruct((m, D_MODEL), F32),
                   jax.ShapeDtypeStruct((m, D_MODEL), BF16),
                   jax.ShapeDtypeStruct((POOL_HIST, m // t, D_POOL), F32)],
        compiler_params=pltpu.CompilerParams(
            dimension_semantics=("parallel",), vmem_limit_bytes=VMEM_LIMIT),
        name="mix_sample",
    )(a2d, z2d, z2d, z2d, hist, x2d, wpool_bf, pscale, wout_bf, gpost, gpre)


MLP_TAIL_ROWS = 16


def _mlp_kernel(h2_ref, x1_slice_ref, wup_ref, wdn_ref, g_ref, y_ref, x1_ref, *, bm, slice_rows):
    j = pl.program_id(1)

    x1_ref[pl.ds(pl.multiple_of(j * slice_rows, slice_rows), slice_rows), :] = x1_slice_ref[...]

    def chunk(first):
        hid = jnp.dot(h2_ref[...], wup_ref[...], preferred_element_type=F32)
        hid = jnp.square(jnp.maximum(hid, 0.0)).astype(BF16)
        part = jnp.dot(hid, wdn_ref[...], preferred_element_type=F32)
        if first:
            y_ref[...] = part
        else:
            y_ref[...] += part

    pl.when(j == 0)(functools.partial(chunk, True))
    pl.when(j > 0)(functools.partial(chunk, False))

    @pl.when(j == pl.num_programs(1) - 1)
    def _():
        for c in range(bm // MLP_TAIL_ROWS):
            rows = pl.ds(c * MLP_TAIL_ROWS, MLP_TAIL_ROWS)
            y_ref[rows, :] = x1_ref[rows, :] + _rmsnorm(y_ref[rows, :], g_ref[...])


def _mlp_cast_kernel(h2_ref, x1_ref, wup_ref, wdn_ref, g_ref, y_ref, wup_bf_ref, wdn_bf_ref):
    j = pl.program_id(0)

    @pl.when(j == 0)
    def _():
        y_ref[...] = jnp.zeros_like(y_ref)

    wup = wup_ref[...].astype(BF16)
    wdn = wdn_ref[...].astype(BF16)
    wup_bf_ref[...] = wup
    wdn_bf_ref[...] = wdn
    hid = jnp.dot(h2_ref[...], wup, preferred_element_type=F32)
    hid = jnp.square(jnp.maximum(hid, 0.0)).astype(BF16)
    y_ref[...] += jnp.dot(hid, wdn, preferred_element_type=F32)

    @pl.when(j == pl.num_programs(0) - 1)
    def _():
        y_ref[...] = x1_ref[...] + _rmsnorm(y_ref[...], g_ref[...])


def _mlp_cast(h2, x1, wup_f32, wdn_f32, g, *, fc):
    m = x1.shape[0]
    return pl.pallas_call(
        _mlp_cast_kernel,
        grid=(D_FF // fc,),
        in_specs=[
            _const_spec((m, D_MODEL)),
            _const_spec((m, D_MODEL)),
            pl.BlockSpec((D_MODEL, fc), lambda j: (0, j)),
            pl.BlockSpec((fc, D_MODEL), lambda j: (j, 0)),
            _const_spec((1, D_MODEL)),
        ],
        out_specs=[pl.BlockSpec((m, D_MODEL), lambda j: (0, 0)),
                   pl.BlockSpec((D_MODEL, fc), lambda j: (0, j)),
                   pl.BlockSpec((fc, D_MODEL), lambda j: (j, 0))],
        out_shape=[jax.ShapeDtypeStruct((m, D_MODEL), F32),
                   jax.ShapeDtypeStruct((D_MODEL, D_FF), BF16),
                   jax.ShapeDtypeStruct((D_FF, D_MODEL), BF16)],
        compiler_params=pltpu.CompilerParams(
            dimension_semantics=("arbitrary",), vmem_limit_bytes=VMEM_LIMIT),
        name="mlp_cast",
    )(h2, x1, wup_f32, wdn_f32, g)


def _mlp(h2, x1, wup_bf, wdn_bf, g, *, bm, fc):
    m = x1.shape[0]
    n_ff = D_FF // fc
    slice_rows = bm // n_ff
    return pl.pallas_call(
        functools.partial(_mlp_kernel, bm=bm, slice_rows=slice_rows),
        grid=(m // bm, n_ff),
        in_specs=[
            pl.BlockSpec((bm, D_MODEL), lambda i, j: (i, 0)),
            pl.BlockSpec((slice_rows, D_MODEL), lambda i, j: (i * n_ff + j, 0)),
            pl.BlockSpec((D_MODEL, fc), lambda i, j: (0, j)),
            pl.BlockSpec((fc, D_MODEL), lambda i, j: (j, 0)),
            _const_spec((1, D_MODEL)),
        ],
        out_specs=pl.BlockSpec((bm, D_MODEL), lambda i, j: (i, 0)),
        out_shape=jax.ShapeDtypeStruct((m, D_MODEL), F32),
        scratch_shapes=[pltpu.VMEM((bm, D_MODEL), F32)],
        compiler_params=pltpu.CompilerParams(
            dimension_semantics=("parallel", "arbitrary"), vmem_limit_bytes=VMEM_LIMIT),
        name="mlp",
    )(h2, x1, wup_bf, wdn_bf, g)


def kernel(x_prompt, x_sample, cache_k_win, cache_v_win, state_pool, norm_attn_pre, norm_attn_post,
           w_in, attn_sinks, w_pool, pool_scale, w_out, norm_mlp_pre, norm_mlp_post, w_up, w_down):
    depth = w_in.shape[0]
    assert depth == 1
    b, s, _ = x_prompt.shape
    bs, t, _ = x_sample.shape

    l = 0
    w_pool_bf = w_pool[l].astype(BF16)
    w_out_bf = w_out[l].astype(BF16)
    row = lambda v: v[l].reshape(1, D_MODEL)
    g_attn_pre, g_attn_post = row(norm_attn_pre), row(norm_attn_post)
    g_mlp_pre, g_mlp_post = row(norm_mlp_pre), row(norm_mlp_post)
    pscale = row(pool_scale)
    sinks = attn_sinks[l]

    xs = x_sample.reshape(bs * t, D_MODEL)
    zs, w_in_bf = _in_proj_cast(xs, g_attn_pre, w_in[l])
    stored_order = lambda c: jnp.transpose(c, (0, 2, 3, 1)).reshape(bs, D_KV, WINDOW)
    as_, ks, vs = _attn_sample(zs.reshape(bs, t, D_IN), stored_order(cache_k_win[l]),
                               stored_order(cache_v_win[l]), sinks, bt=Tiles.ATTN_SAMPLE_SEQS)
    hist_rows = jnp.transpose(state_pool[l], (1, 0, 2))
    x1s, h2s, pools = _mix_sample(as_.reshape(bs * t, D_Q), zs, xs, hist_rows, w_pool_bf, pscale,
                                  w_out_bf, g_attn_post, g_mlp_pre, bt=Tiles.MIX_SAMPLE_SEQS, t=t)
    ys, w_up_bf, w_down_bf = _mlp_cast(h2s, x1s, w_up[l], w_down[l], g_mlp_post,
                                       fc=Tiles.MLP_CAST_FF_COLS)

    xp = x_prompt.reshape(b * s, D_MODEL)
    zp = _in_proj(xp, g_attn_pre, w_in_bf, bm=Tiles.IN_PROJ_ROWS, bn=Tiles.IN_PROJ_COLS)
    ap, kp, vp = _attn_prompt(zp.reshape(b, s, D_IN), sinks)
    x1p, h2p, poolp = _mix_prompt(ap.reshape(b * s, D_Q), zp, xp, w_pool_bf, pscale, w_out_bf,
                                  g_attn_post, g_mlp_pre, bm=Tiles.MIX_ROWS, seq=s)
    yp = _mlp(h2p, x1p, w_up_bf, w_down_bf, g_mlp_post, bm=Tiles.MLP_ROWS, fc=Tiles.MLP_FF_COLS)

    kv_shape = lambda nb: (1, nb, WINDOW, N_KV_HEADS, HEAD_DIM)
    return (yp.reshape(b, s, D_MODEL), ys.reshape(bs, t, D_MODEL),
            kp.reshape(kv_shape(b)), vp.reshape(kv_shape(b)), poolp[None],
            ks.reshape(kv_shape(bs)), vs.reshape(kv_shape(bs)),
            jnp.transpose(pools, (1, 0, 2))[None])
```
